```python
import math
import jax, jax.numpy as jnp
from jax import lax
import numpy as np

D_MODEL = 2048
BATCH = 2
SEQ = 4096
DEPTH = 4
DEC_BATCH = 32
DEC_SEQ = 8
PAST_LEN = 16384
PAGE_SIZE = 128

NORM_EPS = 1e-6
N_MIXERS = 3
EXPAND = 2
D_INNER = EXPAND * D_MODEL
N_A = (DEPTH + 2) // N_MIXERS
N_B = (DEPTH + 1) // N_MIXERS
N_C = DEPTH // N_MIXERS

A_HEAD_DIM = 64
A_HEADS = D_INNER // A_HEAD_DIM
A_KV_HEADS = 8
A_GROUP = A_HEADS // A_KV_HEADS
A_Q = A_HEADS * A_HEAD_DIM
A_KV = A_KV_HEADS * A_HEAD_DIM
A_IN = A_Q + 2 * A_KV + D_INNER
A_SCALE = A_HEAD_DIM ** -0.5
WINDOW = 128
A_BLOCK = WINDOW
N_BUCKETS = 32
MAX_EXACT = N_BUCKETS // 2
MAX_DISTANCE = 128

B_HEAD = 64
B_HEADS = D_INNER // B_HEAD
B_LORA = max(32, int(round(1.8 * D_MODEL ** 0.5 / 32)) * 32)
B_LN_EPS = 64e-5
N_LERP = 6

C_HEADS = 8
C_KEY = D_INNER // 2
C_DK = C_KEY // C_HEADS
C_DV = D_INNER // C_HEADS
C_GATE_RANK = 16
C_GATE_NORM = 16.0
C_CHUNK = 64
C_IN = 2 * C_KEY + 2 * D_INNER + C_GATE_RANK

kernel_name = 'hybrid_swa_rwkv7_gla_decode_step'


def rmsnorm(x, g, eps=NORM_EPS):
    xf = x.astype(jnp.float32)
    y = xf * lax.rsqrt(jnp.mean(xf * xf, axis=-1, keepdims=True) + eps)
    return (y * g.astype(jnp.float32)).astype(x.dtype)


def t5_bucket(dist):
    d = jnp.maximum(dist, 0)
    large = MAX_EXACT + (jnp.log(jnp.maximum(d, 1).astype(jnp.float32) / MAX_EXACT)
                         / math.log(MAX_DISTANCE / MAX_EXACT) * (N_BUCKETS - MAX_EXACT)).astype(jnp.int32)
    return jnp.where(d < MAX_EXACT, d, jnp.minimum(large, N_BUCKETS - 1))


def sink_attention(q, k, v, bias, valid, sinks):
    s = jnp.einsum('bqkgd,bskd->bkgqs', q, k, preferred_element_type=jnp.float32) * A_SCALE
    b = bias.reshape(bias.shape[0], bias.shape[1], A_KV_HEADS, A_GROUP)
    s = s + jnp.transpose(b, (2, 3, 0, 1)).astype(jnp.float32)
    s = jnp.where(valid, s, -jnp.inf)
    sink = sinks.astype(jnp.float32).reshape(A_KV_HEADS, A_GROUP)[:, :, None, None]
    m = jnp.maximum(jnp.max(s, axis=-1, keepdims=True), sink)
    p = jnp.exp(s - m)
    p = p / (jnp.sum(p, axis=-1, keepdims=True) + jnp.exp(sink - m))
    return jnp.einsum('bkgqs,bskd->bqkgd', p.astype(v.dtype), v)


def banded_attention(q, k, v, rel_bias, sinks):
    B, T = q.shape[:2]
    nb = T // A_BLOCK

    def blocks(a):
        return jnp.moveaxis(a.reshape((B, nb, A_BLOCK) + a.shape[2:]), 1, 0)

    def with_prev(a):
        prev = jnp.concatenate([jnp.zeros_like(a[:, :A_BLOCK]), a[:, :T - A_BLOCK]], axis=1)
        return jnp.concatenate([blocks(prev), blocks(a)], axis=2)

    r = jnp.arange(A_BLOCK)[:, None]
    c = jnp.arange(2 * A_BLOCK)[None, :]
    dist = A_BLOCK + r - c
    band = (dist >= 0) & (dist < WINDOW)
    bias = rel_bias[t5_bucket(dist)]

    def one_block(args):
        i, qb, kb, vb = args
        valid = band & ((i > 0) | (c >= A_BLOCK))
        return sink_attention(qb, kb, vb, bias, valid, sinks)

    o = lax.map(one_block, (jnp.arange(nb), blocks(q), with_prev(k), with_prev(v)))
    return jnp.moveaxis(o, 0, 1).reshape(B, T, A_Q)


def attn_branch(h, w_in, q_g, k_g, sinks, w_out, rel_bias, cache_k=None, cache_v=None):
    B, T, _ = h.shape
    q, k, v, gate = jnp.split(h @ w_in, [A_Q, A_Q + A_KV, A_Q + 2 * A_KV], axis=-1)
    q = rmsnorm(q.reshape(B, T, A_KV_HEADS, A_GROUP, A_HEAD_DIM), q_g)
    k = rmsnorm(k.reshape(B, T, A_KV_HEADS, A_HEAD_DIM), k_g)
    v = v.reshape(B, T, A_KV_HEADS, A_HEAD_DIM)
    if cache_k is None:
        o = banded_attention(q, k, v, rel_bias, sinks)
        k_all, v_all = k, v
    else:
        k_all = jnp.concatenate([cache_k.astype(k.dtype), k], axis=1)
        v_all = jnp.concatenate([cache_v.astype(v.dtype), v], axis=1)
        q_pos = PAST_LEN + jnp.arange(T)
        k_pos = PAST_LEN - WINDOW + jnp.arange(WINDOW + T)
        dist = q_pos[:, None] - k_pos[None, :]
        valid = (dist >= 0) & (dist < WINDOW)
        o = sink_attention(q, k_all, v_all, rel_bias[t5_bucket(dist)], valid, sinks).reshape(B, T, A_Q)
    y = (o * jax.nn.silu(gate)) @ w_out
    return y, k_all[:, -WINDOW:], v_all[:, -WINDOW:]


def rwkv_scan(S0, r, decay, k, v, kk, a):
    def step(S, inp):
        r_t, w_t, k_t, v_t, kk_t, a_t = inp
        s_a = jnp.einsum('bhij,bhj->bhi', S, -kk_t)
        S = (S * w_t[:, :, None, :] + s_a[..., None] * (kk_t * a_t)[:, :, None, :]
             + v_t[..., None] * k_t[:, :, None, :])
        return S, jnp.einsum('bhij,bhj->bhi', S, r_t)
    S, o = lax.scan(step, S0, tuple(jnp.moveaxis(t, 1, 0) for t in (r, decay, k, v, kk, a)))
    return S, jnp.moveaxis(o, 0, 1)


def rwkv_branch(h, h_prev, S0, mu, w_rkvg, w_lora_down, w_lora_up, w0, a0, k_k, k_a, r_k,
                ln_g, ln_b, w_out):
    B, T, _ = h.shape
    f32 = jnp.float32
    xx = jnp.concatenate([h_prev[:, None].astype(h.dtype), h[:, :-1]], axis=1) - h
    xm = h[None] + xx[None] * mu[:, None, None, :].astype(h.dtype)
    r, k, v, g = jnp.einsum('cbtd,cde->cbte', xm[:4], w_rkvg)
    lw, la = jnp.einsum('cbtd,cdr->cbtr', xm[4:], w_lora_down)
    w_log = -jax.nn.softplus(-(w0 + jnp.tanh(lw) @ w_lora_up[0]).astype(f32)) - 0.5
    decay = jnp.exp(-jnp.exp(w_log))
    a = jax.nn.sigmoid((a0 + la @ w_lora_up[1]).astype(f32))
    r, k, v, decay, a = [t.astype(f32).reshape(B, T, B_HEADS, B_HEAD) for t in (r, k, v, decay, a)]
    kkr = k * k_k.astype(f32).reshape(B_HEADS, B_HEAD)
    kk = kkr / jnp.maximum(jnp.sqrt(jnp.sum(kkr * kkr, axis=-1, keepdims=True)), 1e-12)
    k = k * (1.0 + (a - 1.0) * k_a.astype(f32).reshape(B_HEADS, B_HEAD))
    S, o = rwkv_scan(S0.astype(f32), r, decay, k, v, kk, a)
    mean = jnp.mean(o, axis=-1, keepdims=True)
    var = jnp.mean(jnp.square(o - mean), axis=-1, keepdims=True)
    o = (o - mean) * lax.rsqrt(var + B_LN_EPS)
    o = o * ln_g.astype(f32).reshape(B_HEADS, B_HEAD) + ln_b.astype(f32).reshape(B_HEADS, B_HEAD)
    o = o + jnp.sum(r * k * r_k.astype(f32), axis=-1, keepdims=True) * v
    y = (o.reshape(B, T, D_INNER).astype(h.dtype) * jax.nn.silu(g)) @ w_out
    return y, S, h[:, -1]


def gla_chunk(S, inp):
    q, k, v, g = inp
    C = q.shape[2]
    b = jnp.cumsum(g, axis=2)
    causal = jnp.tril(jnp.ones((C, C), bool))
    o_inter = jnp.einsum('bhtd,bhde->bhte', q * jnp.exp(b), S)
    diff = jnp.where(causal[:, :, None], b[:, :, :, None, :] - b[:, :, None, :, :], -jnp.inf)
    att = jnp.einsum('bhtsd,bhsd->bhts', q[:, :, :, None, :] * jnp.exp(diff), k)
    o = o_inter + jnp.einsum('bhts,bhse->bhte', att, v)
    b_last = b[:, :, -1:, :]
    S = jnp.exp(b_last)[:, :, 0, :, None] * S + jnp.einsum('bhsd,bhse->bhde', k * jnp.exp(b_last - b), v)
    return S, o


def gla_branch(h, S0, w_in, w_gk_up, b_gk, o_g, w_out):
    B, T, _ = h.shape
    f32 = jnp.float32
    q, k, v, gate, gk_low = jnp.split(
        h @ w_in, [C_KEY, 2 * C_KEY, 2 * C_KEY + D_INNER, 2 * C_KEY + 2 * D_INNER], axis=-1)
    gk = jax.nn.log_sigmoid((gk_low @ w_gk_up + b_gk).astype(f32)) / C_GATE_NORM
    chunk = min(C_CHUNK, T)
    nc = T // chunk

    def to_chunks(t, dh):
        return t.astype(f32).reshape(B, nc, chunk, C_HEADS, dh).transpose(1, 0, 3, 2, 4)

    xs = (to_chunks(q, C_DK) * C_DK ** -0.5, to_chunks(k, C_DK), to_chunks(v, C_DV), to_chunks(gk, C_DK))
    S, o = lax.scan(gla_chunk, S0.astype(f32), xs)
    o = rmsnorm(o.transpose(1, 0, 3, 2, 4).reshape(B, T, C_HEADS, C_DV), o_g)
    y = (o.reshape(B, T, D_INNER).astype(h.dtype) * jax.nn.silu(gate)) @ w_out
    return y, S


def setup_inputs(seed: int = 0) -> dict:
    key = jax.random.key(seed)
    ks = iter(jax.random.split(key, 40))
    f32 = jnp.float32

    def nrm(shape, scale):
        return scale * jax.random.normal(next(ks), shape, f32)

    def gain(shape):
        return 1.0 + 0.05 * jax.random.normal(next(ks), shape, f32)

    return {
        'x_prompt': nrm((BATCH, SEQ, D_MODEL), 1.0),
        'x_sample': nrm((DEC_BATCH, DEC_SEQ, D_MODEL), 1.0),
        'cache_k_win': nrm((N_A, DEC_BATCH, WINDOW, A_KV_HEADS, A_HEAD_DIM), 1.0),
        'cache_v_win': nrm((N_A, DEC_BATCH, WINDOW, A_KV_HEADS, A_HEAD_DIM), 1.0),
        'state_wkv': nrm((N_B, DEC_BATCH, B_HEADS, B_HEAD, B_HEAD), 0.1),
        'state_shift': nrm((N_B, DEC_BATCH, D_MODEL), 1.0),
        'state_gla': nrm((N_C, DEC_BATCH, C_HEADS, C_DK, C_DV), 1.0),
        'norm_g': gain((DEPTH, D_MODEL)),
        'rel_bias': nrm((N_BUCKETS, A_HEADS), 0.5),
        'w_in_a': nrm((N_A, D_MODEL, A_IN), D_MODEL ** -0.5),
        'q_norm_g': gain((N_A, A_HEAD_DIM)),
        'k_norm_g': gain((N_A, A_HEAD_DIM)),
        'sinks': nrm((N_A, A_HEADS), 1.0),
        'w_out_a': nrm((N_A, D_INNER, D_MODEL), D_INNER ** -0.5),
        'mu_b': jax.random.uniform(next(ks), (N_B, N_LERP, D_MODEL), f32),
        'w_rkvg_b': nrm((N_B, 4, D_MODEL, D_INNER), D_MODEL ** -0.5),
        'w_lora_down_b': nrm((N_B, 2, D_MODEL, B_LORA), D_MODEL ** -0.5),
        'w_lora_up_b': nrm((N_B, 2, B_LORA, D_INNER), 0.3 * B_LORA ** -0.5),
        'w0_b': -1.0 + nrm((N_B, D_INNER), 0.5),
        'a0_b': nrm((N_B, D_INNER), 0.1),
        'k_k_b': 0.85 + nrm((N_B, D_INNER), 0.05),
        'k_a_b': gain((N_B, D_INNER)),
        'r_k_b': nrm((N_B, B_HEADS, B_HEAD), 0.1),
        'ln_x_g_b': gain((N_B, D_INNER)),
        'ln_x_b_b': nrm((N_B, D_INNER), 0.01),
        'w_out_b': nrm((N_B, D_INNER, D_MODEL), D_INNER ** -0.5),
        'w_in_c': nrm((N_C, D_MODEL, C_IN), D_MODEL ** -0.5),
        'w_gk_up_c': nrm((N_C, C_GATE_RANK, C_KEY), C_GATE_RANK ** -0.5),
        'b_gk_c': nrm((N_C, C_KEY), 0.1),
        'o_norm_g_c': gain((N_C, C_DV)),
        'w_out_c': nrm((N_C, D_INNER, D_MODEL), D_INNER ** -0.5),
    }


def reference(x_prompt, x_sample, cache_k_win, cache_v_win, state_wkv, state_shift, state_gla,
              norm_g, rel_bias, w_in_a, q_norm_g, k_norm_g, sinks, w_out_a,
              mu_b, w_rkvg_b, w_lora_down_b, w_lora_up_b, w0_b, a0_b, k_k_b, k_a_b, r_k_b,
              ln_x_g_b, ln_x_b_b, w_out_b, w_in_c, w_gk_up_c, b_gk_c, o_norm_g_c, w_out_c):
    xp, xs = x_prompt, x_sample
    kwp, vwp, kws, vws = [], [], [], []
    wkvp, shp, wkvs, shs = [], [], [], []
    glap, glas = [], []
    for layer in range(DEPTH):
        kind, j = layer % N_MIXERS, layer // N_MIXERS
        hp = rmsnorm(xp, norm_g[layer])
        hs = rmsnorm(xs, norm_g[layer])
        if kind == 0:
            wa = (w_in_a[j], q_norm_g[j], k_norm_g[j], sinks[j], w_out_a[j], rel_bias)
            yp, kp_, vp_ = attn_branch(hp, *wa)
            ys, ks_, vs_ = attn_branch(hs, *wa, cache_k_win[j], cache_v_win[j])
            kwp.append(kp_); vwp.append(vp_); kws.append(ks_); vws.append(vs_)
        elif kind == 1:
            wb = (mu_b[j], w_rkvg_b[j], w_lora_down_b[j], w_lora_up_b[j], w0_b[j], a0_b[j],
                  k_k_b[j], k_a_b[j], r_k_b[j], ln_x_g_b[j], ln_x_b_b[j], w_out_b[j])
            B = hp.shape[0]
            yp, Sp, lp = rwkv_branch(hp, jnp.zeros((B, D_MODEL), hp.dtype),
                                     jnp.zeros((B, B_HEADS, B_HEAD, B_HEAD), jnp.float32), *wb)
            ys, Ss, ls = rwkv_branch(hs, state_shift[j], state_wkv[j], *wb)
            wkvp.append(Sp); shp.append(lp); wkvs.append(Ss); shs.append(ls)
        else:
            wc = (w_in_c[j], w_gk_up_c[j], b_gk_c[j], o_norm_g_c[j], w_out_c[j])
            B = hp.shape[0]
            yp, Sp = gla_branch(hp, jnp.zeros((B, C_HEADS, C_DK, C_DV), jnp.float32), *wc)
            ys, Ss = gla_branch(hs, state_gla[j], *wc)
            glap.append(Sp); glas.append(Ss)
        xp = xp + yp.astype(xp.dtype)
        xs = xs + ys.astype(xs.dtype)
    return (xp, xs,
            jnp.stack(kwp), jnp.stack(vwp), jnp.stack(kws), jnp.stack(vws),
            jnp.stack(wkvp), jnp.stack(shp), jnp.stack(wkvs), jnp.stack(shs),
            jnp.stack(glap), jnp.stack(glas))
```

```python
import functools
import math

import jax
import jax.numpy as jnp
from jax import lax
from jax.experimental import pallas as pl
from jax.experimental.pallas import tpu as pltpu

F32 = jnp.float32
BF16 = jnp.bfloat16

NORM_EPS = 1e-6
HEAD = 64
LANES = 128
WINDOW = 128
N_BUCKETS = 32
MAX_EXACT = N_BUCKETS // 2
MAX_DISTANCE = 128
B_LN_EPS = 64e-5
C_GATE_NORM = 16.0
VMEM_LIMIT = 56 * 1024 * 1024

NN = (((1,), (0,)), ((), ()))
NT = (((1,), (1,)), ((), ()))
TN = (((0,), (0,)), ((), ()))


def _params(sem):
    return pltpu.CompilerParams(dimension_semantics=sem, vmem_limit_bytes=VMEM_LIMIT)


def _split(x):
    hi = x.astype(BF16)
    lo = (x - hi.astype(F32)).astype(BF16)
    return hi, lo


def _mm(a, b, dims=NN, passes=1):
    if passes == 1:
        return lax.dot_general(a.astype(BF16), b.astype(BF16), dims, preferred_element_type=F32)
    a_hi, a_lo = _split(a)
    b_hi, b_lo = _split(b)
    dg = functools.partial(lax.dot_general, dimension_numbers=dims, preferred_element_type=F32)
    return dg(a_hi, b_hi) + (dg(a_hi, b_lo) + dg(a_lo, b_hi))


def _mm_exact_lhs(a_bf16, b):
    b0 = b.astype(BF16)
    r1 = b - b0.astype(F32)
    b1 = r1.astype(BF16)
    b2 = (r1 - b1.astype(F32)).astype(BF16)
    dg = functools.partial(lax.dot_general, dimension_numbers=NN, preferred_element_type=F32)
    return dg(a_bf16, b0) + (dg(a_bf16, b1) + dg(a_bf16, b2))


def _softplus(z):
    return jnp.maximum(z, 0.0) + jnp.log1p(jnp.exp(-jnp.abs(z)))


def _silu(g):
    return g * jax.nn.sigmoid(g)


def _rms_rows_kernel(x_ref, g_ref, o_ref):
    x = x_ref[...]
    o_ref[...] = x * lax.rsqrt(jnp.mean(x * x, axis=-1, keepdims=True) + NORM_EPS) * g_ref[...]


def _rms_rows(rows, g):
    n, d = rows.shape
    npad = -(-n // 8) * 8
    rows_p = jnp.pad(rows, ((0, npad - n), (0, 0)))
    out = pl.pallas_call(
        _rms_rows_kernel,
        out_shape=jax.ShapeDtypeStruct((npad, d), F32),
        name="rms_rows",
    )(rows_p, g.reshape(1, d))
    return out[:n]


PROLOGUE_ROWS = 256


def _proj_kernel(*refs, lerp, period, tm):
    if lerp:
        x_ref, g_ref, w_ref, first_ref, mu_ref, o_ref, xm_ref = refs
    else:
        x_ref, g_ref, w_ref, o_ref, xm_ref = refs
    j = pl.program_id(2)

    @pl.when(j == 0)
    def _():
        g = g_ref[...]
        rc = min(PROLOGUE_ROWS, tm)
        carry = first_ref[0, 0:1, :] if lerp else None
        for c0 in range(0, tm, rc):
            x = x_ref[c0:c0 + rc, :]
            h = x * lax.rsqrt(jnp.mean(x * x, axis=-1, keepdims=True) + NORM_EPS) * g
            if lerp:
                row = lax.broadcasted_iota(jnp.int32, h.shape, 0)
                hs = jnp.where(row == 0, carry, pltpu.roll(h, 1, 0))
                if period is not None:
                    hs = jnp.where(row % period == 0, first_ref[0, c0:c0 + rc, :], hs)
                carry = h[rc - 1:rc, :]
                h = h + (hs - h) * mu_ref[0]
            xm_ref[c0:c0 + rc, :] = h.astype(BF16)

    o_ref[0] = jnp.dot(xm_ref[...], w_ref[0].astype(BF16), preferred_element_type=F32)


def _proj(x2d, g, w3, col_off, n_out, tm, tn, first=None, mu=None, period=None):
    m, d = x2d.shape
    nc = w3.shape[0]
    lerp = first is not None
    assert m % tm == 0 and n_out % tn == 0 and col_off % tn == 0
    joff = col_off // tn
    in_specs = [
        pl.BlockSpec((tm, d), lambda i, c, j: (i, 0)),
        pl.BlockSpec((1, d), lambda i, c, j: (0, 0)),
        pl.BlockSpec((1, d, tn), lambda i, c, j: (c, 0, joff + j)),
    ]
    args = [x2d, g.reshape(1, d), w3]
    if lerp:
        fr = first.shape[1]
        in_specs += [
            pl.BlockSpec((1, fr, d), lambda i, c, j: (i, 0, 0)),
            pl.BlockSpec((1, 1, d), lambda i, c, j: (c, 0, 0)),
        ]
        args += [first, mu.reshape(nc, 1, d)]
    return pl.pallas_call(
        functools.partial(_proj_kernel, lerp=lerp, period=period, tm=tm),
        grid=(m // tm, nc, n_out // tn),
        in_specs=in_specs,
        out_specs=pl.BlockSpec((1, tm, tn), lambda i, c, j: (c, i, j)),
        out_shape=jax.ShapeDtypeStruct((nc, m, n_out), F32),
        scratch_shapes=[pltpu.VMEM((tm, d), BF16)],
        compiler_params=_params(("parallel", "arbitrary", "arbitrary")),
        name="proj",
    )(*args)


def _outproj_kernel(a_ref, w_ref, x_ref, o_ref):
    o_ref[...] = x_ref[...] + jnp.dot(a_ref[...].astype(BF16), w_ref[...].astype(BF16),
                                      preferred_element_type=F32)


def _outproj(a2d, w, x2d, tm, tn):
    m, kdim = a2d.shape
    n = w.shape[1]
    return pl.pallas_call(
        _outproj_kernel,
        grid=(m // tm, n // tn),
        in_specs=[
            pl.BlockSpec((tm, kdim), lambda i, j: (i, 0)),
            pl.BlockSpec((kdim, tn), lambda i, j: (0, j)),
            pl.BlockSpec((tm, tn), lambda i, j: (i, j)),
        ],
        out_specs=pl.BlockSpec((tm, tn), lambda i, j: (i, j)),
        out_shape=jax.ShapeDtypeStruct((m, n), F32),
        compiler_params=_params(("parallel", "arbitrary")),
        name="outproj",
    )(a2d, w, x2d)


A_KV_HEADS = 8
A_GROUP = 8


def _attn_kernel(sink_ref, q_ref, kc_ref, vc_ref, kp_ref, vp_ref, gate_ref, bias_ref, qg_ref, kg_ref,
                 og_ref, kwin_ref, vwin_ref, k_scr, v_scr, *, tq, prompt, nblocks):
    i = pl.program_id(1)
    tk = WINDOW + tq
    scale = HEAD ** -0.5
    kg = kg_ref[...]
    qg = qg_ref[...]

    def head_norm(xh, gain):
        return xh * lax.rsqrt(jnp.mean(xh * xh, axis=-1, keepdims=True) + NORM_EPS) * gain

    for kh in range(A_KV_HEADS):
        sl = slice(kh * HEAD, (kh + 1) * HEAD)
        kprev = kp_ref[:, sl]
        if prompt:
            kprev = head_norm(kprev, kg)
        k_scr[0:WINDOW, sl] = kprev
        k_scr[WINDOW:tk, sl] = head_norm(kc_ref[:, sl], kg)
    v_scr[0:WINDOW, :] = vp_ref[...]
    v_scr[WINDOW:tk, :] = vc_ref[...]

    r = lax.broadcasted_iota(jnp.int32, (tq, tk), 0)
    c = lax.broadcasted_iota(jnp.int32, (tq, tk), 1)
    dist = WINDOW + r - c
    valid = (dist >= 0) & (dist < WINDOW)
    if prompt:
        valid = valid & ((i > 0) | (c >= WINDOW))

    for kh in range(A_KV_HEADS):
        kk = k_scr[:, kh * HEAD:(kh + 1) * HEAD].astype(BF16)
        vv = v_scr[:, kh * HEAD:(kh + 1) * HEAD].astype(BF16)
        for pr in range(A_GROUP // 2):
            outs = []
            for e in range(2):
                h = kh * A_GROUP + pr * 2 + e
                qh = head_norm(q_ref[:, h * HEAD:(h + 1) * HEAD], qg)
                s = lax.dot_general(qh.astype(BF16), kk, NT, preferred_element_type=F32) * scale
                s = jnp.where(valid, s + bias_ref[h], -jnp.inf)
                sink = sink_ref[h]
                m = jnp.maximum(jnp.max(s, axis=-1, keepdims=True), sink)
                p = jnp.exp(s - m)
                denom = jnp.sum(p, axis=-1, keepdims=True) + jnp.exp(sink - m)
                o = jnp.dot(p.astype(BF16), vv, preferred_element_type=F32)
                outs.append(o / denom)
            lanes = slice((kh * A_GROUP + pr * 2) * HEAD, (kh * A_GROUP + pr * 2 + 2) * HEAD)
            o2 = jnp.concatenate(outs, axis=-1)
            og_ref[:, lanes] = (o2 * _silu(gate_ref[:, lanes])).astype(og_ref.dtype)

    @pl.when(i == nblocks - 1)
    def _():
        kwin_ref[0] = k_scr[tk - WINDOW:tk, :]
        vwin_ref[0] = v_scr[tk - WINDOW:tk, :]


def _t5_bucket(dist):
    d = jnp.maximum(dist, 0)
    large = MAX_EXACT + (jnp.log(jnp.maximum(d, 1).astype(F32) / MAX_EXACT)
                         / math.log(MAX_DISTANCE / MAX_EXACT) * (N_BUCKETS - MAX_EXACT)).astype(jnp.int32)
    return jnp.where(d < MAX_EXACT, d, jnp.minimum(large, N_BUCKETS - 1))


def _attn_mix(q, kv, gate, rel_bias, q_g, k_g, sinks, bn, tn_, cache_k, cache_v, og_dtype):
    m = q.shape[0]
    prompt = cache_k is None
    tq = WINDOW if prompt else tn_
    nb = tn_ // tq
    tk = WINDOW + tq
    dist = WINDOW + jnp.arange(tq)[:, None] - jnp.arange(tk)[None, :]
    bias = jnp.transpose(rel_bias[_t5_bucket(dist)], (2, 0, 1)).astype(F32)
    nkv = A_KV_HEADS * HEAD
    row = lambda b, i: b * nb + i
    if prompt:
        kp_arr, vp_arr = kv, kv
        kp_spec = pl.BlockSpec((WINDOW, nkv), lambda b, i: (jnp.maximum(row(b, i) - 1, 0), 0))
        vp_spec = pl.BlockSpec((WINDOW, nkv), lambda b, i: (jnp.maximum(row(b, i) - 1, 0), 1))
    else:
        kp_arr = cache_k.reshape(bn * WINDOW, nkv)
        vp_arr = cache_v.reshape(bn * WINDOW, nkv)
        kp_spec = pl.BlockSpec((WINDOW, nkv), lambda b, i: (b, 0))
        vp_spec = pl.BlockSpec((WINDOW, nkv), lambda b, i: (b, 0))
    nq = q.shape[1]
    og, kwin, vwin = pl.pallas_call(
        functools.partial(_attn_kernel, tq=tq, prompt=prompt, nblocks=nb),
        grid=(bn, nb),
        in_specs=[
            pl.BlockSpec(memory_space=pltpu.SMEM),
            pl.BlockSpec((tq, nq), lambda b, i: (row(b, i), 0)),
            pl.BlockSpec((tq, nkv), lambda b, i: (row(b, i), 0)),
            pl.BlockSpec((tq, nkv), lambda b, i: (row(b, i), 1)),
            kp_spec, vp_spec,
            pl.BlockSpec((tq, nq), lambda b, i: (row(b, i), 0)),
            pl.BlockSpec((nq // HEAD, tq, tk), lambda b, i: (0, 0, 0)),
            pl.BlockSpec((1, HEAD), lambda b, i: (0, 0)),
            pl.BlockSpec((1, HEAD), lambda b, i: (0, 0)),
        ],
        out_specs=[
            pl.BlockSpec((tq, nq), lambda b, i: (row(b, i), 0)),
            pl.BlockSpec((1, WINDOW, nkv), lambda b, i: (b, 0, 0)),
            pl.BlockSpec((1, WINDOW, nkv), lambda b, i: (b, 0, 0)),
        ],
        out_shape=[
            jax.ShapeDtypeStruct((m, nq), og_dtype),
            jax.ShapeDtypeStruct((bn, WINDOW, nkv), F32),
            jax.ShapeDtypeStruct((bn, WINDOW, nkv), F32),
        ],
        scratch_shapes=[pltpu.VMEM((tk, nkv), F32), pltpu.VMEM((tk, nkv), F32)],
        compiler_params=_params(("parallel", "arbitrary")),
        name="swa_attention",
    )(sinks, q, kv, kv, kp_arr, vp_arr, gate, bias, q_g.reshape(1, HEAD), k_g.reshape(1, HEAD))
    return og, kwin, vwin


def _attn_layer(x2d, bn, tn_, g, w_in, q_g, k_g, sinks, w_out, rel_bias, cache_k, cache_v, tm, og_dtype):
    d = x2d.shape[1]
    w3 = w_in.reshape(1, d, w_in.shape[1])
    nq = w_out.shape[0]
    nkv = A_KV_HEADS * HEAD
    q = _proj(x2d, g, w3, 0, nq, tm, 512)[0]
    kv = _proj(x2d, g, w3, nq, 2 * nkv, tm, 512)[0]
    gate = _proj(x2d, g, w3, nq + 2 * nkv, nq, tm, 512)[0]
    og, kwin, vwin = _attn_mix(q, kv, gate, rel_bias, q_g, k_g, sinks, bn, tn_, cache_k, cache_v, og_dtype)
    y = _outproj(og, w_out, x2d, tm, 512)
    shape = (bn, WINDOW, A_KV_HEADS, HEAD)
    return y, kwin.reshape(shape), vwin.reshape(shape)


def _rwkv_kernel(r_ref, k_ref, v_ref, g_ref, lw_ref, la_ref, wup_ref, vec_ref, s0_ref,
                 og_ref, sout_ref, s_ref, *, C, nchunks, nt, passes):
    t = pl.program_id(2)
    n = 2 * C
    lane = lax.broadcasted_iota(jnp.int32, (1, LANES), 1)
    lo = (lane < HEAD).astype(F32)
    hi = 1.0 - lo

    @pl.when(t == 0)
    def _():
        z = jnp.zeros((HEAD, HEAD), F32)
        s_ref[0:HEAD, :] = jnp.concatenate([s0_ref[0, 0], z], axis=-1)
        s_ref[HEAD:LANES, :] = jnp.concatenate([z, s0_ref[0, 1]], axis=-1)

    vec = vec_ref[...]
    w0, a0, kk_w, ka_w, rk_w, ln_g, ln_b = (vec[i:i + 1, :] for i in range(7))
    wup0 = wup_ref[0]
    wup1 = wup_ref[1]

    rr = lax.broadcasted_iota(jnp.int32, (n, n), 0)
    cc = lax.broadcasted_iota(jnp.int32, (n, n), 1)
    strict = cc < rr
    incl = cc <= rr
    eye = (cc == rr).astype(F32)
    tr = lax.broadcasted_iota(jnp.int32, (C, C), 0)
    tc = lax.broadcasted_iota(jnp.int32, (C, C), 1)
    tri = (tc <= tr).astype(BF16)
    jr = lax.broadcasted_iota(jnp.int32, (LANES, LANES), 0)
    jc = lax.broadcasted_iota(jnp.int32, (LANES, LANES), 1)
    ones_bd = ((jr // HEAD) == (jc // HEAD)).astype(F32)
    mean_bd = ones_bd * (1.0 / HEAD)
    nsteps = int(math.log2(C)) - 1

    def chunk(ci, carry):
        rows = pl.ds(pl.multiple_of(ci * C, C), C)
        r = r_ref[0, rows, :]
        k = k_ref[0, rows, :]
        v = v_ref[0, rows, :]
        g = g_ref[0, rows, :]
        lw = lw_ref[0, rows, :]
        la = la_ref[0, rows, :]

        xw = w0 + _mm(jnp.tanh(lw), wup0, NN, 3)
        ld = -jnp.exp(-_softplus(-xw) - 0.5)
        a = jax.nn.sigmoid(a0 + _mm(la, wup1, NN, 3))
        kkr = k * kk_w
        kk = kkr / jnp.maximum(jnp.sqrt(_mm(kkr * kkr, ones_bd, NN, 3)), 1e-12)
        k2 = k * (1.0 + (a - 1.0) * ka_w)

        b = _mm_exact_lhs(tri, ld)
        eb = jnp.exp(b)
        enb = jnp.exp(-b)
        al = -kk * jnp.exp(b - ld)
        be = kk * a * enb
        kt = k2 * enb
        rb = r * eb
        cat = lambda x: jnp.concatenate([x * lo, x * hi], axis=0)
        la_, lr_, rb_, rk_, vb_ = cat(al), cat(rb), cat(be), cat(kt), cat(v)

        m_ab = jnp.where(strict, _mm(la_, rb_, NT, passes), 0.0)
        m_ak = jnp.where(strict, _mm(la_, rk_, NT, passes), 0.0)
        m_rb = jnp.where(incl, _mm(lr_, rb_, NT, passes), 0.0)
        m_rk = jnp.where(incl, _mm(lr_, rk_, NT, passes), 0.0)

        apow = m_ab
        tinv = eye + apow
        for _ in range(nsteps):
            apow = _mm(apow, apow, NN, passes)
            tinv = tinv + _mm(apow, tinv, NN, passes)

        s = s_ref[...]
        u = _mm(tinv, _mm(la_, s, NT, passes) + _mm(m_ak, vb_, NN, passes), NN, passes)
        o_bd = _mm(lr_, s, NT, passes) + _mm(m_rb, u, NN, passes) + _mm(m_rk, vb_, NN, passes)
        o = o_bd[0:C] + o_bd[C:n]
        e_c = eb[C - 1:C, :]
        s_ref[...] = s * e_c + _mm(u, rb_ * e_c, TN, passes) + _mm(vb_, rk_ * e_c, TN, passes)

        mean = _mm(o, mean_bd, NN, 3)
        dlt = o - mean
        var = _mm(dlt * dlt, mean_bd, NN, 3)
        on = dlt * lax.rsqrt(var + B_LN_EPS) * ln_g + ln_b
        on = on + _mm(r * k2 * rk_w, ones_bd, NN, 3) * v
        og_ref[rows, :] = (on * _silu(g)).astype(og_ref.dtype)
        return carry

    lax.fori_loop(0, nchunks, chunk, 0)

    @pl.when(t == nt - 1)
    def _():
        sout_ref[0, 0] = s_ref[0:HEAD, 0:HEAD]
        sout_ref[0, 1] = s_ref[HEAD:LANES, HEAD:LANES]


def _rwkv_mix(rkvg, lora, wup, vec, s0, bn, tn_, C, tb, og_dtype, passes):
    m = rkvg.shape[1]
    nch = rkvg.shape[2]
    npairs = nch // LANES
    nt = tn_ // tb
    rank = lora.shape[2]
    nh = s0.shape[1]
    row = lambda b, t: b * nt + t

    def xspec(c):
        return pl.BlockSpec((1, tb, LANES), lambda b, p, t: (c, row(b, t), p))

    def lspec(c):
        return pl.BlockSpec((1, tb, rank), lambda b, p, t: (c, row(b, t), 0))

    og, sout = pl.pallas_call(
        functools.partial(_rwkv_kernel, C=C, nchunks=tb // C, nt=nt, passes=passes),
        grid=(bn, npairs, nt),
        in_specs=[
            xspec(0), xspec(1), xspec(2), xspec(3), lspec(0), lspec(1),
            pl.BlockSpec((2, rank, LANES), lambda b, p, t: (0, 0, p)),
            pl.BlockSpec((8, LANES), lambda b, p, t: (0, p)),
            pl.BlockSpec((1, 2, HEAD, HEAD), lambda b, p, t: (b, p, 0, 0)),
        ],
        out_specs=[
            pl.BlockSpec((tb, LANES), lambda b, p, t: (row(b, t), p)),
            pl.BlockSpec((1, 2, HEAD, HEAD), lambda b, p, t: (b, p, 0, 0)),
        ],
        out_shape=[
            jax.ShapeDtypeStruct((m, nch), og_dtype),
            jax.ShapeDtypeStruct((bn, nh, HEAD, HEAD), F32),
        ],
        scratch_shapes=[pltpu.VMEM((LANES, LANES), F32)],
        compiler_params=_params(("parallel", "parallel", "arbitrary")),
        name="rwkv7_chunked",
    )(rkvg, rkvg, rkvg, rkvg, lora, lora, wup, vec, s0)
    return og, sout


def _rwkv_layer(x2d, bn, tn_, g, shift, s0, mu, w_rkvg, w_down, w_up, w0, a0, k_k, k_a, r_k, ln_g, ln_b,
                w_out, tm, C, tb, og_dtype, passes):
    m, d = x2d.shape
    nblk = m // tm
    if tm <= tn_:
        starts = jnp.arange(nblk) * tm
        prev = _rms_rows(x2d[jnp.maximum(starts - 1, 0)], g)
        first = jnp.where((starts % tn_ == 0)[:, None], shift[starts // tn_], prev).reshape(nblk, 1, d)
        period = None
    else:
        assert nblk == 1
        first = jnp.zeros((bn, tn_, d), F32).at[:, 0].set(shift).reshape(1, m, d)
        period = tn_
    shift_out = _rms_rows(x2d[tn_ - 1::tn_], g)
    rkvg = _proj(x2d, g, w_rkvg, 0, w_rkvg.shape[2], tm, 512, first=first, mu=mu[:4], period=period)
    rank = w_down.shape[2]
    lora = _proj(x2d, g, w_down, 0, rank, tm, rank, first=first, mu=mu[4:6], period=period)
    nch = w_rkvg.shape[2]
    vec = jnp.stack([w0, a0, k_k, k_a, r_k.reshape(nch), ln_g, ln_b, jnp.zeros((nch,), F32)])
    og, sout = _rwkv_mix(rkvg, lora, w_up, vec, s0, bn, tn_, C, tb, og_dtype, passes)
    y = _outproj(og, w_out, x2d, tm, 512)
    return y, sout, shift_out


GLA_SUB = 16


def _gla_kernel(q_ref, k_ref, v_ref, gate_ref, gl_ref, wup_ref, bgk_ref, og_w_ref, s0_ref,
                og_ref, sout_ref, st_ref, *, C, nchunks, nt, passes):
    t = pl.program_id(2)
    dk = q_ref.shape[-1]
    sub = min(GLA_SUB, C)
    nsub = C // sub
    qscale = dk ** -0.5

    @pl.when(t == 0)
    def _():
        st_ref[...] = s0_ref[0, 0].T

    tr = lax.broadcasted_iota(jnp.int32, (C, C), 0)
    tc = lax.broadcasted_iota(jnp.int32, (C, C), 1)
    tri = (tc <= tr).astype(BF16)
    wup = wup_ref[...]
    bgk = bgk_ref[...]
    og_w = og_w_ref[...]

    def chunk(ci, carry):
        rows = pl.ds(pl.multiple_of(ci * C, C), C)
        q = q_ref[rows, :] * qscale
        k = k_ref[rows, :]
        v = v_ref[rows, :]
        gate = gate_ref[rows, :]
        gl = gl_ref[rows, :]
        glog = -_softplus(-(_mm(gl, wup, NN, 3) + bgk)) * (1.0 / C_GATE_NORM)
        b = _mm_exact_lhs(tri, glog)
        st = st_ref[...]
        o = _mm(q * jnp.exp(b), st, NT, passes)
        parts = []
        for i in range(nsub):
            r0 = i * sub
            nk = r0 + sub
            ref = b[r0:r0 + 1, :] - glog[r0:r0 + 1, :]
            qi = q[r0:nk] * jnp.exp(b[r0:nk] - ref)
            ki = k[0:nk] * jnp.exp(ref - b[0:nk])
            att = _mm(qi, ki, NT, passes)
            ar = lax.broadcasted_iota(jnp.int32, (sub, nk), 0) + r0
            ac = lax.broadcasted_iota(jnp.int32, (sub, nk), 1)
            att = jnp.where(ac <= ar, att, 0.0)
            parts.append(_mm(att, v[0:nk], NN, passes))
        o = o + (jnp.concatenate(parts, axis=0) if nsub > 1 else parts[0])
        bl = b[C - 1:C, :]
        st_ref[...] = st * jnp.exp(bl) + _mm(v, k * jnp.exp(bl - b), TN, passes)
        on = o * lax.rsqrt(jnp.mean(o * o, axis=-1, keepdims=True) + NORM_EPS) * og_w
        og_ref[rows, :] = (on * _silu(gate)).astype(og_ref.dtype)
        return carry

    lax.fori_loop(0, nchunks, chunk, 0)

    @pl.when(t == nt - 1)
    def _():
        sout_ref[0, 0] = st_ref[...].T


def _gla_mix(qk, v, gate, gl, w_up, b_gk, o_g, s0, bn, tn_, C, tb, og_dtype, passes):
    m = qk.shape[0]
    nh, dk, dv = s0.shape[1:]
    nt = tn_ // tb
    rank = gl.shape[1]
    row = lambda b, t: b * nt + t
    og, sout = pl.pallas_call(
        functools.partial(_gla_kernel, C=C, nchunks=tb // C, nt=nt, passes=passes),
        grid=(bn, nh, nt),
        in_specs=[
            pl.BlockSpec((tb, dk), lambda b, h, t: (row(b, t), h)),
            pl.BlockSpec((tb, dk), lambda b, h, t: (row(b, t), nh + h)),
            pl.BlockSpec((tb, dv), lambda b, h, t: (row(b, t), h)),
            pl.BlockSpec((tb, dv), lambda b, h, t: (row(b, t), h)),
            pl.BlockSpec((tb, rank), lambda b, h, t: (row(b, t), 0)),
            pl.BlockSpec((rank, dk), lambda b, h, t: (0, h)),
            pl.BlockSpec((1, dk), lambda b, h, t: (0, h)),
            pl.BlockSpec((1, dv), lambda b, h, t: (0, 0)),
            pl.BlockSpec((1, 1, dk, dv), lambda b, h, t: (b, h, 0, 0)),
        ],
        out_specs=[
            pl.BlockSpec((tb, dv), lambda b, h, t: (row(b, t), h)),
            pl.BlockSpec((1, 1, dk, dv), lambda b, h, t: (b, h, 0, 0)),
        ],
        out_shape=[
            jax.ShapeDtypeStruct((m, nh * dv), og_dtype),
            jax.ShapeDtypeStruct((bn, nh, dk, dv), F32),
        ],
        scratch_shapes=[pltpu.VMEM((dv, dk), F32)],
        compiler_params=_params(("parallel", "parallel", "arbitrary")),
        name="gla_chunked",
    )(qk, qk, v, gate, gl, w_up, b_gk.reshape(1, -1), o_g.reshape(1, -1), s0)
    return og, sout


def _gla_layer(x2d, bn, tn_, g, s0, w_in, w_up, b_gk, o_g, w_out, tm, C, tb, og_dtype, passes):
    d = x2d.shape[1]
    nh, dk, dv = s0.shape[1:]
    ckey = nh * dk
    dinner = nh * dv
    rank = w_up.shape[0]
    w3 = w_in.reshape(1, d, w_in.shape[1])
    qk = _proj(x2d, g, w3, 0, 2 * ckey, tm, 512)[0]
    v = _proj(x2d, g, w3, 2 * ckey, dinner, tm, 512)[0]
    gate = _proj(x2d, g, w3, 2 * ckey + dinner, dinner, tm, 512)[0]
    gl = _proj(x2d, g, w_in[:, 2 * ckey + 2 * dinner:].reshape(1, d, rank), 0, rank, tm, rank)[0]
    og, sout = _gla_mix(qk, v, gate, gl, w_up, b_gk, o_g, s0, bn, tn_, C, tb, og_dtype, passes)
    y = _outproj(og, w_out, x2d, tm, 512)
    return y, sout


PROMPT_TM = 1024
RWKV_CHUNK = 64
RWKV_TB = 512
GLA_CHUNK = 64
GLA_TB = 512
CHUNK_PASSES = 1


def kernel(x_prompt, x_sample, cache_k_win, cache_v_win, state_wkv, state_shift, state_gla, norm_g, rel_bias, w_in_a, q_norm_g, k_norm_g, sinks, w_out_a, mu_b, w_rkvg_b, w_lora_down_b, w_lora_up_b, w0_b, a0_b, k_k_b, k_a_b, r_k_b, ln_x_g_b, ln_x_b_b, w_out_b, w_in_c, w_gk_up_c, b_gk_c, o_norm_g_c, w_out_c):
    bp, tp, d = x_prompt.shape
    bs, ts, _ = x_sample.shape
    depth = norm_g.shape[0]
    xp = x_prompt.reshape(bp * tp, d)
    xs = x_sample.reshape(bs * ts, d)
    ms = bs * ts
    kwp, vwp, kws, vws, wkvp, shp, wkvs, shs, glap, glas = ([] for _ in range(10))
    for layer in range(depth):
        kind, j = layer % 3, layer // 3
        g = norm_g[layer]
        if kind == 0:
            wa = (w_in_a[j], q_norm_g[j], k_norm_g[j], sinks[j], w_out_a[j], rel_bias)
            xp, kp_, vp_ = _attn_layer(xp, bp, tp, g, *wa, None, None, PROMPT_TM, BF16)
            xs, ks_, vs_ = _attn_layer(xs, bs, ts, g, *wa, cache_k_win[j], cache_v_win[j], ms, F32)
            kwp.append(kp_); vwp.append(vp_); kws.append(ks_); vws.append(vs_)
        elif kind == 1:
            wb = (mu_b[j], w_rkvg_b[j], w_lora_down_b[j], w_lora_up_b[j], w0_b[j], a0_b[j], k_k_b[j], k_a_b[j],
                  r_k_b[j], ln_x_g_b[j], ln_x_b_b[j], w_out_b[j])
            nh = w_rkvg_b.shape[3] // HEAD
            xp, sp_, lp_ = _rwkv_layer(xp, bp, tp, g, jnp.zeros((bp, d), F32), jnp.zeros((bp, nh, HEAD, HEAD), F32),
                                       *wb, PROMPT_TM, RWKV_CHUNK, RWKV_TB, BF16, CHUNK_PASSES)
            xs, ss_, ls_ = _rwkv_layer(xs, bs, ts, g, state_shift[j], state_wkv[j], *wb, ms, ts, ts, F32, CHUNK_PASSES)
            wkvp.append(sp_); shp.append(lp_); wkvs.append(ss_); shs.append(ls_)
        else:
            wc = (w_in_c[j], w_gk_up_c[j], b_gk_c[j], o_norm_g_c[j], w_out_c[j])
            xp, sp_ = _gla_layer(xp, bp, tp, g, jnp.zeros((bp,) + state_gla.shape[2:], F32), *wc,
                                 PROMPT_TM, GLA_CHUNK, GLA_TB, BF16, CHUNK_PASSES)
            xs, ss_ = _gla_layer(xs, bs, ts, g, state_gla[j], *wc, ms, ts, ts, F32, CHUNK_PASSES)
            glap.append(sp_); glas.append(ss_)
    return (xp.reshape(bp, tp, d), xs.reshape(bs, ts, d),
            jnp.stack(kwp), jnp.stack(vwp), jnp.stack(kws), jnp.stack(vws),
            jnp.stack(wkvp), jnp.stack(shp), jnp.stack(wkvs), jnp.stack(shs),
            jnp.stack(glap), jnp.stack(glas))
```

```python
import functools
import math

import jax
import jax.numpy as jnp
from jax import lax
from jax.experimental import pallas as pl
from jax.experimental.pallas import tpu as pltpu

F32 = jnp.float32
BF16 = jnp.bfloat16

NORM_EPS = 1e-6
HEAD = 64
LANES = 128
WINDOW = 128
N_BUCKETS = 32
MAX_EXACT = N_BUCKETS // 2
MAX_DISTANCE = 128
B_LN_EPS = 64e-5
C_GATE_NORM = 16.0
VMEM_LIMIT = 56 * 1024 * 1024

NN = (((1,), (0,)), ((), ()))
NT = (((1,), (1,)), ((), ()))
TN = (((0,), (0,)), ((), ()))


def _params(sem):
    return pltpu.CompilerParams(dimension_semantics=sem, vmem_limit_bytes=VMEM_LIMIT)


def _split(x):
    hi = x.astype(BF16)
    lo = (x - hi.astype(F32)).astype(BF16)
    return hi, lo


def _mm(a, b, dims=NN, passes=1):
    if passes == 1:
        return lax.dot_general(a.astype(BF16), b.astype(BF16), dims, preferred_element_type=F32)
    a_hi, a_lo = _split(a)
    b_hi, b_lo = _split(b)
    dg = functools.partial(lax.dot_general, dimension_numbers=dims, preferred_element_type=F32)
    return dg(a_hi, b_hi) + (dg(a_hi, b_lo) + dg(a_lo, b_hi))


def _mm_exact_lhs(a_bf16, b):
    b0 = b.astype(BF16)
    r1 = b - b0.astype(F32)
    b1 = r1.astype(BF16)
    b2 = (r1 - b1.astype(F32)).astype(BF16)
    dg = functools.partial(lax.dot_general, dimension_numbers=NN, preferred_element_type=F32)
    return dg(a_bf16, b0) + (dg(a_bf16, b1) + dg(a_bf16, b2))


def _softplus(z):
    return jnp.maximum(z, 0.0) + jnp.log1p(jnp.exp(-jnp.abs(z)))


def _silu(g):
    return g * jax.nn.sigmoid(g)


def _tri(c):
    r = lax.broadcasted_iota(jnp.int32, (c, c), 0)
    col = lax.broadcasted_iota(jnp.int32, (c, c), 1)
    return (col <= r).astype(BF16)


def _rms_rows_kernel(x_ref, g_ref, o_ref):
    x = x_ref[...]
    o_ref[...] = x * lax.rsqrt(jnp.mean(x * x, axis=-1, keepdims=True) + NORM_EPS) * g_ref[...]


def _rms_rows(rows, g):
    n, d = rows.shape
    npad = -(-n // 8) * 8
    rows_p = jnp.pad(rows, ((0, npad - n), (0, 0)))
    out = pl.pallas_call(
        _rms_rows_kernel,
        out_shape=jax.ShapeDtypeStruct((npad, d), F32),
        name="rms_rows",
    )(rows_p, g.reshape(1, d))
    return out[:n]


PROLOGUE_ROWS = 256


def _proj_kernel(*refs, lerp, period, tm):
    if lerp:
        x_ref, g_ref, w_ref, first_ref, mu_ref, o_ref, xm_ref = refs
    else:
        x_ref, g_ref, w_ref, o_ref, xm_ref = refs
    j = pl.program_id(2)

    @pl.when(j == 0)
    def _():
        g = g_ref[...]
        rc = min(PROLOGUE_ROWS, tm)
        carry = first_ref[0, 0:1, :] if lerp else None
        for c0 in range(0, tm, rc):
            x = x_ref[c0:c0 + rc, :]
            h = x * lax.rsqrt(jnp.mean(x * x, axis=-1, keepdims=True) + NORM_EPS) * g
            if lerp:
                row = lax.broadcasted_iota(jnp.int32, h.shape, 0)
                hs = jnp.where(row == 0, carry, pltpu.roll(h, 1, 0))
                if period is not None:
                    hs = jnp.where(row % period == 0, first_ref[0, c0:c0 + rc, :], hs)
                carry = h[rc - 1:rc, :]
                h = h + (hs - h) * mu_ref[0]
            xm_ref[c0:c0 + rc, :] = h.astype(BF16)

    o_ref[0] = jnp.dot(xm_ref[...], w_ref[0].astype(BF16), preferred_element_type=F32)


def _proj(x2d, g, w3, col_off, n_out, tm, tn, first=None, mu=None, period=None):
    m, d = x2d.shape
    nc = w3.shape[0]
    lerp = first is not None
    assert m % tm == 0 and n_out % tn == 0 and col_off % tn == 0
    joff = col_off // tn
    in_specs = [
        pl.BlockSpec((tm, d), lambda i, c, j: (i, 0)),
        pl.BlockSpec((1, d), lambda i, c, j: (0, 0)),
        pl.BlockSpec((1, d, tn), lambda i, c, j: (c, 0, joff + j)),
    ]
    args = [x2d, g.reshape(1, d), w3]
    if lerp:
        fr = first.shape[1]
        in_specs += [
            pl.BlockSpec((1, fr, d), lambda i, c, j: (i, 0, 0)),
            pl.BlockSpec((1, 1, d), lambda i, c, j: (c, 0, 0)),
        ]
        args += [first, mu.reshape(nc, 1, d)]
    return pl.pallas_call(
        functools.partial(_proj_kernel, lerp=lerp, period=period, tm=tm),
        grid=(m // tm, nc, n_out // tn),
        in_specs=in_specs,
        out_specs=pl.BlockSpec((1, tm, tn), lambda i, c, j: (c, i, j)),
        out_shape=jax.ShapeDtypeStruct((nc, m, n_out), F32),
        scratch_shapes=[pltpu.VMEM((tm, d), BF16)],
        compiler_params=_params(("parallel", "arbitrary", "arbitrary")),
        name="proj",
    )(*args)


def _outproj_kernel(a_ref, w_ref, x_ref, o_ref):
    o_ref[...] = x_ref[...] + jnp.dot(a_ref[...].astype(BF16), w_ref[...].astype(BF16),
                                      preferred_element_type=F32)


def _outproj(a2d, w, x2d, tm, tn):
    m, kdim = a2d.shape
    n = w.shape[1]
    return pl.pallas_call(
        _outproj_kernel,
        grid=(m // tm, n // tn),
        in_specs=[
            pl.BlockSpec((tm, kdim), lambda i, j: (i, 0)),
            pl.BlockSpec((kdim, tn), lambda i, j: (0, j)),
            pl.BlockSpec((tm, tn), lambda i, j: (i, j)),
        ],
        out_specs=pl.BlockSpec((tm, tn), lambda i, j: (i, j)),
        out_shape=jax.ShapeDtypeStruct((m, n), F32),
        compiler_params=_params(("parallel", "arbitrary")),
        name="outproj",
    )(a2d, w, x2d)


A_KV_HEADS = 8
A_GROUP = 8


def _attn_kernel(q_ref, kc_ref, vc_ref, kp_ref, vp_ref, gate_ref, bias_ref, qg_ref, kg_ref,
                 og_ref, kwin_ref, vwin_ref, k_scr, v_scr, *, tq, prompt, nblocks):
    i = pl.program_id(1)
    tk = WINDOW + tq
    scale = HEAD ** -0.5
    kg = kg_ref[...]
    qg = qg_ref[...]

    def head_norm(xh, gain):
        return xh * lax.rsqrt(jnp.mean(xh * xh, axis=-1, keepdims=True) + NORM_EPS) * gain

    for kh in range(A_KV_HEADS):
        sl = slice(kh * HEAD, (kh + 1) * HEAD)
        kprev = kp_ref[:, sl]
        if prompt:
            kprev = head_norm(kprev, kg)
        k_scr[0:WINDOW, sl] = kprev
        k_scr[WINDOW:tk, sl] = head_norm(kc_ref[:, sl], kg)
    v_scr[0:WINDOW, :] = vp_ref[...]
    v_scr[WINDOW:tk, :] = vc_ref[...]
    k_scr[0:1, :] = jnp.zeros((1, k_scr.shape[1]), F32)
    v_scr[0:1, :] = jnp.zeros((1, v_scr.shape[1]), F32)

    rows = A_GROUP * tq
    r = lax.broadcasted_iota(jnp.int32, (rows, tk), 0) % tq
    c = lax.broadcasted_iota(jnp.int32, (rows, tk), 1)
    dist = WINDOW + r - c
    valid = (dist >= 0) & (dist < WINDOW)
    if prompt:
        valid = valid & ((i > 0) | (c >= WINDOW))
    valid = valid | (c == 0)

    for kh in range(A_KV_HEADS):
        kk = k_scr[:, kh * HEAD:(kh + 1) * HEAD].astype(BF16)
        vv = v_scr[:, kh * HEAD:(kh + 1) * HEAD].astype(BF16)
        h0 = kh * A_GROUP
        qs = jnp.concatenate([q_ref[:, (h0 + e) * HEAD:(h0 + e + 1) * HEAD] for e in range(A_GROUP)], axis=0)
        qs = head_norm(qs, qg)
        s = lax.dot_general(qs.astype(BF16), kk, NT, preferred_element_type=F32) * scale
        s = jnp.where(valid, s + bias_ref[h0:h0 + A_GROUP].reshape(rows, tk), -jnp.inf)
        p = jnp.exp(s - jnp.max(s, axis=-1, keepdims=True))
        o = jnp.dot(p.astype(BF16), vv, preferred_element_type=F32) / jnp.sum(p, axis=-1, keepdims=True)
        lanes = slice(h0 * HEAD, (h0 + A_GROUP) * HEAD)
        o = jnp.concatenate([o[e * tq:(e + 1) * tq] for e in range(A_GROUP)], axis=-1)
        og_ref[:, lanes] = (o * _silu(gate_ref[:, lanes])).astype(og_ref.dtype)

    @pl.when(i == nblocks - 1)
    def _():
        kwin_ref[0] = k_scr[tk - WINDOW:tk, :]
        vwin_ref[0] = v_scr[tk - WINDOW:tk, :]


def _t5_bucket(dist):
    d = jnp.maximum(dist, 0)
    large = MAX_EXACT + (jnp.log(jnp.maximum(d, 1).astype(F32) / MAX_EXACT)
                         / math.log(MAX_DISTANCE / MAX_EXACT) * (N_BUCKETS - MAX_EXACT)).astype(jnp.int32)
    return jnp.where(d < MAX_EXACT, d, jnp.minimum(large, N_BUCKETS - 1))


def _attn_mix(q, kv, gate, rel_bias, q_g, k_g, sinks, bn, tn_, cache_k, cache_v, og_dtype):
    m = q.shape[0]
    prompt = cache_k is None
    tq = WINDOW if prompt else tn_
    nb = tn_ // tq
    tk = WINDOW + tq
    dist = WINDOW + jnp.arange(tq)[:, None] - jnp.arange(tk)[None, :]
    bias = jnp.transpose(rel_bias[_t5_bucket(dist)], (2, 0, 1)).astype(F32)
    bias = bias.at[:, :, 0].set(sinks.astype(F32)[:, None])
    nkv = A_KV_HEADS * HEAD
    row = lambda b, i: b * nb + i
    if prompt:
        kp_arr, vp_arr = kv, kv
        kp_spec = pl.BlockSpec((WINDOW, nkv), lambda b, i: (jnp.maximum(row(b, i) - 1, 0), 0))
        vp_spec = pl.BlockSpec((WINDOW, nkv), lambda b, i: (jnp.maximum(row(b, i) - 1, 0), 1))
    else:
        kp_arr = cache_k.reshape(bn * WINDOW, nkv)
        vp_arr = cache_v.reshape(bn * WINDOW, nkv)
        kp_spec = pl.BlockSpec((WINDOW, nkv), lambda b, i: (b, 0))
        vp_spec = pl.BlockSpec((WINDOW, nkv), lambda b, i: (b, 0))
    nq = q.shape[1]
    og, kwin, vwin = pl.pallas_call(
        functools.partial(_attn_kernel, tq=tq, prompt=prompt, nblocks=nb),
        grid=(bn, nb),
        in_specs=[
            pl.BlockSpec((tq, nq), lambda b, i: (row(b, i), 0)),
            pl.BlockSpec((tq, nkv), lambda b, i: (row(b, i), 0)),
            pl.BlockSpec((tq, nkv), lambda b, i: (row(b, i), 1)),
            kp_spec, vp_spec,
            pl.BlockSpec((tq, nq), lambda b, i: (row(b, i), 0)),
            pl.BlockSpec((nq // HEAD, tq, tk), lambda b, i: (0, 0, 0)),
            pl.BlockSpec((1, HEAD), lambda b, i: (0, 0)),
            pl.BlockSpec((1, HEAD), lambda b, i: (0, 0)),
        ],
        out_specs=[
            pl.BlockSpec((tq, nq), lambda b, i: (row(b, i), 0)),
            pl.BlockSpec((1, WINDOW, nkv), lambda b, i: (b, 0, 0)),
            pl.BlockSpec((1, WINDOW, nkv), lambda b, i: (b, 0, 0)),
        ],
        out_shape=[
            jax.ShapeDtypeStruct((m, nq), og_dtype),
            jax.ShapeDtypeStruct((bn, WINDOW, nkv), F32),
            jax.ShapeDtypeStruct((bn, WINDOW, nkv), F32),
        ],
        scratch_shapes=[pltpu.VMEM((tk, nkv), F32), pltpu.VMEM((tk, nkv), F32)],
        compiler_params=_params(("parallel", "arbitrary")),
        name="swa_attention",
    )(q, kv, kv, kp_arr, vp_arr, gate, bias, q_g.reshape(1, HEAD), k_g.reshape(1, HEAD))
    return og, kwin, vwin


def _attn_layer(x2d, bn, tn_, g, w_in, q_g, k_g, sinks, w_out, rel_bias, cache_k, cache_v, tm, og_dtype):
    d = x2d.shape[1]
    w3 = w_in.reshape(1, d, w_in.shape[1])
    nq = w_out.shape[0]
    nkv = A_KV_HEADS * HEAD
    q = _proj(x2d, g, w3, 0, nq, tm, 512)[0]
    kv = _proj(x2d, g, w3, nq, 2 * nkv, tm, 512)[0]
    gate = _proj(x2d, g, w3, nq + 2 * nkv, nq, tm, 512)[0]
    og, kwin, vwin = _attn_mix(q, kv, gate, rel_bias, q_g, k_g, sinks, bn, tn_, cache_k, cache_v, og_dtype)
    y = _outproj(og, w_out, x2d, tm, 512)
    shape = (bn, WINDOW, A_KV_HEADS, HEAD)
    return y, kwin.reshape(shape), vwin.reshape(shape)


def _rwkv_kernel(r_ref, k_ref, v_ref, g_ref, lw_ref, la_ref, wup_ref, vec_ref, s0_ref,
                 og_ref, sout_ref,
                 s_ref, kk_scr, k2_scr, a_scr, ld_scr, q_scr, o0_scr, gp_scr, z_scr, ec_scr, o_scr,
                 *, C, tb, lp, nt, unroll):
    t = pl.program_id(2)
    n = 2 * C
    nchunks = tb // C
    lane = lax.broadcasted_iota(jnp.int32, (1, LANES), 1)
    lo = (lane < HEAD).astype(F32)
    hi = 1.0 - lo

    @pl.when(t == 0)
    def _():
        z = jnp.zeros((HEAD, HEAD), F32)
        for pp in range(lp):
            s_ref[pp, 0:HEAD, :] = jnp.concatenate([s0_ref[0, 2 * pp], z], axis=-1)
            s_ref[pp, HEAD:LANES, :] = jnp.concatenate([z, s0_ref[0, 2 * pp + 1]], axis=-1)

    vec = vec_ref[...]
    w0, a0, kk_w, ka_w, rk_w, ln_g, ln_b = (vec[i:i + 1, :] for i in range(7))

    jr = lax.broadcasted_iota(jnp.int32, (LANES, LANES), 0)
    jc = lax.broadcasted_iota(jnp.int32, (LANES, LANES), 1)
    ones_bd = ((jr // HEAD) == (jc // HEAD)).astype(F32)
    mean_bd = ones_bd * (1.0 / HEAD)

    def per_pair(fn, *xs):
        return jnp.concatenate([fn(*(x[:, pp * LANES:(pp + 1) * LANES] for x in xs)) for pp in range(lp)], axis=-1)

    k = k_ref[0]
    xw = w0 + _mm(jnp.tanh(lw_ref[0]), wup_ref[0], NN, 3)
    ld_scr[...] = -jnp.exp(-_softplus(-xw) - 0.5)
    a = jax.nn.sigmoid(a0 + _mm(la_ref[0], wup_ref[1], NN, 3))
    a_scr[...] = a
    kkr = k * kk_w
    ssq = per_pair(lambda x: _mm(x * x, ones_bd, NN, 3), kkr)
    kk_scr[...] = kkr / jnp.maximum(jnp.sqrt(ssq), 1e-12)
    k2_scr[...] = k * (1.0 + (a - 1.0) * ka_w)

    rr = lax.broadcasted_iota(jnp.int32, (n, n), 0)
    cc = lax.broadcasted_iota(jnp.int32, (n, n), 1)
    strict = cc < rr
    incl = cc <= rr
    eye = (cc == rr).astype(F32)
    tri = _tri(C)
    nsteps = int(math.log2(C)) - 1

    def stage_b(gi, carry):
        for cu in range(unroll):
            ci = gi * unroll + cu
            rows = pl.ds(pl.multiple_of(ci * C, C), C)
            ld = ld_scr[rows, :]
            b = _mm_exact_lhs(tri, ld)
            eb = jnp.exp(b)
            enb = jnp.exp(-b)
            kk = kk_scr[rows, :]
            al = -kk * jnp.exp(b - ld)
            be = kk * a_scr[rows, :] * enb
            kt = k2_scr[rows, :] * enb
            rb = r_ref[0, rows, :] * eb
            v = v_ref[0, rows, :]
            e_c = eb[C - 1:C, :]
            ec_scr[ci] = jnp.broadcast_to(e_c, (8, e_c.shape[1]))
            for pp in range(lp):
                ls = slice(pp * LANES, (pp + 1) * LANES)
                cat = lambda x: jnp.concatenate([x[:, ls] * lo, x[:, ls] * hi], axis=0)
                la_, lr_, rb_, rk_, vb_ = cat(al), cat(rb), cat(be), cat(kt), cat(v)
                m_ab = jnp.where(strict, _mm(la_, rb_, NT), 0.0)
                m_ak = jnp.where(strict, _mm(la_, rk_, NT), 0.0)
                m_rb = jnp.where(incl, _mm(lr_, rb_, NT), 0.0)
                m_rk = jnp.where(incl, _mm(lr_, rk_, NT), 0.0)
                apow = m_ab
                tinv = eye + apow
                for _ in range(nsteps):
                    apow = _mm(apow, apow)
                    tinv = tinv + _mm(apow, tinv)
                w = _mm(tinv, la_)
                akv = _mm(m_ak, vb_)
                u0 = _mm(tinv, akv)
                ecp = e_c[:, ls]
                rbe = rb_ * ecp
                idx = ci * lp + pp
                q_scr[idx] = lr_ + _mm(m_rb, w)
                o0_scr[idx] = _mm(m_rb, u0) + _mm(m_rk, vb_)
                gp_scr[idx] = _mm(w, rbe, TN)
                z_scr[idx] = _mm(u0, rbe, TN) + _mm(vb_, rk_ * ecp, TN)
        return carry

    lax.fori_loop(0, nchunks // unroll, stage_b, 0)

    for ci in range(nchunks):
        for pp in range(lp):
            ls = slice(pp * LANES, (pp + 1) * LANES)
            idx = ci * lp + pp
            s = s_ref[pp]
            o_bd = _mm(q_scr[idx], s, NT) + o0_scr[idx]
            o_scr[ci * C:(ci + 1) * C, ls] = o_bd[0:C] + o_bd[C:n]
            s_ref[pp] = s * ec_scr[ci, 0:1, ls] + _mm(s, gp_scr[idx]) + z_scr[idx]

    o = o_scr[...]
    v = v_ref[0]
    mean = per_pair(lambda x: _mm(x, mean_bd, NN, 3), o)
    dlt = o - mean
    var = per_pair(lambda x: _mm(x, mean_bd, NN, 3), dlt * dlt)
    on = dlt * lax.rsqrt(var + B_LN_EPS) * ln_g + ln_b
    on = on + per_pair(lambda x: _mm(x, ones_bd, NN, 3), r_ref[0] * k2_scr[...] * rk_w) * v
    og_ref[...] = (on * _silu(g_ref[0])).astype(og_ref.dtype)

    @pl.when(t == nt - 1)
    def _():
        for pp in range(lp):
            sout_ref[0, 2 * pp] = s_ref[pp, 0:HEAD, 0:HEAD]
            sout_ref[0, 2 * pp + 1] = s_ref[pp, HEAD:LANES, HEAD:LANES]


def _rwkv_mix(rkvg, lora, wup, vec, s0, bn, tn_, C, tb, lp, unroll, og_dtype):
    m = rkvg.shape[1]
    nch = rkvg.shape[2]
    wl = lp * LANES
    ngrp = nch // wl
    nt = tn_ // tb
    nchunks = tb // C
    rank = lora.shape[2]
    nh = s0.shape[1]
    row = lambda b, t: b * nt + t

    def xspec(c):
        return pl.BlockSpec((1, tb, wl), lambda b, p, t: (c, row(b, t), p))

    def lspec(c):
        return pl.BlockSpec((1, tb, rank), lambda b, p, t: (c, row(b, t), 0))

    og, sout = pl.pallas_call(
        functools.partial(_rwkv_kernel, C=C, tb=tb, lp=lp, nt=nt, unroll=unroll),
        grid=(bn, ngrp, nt),
        in_specs=[
            xspec(0), xspec(1), xspec(2), xspec(3), lspec(0), lspec(1),
            pl.BlockSpec((2, rank, wl), lambda b, p, t: (0, 0, p)),
            pl.BlockSpec((8, wl), lambda b, p, t: (0, p)),
            pl.BlockSpec((1, 2 * lp, HEAD, HEAD), lambda b, p, t: (b, p, 0, 0)),
        ],
        out_specs=[
            pl.BlockSpec((tb, wl), lambda b, p, t: (row(b, t), p)),
            pl.BlockSpec((1, 2 * lp, HEAD, HEAD), lambda b, p, t: (b, p, 0, 0)),
        ],
        out_shape=[
            jax.ShapeDtypeStruct((m, nch), og_dtype),
            jax.ShapeDtypeStruct((bn, nh, HEAD, HEAD), F32),
        ],
        scratch_shapes=[
            pltpu.VMEM((lp, LANES, LANES), F32),
            pltpu.VMEM((tb, wl), F32), pltpu.VMEM((tb, wl), F32), pltpu.VMEM((tb, wl), F32), pltpu.VMEM((tb, wl), F32),
            pltpu.VMEM((nchunks * lp, 2 * C, LANES), F32), pltpu.VMEM((nchunks * lp, 2 * C, LANES), F32),
            pltpu.VMEM((nchunks * lp, LANES, LANES), F32), pltpu.VMEM((nchunks * lp, LANES, LANES), F32),
            pltpu.VMEM((nchunks, 8, wl), F32),
            pltpu.VMEM((tb, wl), F32),
        ],
        compiler_params=_params(("parallel", "parallel", "arbitrary")),
        name="rwkv7_chunked",
    )(rkvg, rkvg, rkvg, rkvg, lora, lora, wup, vec, s0)
    return og, sout


def _rwkv_layer(x2d, bn, tn_, g, shift, s0, mu, w_rkvg, w_down, w_up, w0, a0, k_k, k_a, r_k, ln_g, ln_b,
                w_out, tm, C, tb, lp, unroll, og_dtype):
    m, d = x2d.shape
    nblk = m // tm
    if tm <= tn_:
        starts = jnp.arange(nblk) * tm
        prev = _rms_rows(x2d[jnp.maximum(starts - 1, 0)], g)
        first = jnp.where((starts % tn_ == 0)[:, None], shift[starts // tn_], prev).reshape(nblk, 1, d)
        period = None
    else:
        assert nblk == 1
        first = jnp.zeros((bn, tn_, d), F32).at[:, 0].set(shift).reshape(1, m, d)
        period = tn_
    shift_out = _rms_rows(x2d[tn_ - 1::tn_], g)
    rkvg = _proj(x2d, g, w_rkvg, 0, w_rkvg.shape[2], tm, 512, first=first, mu=mu[:4], period=period)
    rank = w_down.shape[2]
    lora = _proj(x2d, g, w_down, 0, rank, tm, rank, first=first, mu=mu[4:6], period=period)
    nch = w_rkvg.shape[2]
    vec = jnp.stack([w0, a0, k_k, k_a, r_k.reshape(nch), ln_g, ln_b, jnp.zeros((nch,), F32)])
    og, sout = _rwkv_mix(rkvg, lora, w_up, vec, s0, bn, tn_, C, tb, lp, unroll, og_dtype)
    y = _outproj(og, w_out, x2d, tm, 512)
    return y, sout, shift_out


GLA_SUB = 16


def _gla_kernel(q_ref, k_ref, v_ref, gate_ref, gl_ref, wup_ref, bgk_ref, og_w_ref, s0_ref,
                og_ref, sout_ref, st_ref, *, C, nchunks, nt, passes):
    t = pl.program_id(2)
    dk = q_ref.shape[-1]
    sub = min(GLA_SUB, C)
    nsub = C // sub
    qscale = dk ** -0.5

    @pl.when(t == 0)
    def _():
        st_ref[...] = s0_ref[0, 0].T

    tri = _tri(C)
    wup = wup_ref[...]
    bgk = bgk_ref[...]
    og_w = og_w_ref[...]

    def chunk(ci, carry):
        rows = pl.ds(pl.multiple_of(ci * C, C), C)
        q = q_ref[rows, :] * qscale
        k = k_ref[rows, :]
        v = v_ref[rows, :]
        gate = gate_ref[rows, :]
        gl = gl_ref[rows, :]
        glog = -_softplus(-(_mm(gl, wup, NN, 3) + bgk)) * (1.0 / C_GATE_NORM)
        b = _mm_exact_lhs(tri, glog)
        st = st_ref[...]
        o = _mm(q * jnp.exp(b), st, NT, passes)
        parts = []
        for i in range(nsub):
            r0 = i * sub
            nk = r0 + sub
            ref = b[r0:r0 + 1, :] - glog[r0:r0 + 1, :]
            qi = q[r0:nk] * jnp.exp(b[r0:nk] - ref)
            ki = k[0:nk] * jnp.exp(ref - b[0:nk])
            att = _mm(qi, ki, NT, passes)
            ar = lax.broadcasted_iota(jnp.int32, (sub, nk), 0) + r0
            ac = lax.broadcasted_iota(jnp.int32, (sub, nk), 1)
            att = jnp.where(ac <= ar, att, 0.0)
            parts.append(_mm(att, v[0:nk], NN, passes))
        o = o + (jnp.concatenate(parts, axis=0) if nsub > 1 else parts[0])
        bl = b[C - 1:C, :]
        st_ref[...] = st * jnp.exp(bl) + _mm(v, k * jnp.exp(bl - b), TN, passes)
        on = o * lax.rsqrt(jnp.mean(o * o, axis=-1, keepdims=True) + NORM_EPS) * og_w
        og_ref[rows, :] = (on * _silu(gate)).astype(og_ref.dtype)
        return carry

    lax.fori_loop(0, nchunks, chunk, 0)

    @pl.when(t == nt - 1)
    def _():
        sout_ref[0, 0] = st_ref[...].T


def _gla_mix(qk, v, gate, gl, w_up, b_gk, o_g, s0, bn, tn_, C, tb, og_dtype, passes):
    m = qk.shape[0]
    nh, dk, dv = s0.shape[1:]
    nt = tn_ // tb
    rank = gl.shape[1]
    row = lambda b, t: b * nt + t
    og, sout = pl.pallas_call(
        functools.partial(_gla_kernel, C=C, nchunks=tb // C, nt=nt, passes=passes),
        grid=(bn, nh, nt),
        in_specs=[
            pl.BlockSpec((tb, dk), lambda b, h, t: (row(b, t), h)),
            pl.BlockSpec((tb, dk), lambda b, h, t: (row(b, t), nh + h)),
            pl.BlockSpec((tb, dv), lambda b, h, t: (row(b, t), h)),
            pl.BlockSpec((tb, dv), lambda b, h, t: (row(b, t), h)),
            pl.BlockSpec((tb, rank), lambda b, h, t: (row(b, t), 0)),
            pl.BlockSpec((rank, dk), lambda b, h, t: (0, h)),
            pl.BlockSpec((1, dk), lambda b, h, t: (0, h)),
            pl.BlockSpec((1, dv), lambda b, h, t: (0, 0)),
            pl.BlockSpec((1, 1, dk, dv), lambda b, h, t: (b, h, 0, 0)),
        ],
        out_specs=[
            pl.BlockSpec((tb, dv), lambda b, h, t: (row(b, t), h)),
            pl.BlockSpec((1, 1, dk, dv), lambda b, h, t: (b, h, 0, 0)),
        ],
        out_shape=[
            jax.ShapeDtypeStruct((m, nh * dv), og_dtype),
            jax.ShapeDtypeStruct((bn, nh, dk, dv), F32),
        ],
        scratch_shapes=[pltpu.VMEM((dv, dk), F32)],
        compiler_params=_params(("parallel", "parallel", "arbitrary")),
        name="gla_chunked",
    )(qk, qk, v, gate, gl, w_up, b_gk.reshape(1, -1), o_g.reshape(1, -1), s0)
    return og, sout


def _gla_layer(x2d, bn, tn_, g, s0, w_in, w_up, b_gk, o_g, w_out, tm, C, tb, og_dtype, passes):
    d = x2d.shape[1]
    nh, dk, dv = s0.shape[1:]
    ckey = nh * dk
    dinner = nh * dv
    rank = w_up.shape[0]
    w3 = w_in.reshape(1, d, w_in.shape[1])
    qk = _proj(x2d, g, w3, 0, 2 * ckey, tm, 512)[0]
    v = _proj(x2d, g, w3, 2 * ckey, dinner, tm, 512)[0]
    gate = _proj(x2d, g, w3, 2 * ckey + dinner, dinner, tm, 512)[0]
    gl = _proj(x2d, g, w_in[:, 2 * ckey + 2 * dinner:].reshape(1, d, rank), 0, rank, tm, rank)[0]
    og, sout = _gla_mix(qk, v, gate, gl, w_up, b_gk, o_g, s0, bn, tn_, C, tb, og_dtype, passes)
    y = _outproj(og, w_out, x2d, tm, 512)
    return y, sout


PROMPT_TM = 1024
RWKV_CHUNK = 64
RWKV_TB = 512
RWKV_PAIRS = 4
RWKV_UNROLL = 2
GLA_CHUNK = 64
GLA_TB = 512
CHUNK_PASSES = 1


def kernel(x_prompt, x_sample, cache_k_win, cache_v_win, state_wkv, state_shift, state_gla, norm_g, rel_bias, w_in_a, q_norm_g, k_norm_g, sinks, w_out_a, mu_b, w_rkvg_b, w_lora_down_b, w_lora_up_b, w0_b, a0_b, k_k_b, k_a_b, r_k_b, ln_x_g_b, ln_x_b_b, w_out_b, w_in_c, w_gk_up_c, b_gk_c, o_norm_g_c, w_out_c):
    bp, tp, d = x_prompt.shape
    bs, ts, _ = x_sample.shape
    depth = norm_g.shape[0]
    xp = x_prompt.reshape(bp * tp, d)
    xs = x_sample.reshape(bs * ts, d)
    ms = bs * ts
    kwp, vwp, kws, vws, wkvp, shp, wkvs, shs, glap, glas = ([] for _ in range(10))
    for layer in range(depth):
        kind, j = layer % 3, layer // 3
        g = norm_g[layer]
        if kind == 0:
            wa = (w_in_a[j], q_norm_g[j], k_norm_g[j], sinks[j], w_out_a[j], rel_bias)
            xp, kp_, vp_ = _attn_layer(xp, bp, tp, g, *wa, None, None, PROMPT_TM, BF16)
            xs, ks_, vs_ = _attn_layer(xs, bs, ts, g, *wa, cache_k_win[j], cache_v_win[j], ms, F32)
            kwp.append(kp_); vwp.append(vp_); kws.append(ks_); vws.append(vs_)
        elif kind == 1:
            wb = (mu_b[j], w_rkvg_b[j], w_lora_down_b[j], w_lora_up_b[j], w0_b[j], a0_b[j], k_k_b[j], k_a_b[j],
                  r_k_b[j], ln_x_g_b[j], ln_x_b_b[j], w_out_b[j])
            nh = w_rkvg_b.shape[3] // HEAD
            xp, sp_, lp_ = _rwkv_layer(xp, bp, tp, g, jnp.zeros((bp, d), F32), jnp.zeros((bp, nh, HEAD, HEAD), F32),
                                       *wb, PROMPT_TM, RWKV_CHUNK, RWKV_TB, RWKV_PAIRS, RWKV_UNROLL, BF16)
            xs, ss_, ls_ = _rwkv_layer(xs, bs, ts, g, state_shift[j], state_wkv[j], *wb, ms, ts, ts, RWKV_PAIRS, 1, F32)
            wkvp.append(sp_); shp.append(lp_); wkvs.append(ss_); shs.append(ls_)
        else:
            wc = (w_in_c[j], w_gk_up_c[j], b_gk_c[j], o_norm_g_c[j], w_out_c[j])
            xp, sp_ = _gla_layer(xp, bp, tp, g, jnp.zeros((bp,) + state_gla.shape[2:], F32), *wc,
                                 PROMPT_TM, GLA_CHUNK, GLA_TB, BF16, CHUNK_PASSES)
            xs, ss_ = _gla_layer(xs, bs, ts, g, state_gla[j], *wc, ms, ts, ts, F32, CHUNK_PASSES)
            glap.append(sp_); glas.append(ss_)
    return (xp.reshape(bp, tp, d), xs.reshape(bs, ts, d),
            jnp.stack(kwp), jnp.stack(vwp), jnp.stack(kws), jnp.stack(vws),
            jnp.stack(wkvp), jnp.stack(shp), jnp.stack(wkvs), jnp.stack(shs),
            jnp.stack(glap), jnp.stack(glas))
```

```python
import functools
import math

import jax
import jax.numpy as jnp
from jax import lax
from jax.experimental import pallas as pl
from jax.experimental.pallas import tpu as pltpu

F32 = jnp.float32
BF16 = jnp.bfloat16

NORM_EPS = 1e-6
HEAD = 64
LANES = 128
WINDOW = 128
N_BUCKETS = 32
MAX_EXACT = N_BUCKETS // 2
MAX_DISTANCE = 128
B_LN_EPS = 64e-5
C_GATE_NORM = 16.0
VMEM_LIMIT = 56 * 1024 * 1024

NN = (((1,), (0,)), ((), ()))
NT = (((1,), (1,)), ((), ()))
TN = (((0,), (0,)), ((), ()))


def _params(sem):
    return pltpu.CompilerParams(dimension_semantics=sem, vmem_limit_bytes=VMEM_LIMIT)


def _split(x):
    hi = x.astype(BF16)
    lo = (x - hi.astype(F32)).astype(BF16)
    return hi, lo


def _mm(a, b, dims=NN, passes=1):
    if passes == 1:
        return lax.dot_general(a.astype(BF16), b.astype(BF16), dims, preferred_element_type=F32)
    a_hi, a_lo = _split(a)
    b_hi, b_lo = _split(b)
    dg = functools.partial(lax.dot_general, dimension_numbers=dims, preferred_element_type=F32)
    return dg(a_hi, b_hi) + (dg(a_hi, b_lo) + dg(a_lo, b_hi))


def _mm_exact_lhs(a_bf16, b):
    b0 = b.astype(BF16)
    r1 = b - b0.astype(F32)
    b1 = r1.astype(BF16)
    b2 = (r1 - b1.astype(F32)).astype(BF16)
    dg = functools.partial(lax.dot_general, dimension_numbers=NN, preferred_element_type=F32)
    return dg(a_bf16, b0) + (dg(a_bf16, b1) + dg(a_bf16, b2))


def _softplus(z):
    return jnp.maximum(z, 0.0) + jnp.log1p(jnp.exp(-jnp.abs(z)))


def _silu(g):
    return g * jax.nn.sigmoid(g)


def _tri(c):
    r = lax.broadcasted_iota(jnp.int32, (c, c), 0)
    col = lax.broadcasted_iota(jnp.int32, (c, c), 1)
    return (col <= r).astype(BF16)


def _rms_rows_kernel(x_ref, g_ref, o_ref):
    x = x_ref[...]
    o_ref[...] = x * lax.rsqrt(jnp.mean(x * x, axis=-1, keepdims=True) + NORM_EPS) * g_ref[...]


def _rms_rows(rows, g):
    n, d = rows.shape
    npad = -(-n // 8) * 8
    rows_p = jnp.pad(rows, ((0, npad - n), (0, 0)))
    out = pl.pallas_call(
        _rms_rows_kernel,
        out_shape=jax.ShapeDtypeStruct((npad, d), F32),
        name="rms_rows",
    )(rows_p, g.reshape(1, d))
    return out[:n]


PROLOGUE_ROWS = 256


def _proj_kernel(*refs, lerp, period, tm):
    if lerp:
        x_ref, g_ref, w_ref, first_ref, mu_ref, o_ref, xm_ref = refs
    else:
        x_ref, g_ref, w_ref, o_ref, xm_ref = refs
    j = pl.program_id(2)

    @pl.when(j == 0)
    def _():
        g = g_ref[...]
        rc = min(PROLOGUE_ROWS, tm)
        carry = first_ref[0, 0:1, :] if lerp else None
        for c0 in range(0, tm, rc):
            x = x_ref[c0:c0 + rc, :]
            h = x * lax.rsqrt(jnp.mean(x * x, axis=-1, keepdims=True) + NORM_EPS) * g
            if lerp:
                row = lax.broadcasted_iota(jnp.int32, h.shape, 0)
                hs = jnp.where(row == 0, carry, pltpu.roll(h, 1, 0))
                if period is not None:
                    hs = jnp.where(row % period == 0, first_ref[0, c0:c0 + rc, :], hs)
                carry = h[rc - 1:rc, :]
                h = h + (hs - h) * mu_ref[0]
            xm_ref[c0:c0 + rc, :] = h.astype(BF16)

    o_ref[0] = jnp.dot(xm_ref[...], w_ref[0].astype(BF16), preferred_element_type=F32)


def _proj(x2d, g, w3, col_off, n_out, tm, tn, first=None, mu=None, period=None):
    m, d = x2d.shape
    nc = w3.shape[0]
    lerp = first is not None
    assert m % tm == 0 and n_out % tn == 0 and col_off % tn == 0
    joff = col_off // tn
    in_specs = [
        pl.BlockSpec((tm, d), lambda i, c, j: (i, 0)),
        pl.BlockSpec((1, d), lambda i, c, j: (0, 0)),
        pl.BlockSpec((1, d, tn), lambda i, c, j: (c, 0, joff + j)),
    ]
    args = [x2d, g.reshape(1, d), w3]
    if lerp:
        fr = first.shape[1]
        in_specs += [
            pl.BlockSpec((1, fr, d), lambda i, c, j: (i, 0, 0)),
            pl.BlockSpec((1, 1, d), lambda i, c, j: (c, 0, 0)),
        ]
        args += [first, mu.reshape(nc, 1, d)]
    return pl.pallas_call(
        functools.partial(_proj_kernel, lerp=lerp, period=period, tm=tm),
        grid=(m // tm, nc, n_out // tn),
        in_specs=in_specs,
        out_specs=pl.BlockSpec((1, tm, tn), lambda i, c, j: (c, i, j)),
        out_shape=jax.ShapeDtypeStruct((nc, m, n_out), F32),
        scratch_shapes=[pltpu.VMEM((tm, d), BF16)],
        compiler_params=_params(("parallel", "arbitrary", "arbitrary")),
        name="proj",
    )(*args)


def _outproj_kernel(a_ref, w_ref, x_ref, o_ref):
    o_ref[...] = x_ref[...] + jnp.dot(a_ref[...].astype(BF16), w_ref[...].astype(BF16),
                                      preferred_element_type=F32)


def _outproj(a2d, w, x2d, tm, tn):
    m, kdim = a2d.shape
    n = w.shape[1]
    return pl.pallas_call(
        _outproj_kernel,
        grid=(m // tm, n // tn),
        in_specs=[
            pl.BlockSpec((tm, kdim), lambda i, j: (i, 0)),
            pl.BlockSpec((kdim, tn), lambda i, j: (0, j)),
            pl.BlockSpec((tm, tn), lambda i, j: (i, j)),
        ],
        out_specs=pl.BlockSpec((tm, tn), lambda i, j: (i, j)),
        out_shape=jax.ShapeDtypeStruct((m, n), F32),
        compiler_params=_params(("parallel", "arbitrary")),
        name="outproj",
    )(a2d, w, x2d)


A_KV_HEADS = 8
A_GROUP = 8


def _attn_kernel(q_ref, kc_ref, vc_ref, kp_ref, vp_ref, gate_ref, bias_ref, qg_ref, kg_ref,
                 og_ref, kwin_ref, vwin_ref, k_scr, v_scr, *, tq, prompt, nblocks):
    i = pl.program_id(1)
    tk = WINDOW + tq
    scale = HEAD ** -0.5
    kg = kg_ref[...]
    qg = qg_ref[...]

    def head_norm(xh, gain):
        return xh * lax.rsqrt(jnp.mean(xh * xh, axis=-1, keepdims=True) + NORM_EPS) * gain

    for kh in range(A_KV_HEADS):
        sl = slice(kh * HEAD, (kh + 1) * HEAD)
        kprev = kp_ref[:, sl]
        if prompt:
            kprev = head_norm(kprev, kg)
        k_scr[0:WINDOW, sl] = kprev
        k_scr[WINDOW:tk, sl] = head_norm(kc_ref[:, sl], kg)
    v_scr[0:WINDOW, :] = vp_ref[...]
    v_scr[WINDOW:tk, :] = vc_ref[...]
    k_scr[0:1, :] = jnp.zeros((1, k_scr.shape[1]), F32)
    v_scr[0:1, :] = jnp.zeros((1, v_scr.shape[1]), F32)

    rows = A_GROUP * tq
    r = lax.broadcasted_iota(jnp.int32, (rows, tk), 0) % tq
    c = lax.broadcasted_iota(jnp.int32, (rows, tk), 1)
    dist = WINDOW + r - c
    valid = (dist >= 0) & (dist < WINDOW)
    if prompt:
        valid = valid & ((i > 0) | (c >= WINDOW))
    valid = valid | (c == 0)

    for kh in range(A_KV_HEADS):
        kk = k_scr[:, kh * HEAD:(kh + 1) * HEAD].astype(BF16)
        vv = v_scr[:, kh * HEAD:(kh + 1) * HEAD].astype(BF16)
        h0 = kh * A_GROUP
        qs = jnp.concatenate([q_ref[:, (h0 + e) * HEAD:(h0 + e + 1) * HEAD] for e in range(A_GROUP)], axis=0)
        qs = head_norm(qs, qg)
        s = lax.dot_general(qs.astype(BF16), kk, NT, preferred_element_type=F32) * scale
        s = jnp.where(valid, s + bias_ref[h0:h0 + A_GROUP].reshape(rows, tk), -jnp.inf)
        p = jnp.exp(s - jnp.max(s, axis=-1, keepdims=True))
        o = jnp.dot(p.astype(BF16), vv, preferred_element_type=F32) / jnp.sum(p, axis=-1, keepdims=True)
        lanes = slice(h0 * HEAD, (h0 + A_GROUP) * HEAD)
        o = jnp.concatenate([o[e * tq:(e + 1) * tq] for e in range(A_GROUP)], axis=-1)
        og_ref[:, lanes] = (o * _silu(gate_ref[:, lanes])).astype(og_ref.dtype)

    @pl.when(i == nblocks - 1)
    def _():
        kwin_ref[0] = k_scr[tk - WINDOW:tk, :]
        vwin_ref[0] = v_scr[tk - WINDOW:tk, :]


def _t5_bucket(dist):
    d = jnp.maximum(dist, 0)
    large = MAX_EXACT + (jnp.log(jnp.maximum(d, 1).astype(F32) / MAX_EXACT)
                         / math.log(MAX_DISTANCE / MAX_EXACT) * (N_BUCKETS - MAX_EXACT)).astype(jnp.int32)
    return jnp.where(d < MAX_EXACT, d, jnp.minimum(large, N_BUCKETS - 1))


def _attn_mix(q, kv, gate, rel_bias, q_g, k_g, sinks, bn, tn_, cache_k, cache_v, og_dtype):
    m = q.shape[0]
    prompt = cache_k is None
    tq = WINDOW if prompt else tn_
    nb = tn_ // tq
    tk = WINDOW + tq
    dist = WINDOW + jnp.arange(tq)[:, None] - jnp.arange(tk)[None, :]
    bias = jnp.transpose(rel_bias[_t5_bucket(dist)], (2, 0, 1)).astype(F32)
    bias = bias.at[:, :, 0].set(sinks.astype(F32)[:, None])
    nkv = A_KV_HEADS * HEAD
    row = lambda b, i: b * nb + i
    if prompt:
        kp_arr, vp_arr = kv, kv
        kp_spec = pl.BlockSpec((WINDOW, nkv), lambda b, i: (jnp.maximum(row(b, i) - 1, 0), 0))
        vp_spec = pl.BlockSpec((WINDOW, nkv), lambda b, i: (jnp.maximum(row(b, i) - 1, 0), 1))
    else:
        kp_arr = cache_k.reshape(bn * WINDOW, nkv)
        vp_arr = cache_v.reshape(bn * WINDOW, nkv)
        kp_spec = pl.BlockSpec((WINDOW, nkv), lambda b, i: (b, 0))
        vp_spec = pl.BlockSpec((WINDOW, nkv), lambda b, i: (b, 0))
    nq = q.shape[1]
    og, kwin, vwin = pl.pallas_call(
        functools.partial(_attn_kernel, tq=tq, prompt=prompt, nblocks=nb),
        grid=(bn, nb),
        in_specs=[
            pl.BlockSpec((tq, nq), lambda b, i: (row(b, i), 0)),
            pl.BlockSpec((tq, nkv), lambda b, i: (row(b, i), 0)),
            pl.BlockSpec((tq, nkv), lambda b, i: (row(b, i), 1)),
            kp_spec, vp_spec,
            pl.BlockSpec((tq, nq), lambda b, i: (row(b, i), 0)),
            pl.BlockSpec((nq // HEAD, tq, tk), lambda b, i: (0, 0, 0)),
            pl.BlockSpec((1, HEAD), lambda b, i: (0, 0)),
            pl.BlockSpec((1, HEAD), lambda b, i: (0, 0)),
        ],
        out_specs=[
            pl.BlockSpec((tq, nq), lambda b, i: (row(b, i), 0)),
            pl.BlockSpec((1, WINDOW, nkv), lambda b, i: (b, 0, 0)),
            pl.BlockSpec((1, WINDOW, nkv), lambda b, i: (b, 0, 0)),
        ],
        out_shape=[
            jax.ShapeDtypeStruct((m, nq), og_dtype),
            jax.ShapeDtypeStruct((bn, WINDOW, nkv), F32),
            jax.ShapeDtypeStruct((bn, WINDOW, nkv), F32),
        ],
        scratch_shapes=[pltpu.VMEM((tk, nkv), F32), pltpu.VMEM((tk, nkv), F32)],
        compiler_params=_params(("parallel", "arbitrary")),
        name="swa_attention",
    )(q, kv, kv, kp_arr, vp_arr, gate, bias, q_g.reshape(1, HEAD), k_g.reshape(1, HEAD))
    return og, kwin, vwin


def _attn_layer(x2d, bn, tn_, g, w_in, q_g, k_g, sinks, w_out, rel_bias, cache_k, cache_v, tm, og_dtype):
    d = x2d.shape[1]
    w3 = w_in.reshape(1, d, w_in.shape[1])
    nq = w_out.shape[0]
    nkv = A_KV_HEADS * HEAD
    q = _proj(x2d, g, w3, 0, nq, tm, 512)[0]
    kv = _proj(x2d, g, w3, nq, 2 * nkv, tm, 512)[0]
    gate = _proj(x2d, g, w3, nq + 2 * nkv, nq, tm, 512)[0]
    og, kwin, vwin = _attn_mix(q, kv, gate, rel_bias, q_g, k_g, sinks, bn, tn_, cache_k, cache_v, og_dtype)
    y = _outproj(og, w_out, x2d, tm, 512)
    shape = (bn, WINDOW, A_KV_HEADS, HEAD)
    return y, kwin.reshape(shape), vwin.reshape(shape)


def _rwkv_kernel(r_ref, k_ref, v_ref, g_ref, lw_ref, la_ref, wup_ref, vec_ref, s0_ref,
                 og_ref, sout_ref,
                 s_ref, kk_scr, k2_scr, a_scr, ld_scr, q_scr, o0_scr, gp_scr, z_scr, ec_scr, o_scr,
                 *, C, tb, lp, nt, unroll):
    t = pl.program_id(2)
    n = 2 * C
    nchunks = tb // C
    lane = lax.broadcasted_iota(jnp.int32, (1, LANES), 1)
    lo = (lane < HEAD).astype(F32)
    hi = 1.0 - lo

    @pl.when(t == 0)
    def _():
        z = jnp.zeros((HEAD, HEAD), F32)
        for pp in range(lp):
            s_ref[pp, 0:HEAD, :] = jnp.concatenate([s0_ref[0, 2 * pp], z], axis=-1)
            s_ref[pp, HEAD:LANES, :] = jnp.concatenate([z, s0_ref[0, 2 * pp + 1]], axis=-1)

    vec = vec_ref[...]
    w0, a0, kk_w, ka_w, rk_w, ln_g, ln_b = (vec[i:i + 1, :] for i in range(7))

    jr = lax.broadcasted_iota(jnp.int32, (LANES, LANES), 0)
    jc = lax.broadcasted_iota(jnp.int32, (LANES, LANES), 1)
    ones_bd = ((jr // HEAD) == (jc // HEAD)).astype(F32)
    mean_bd = ones_bd * (1.0 / HEAD)

    def per_pair(fn, *xs):
        return jnp.concatenate([fn(*(x[:, pp * LANES:(pp + 1) * LANES] for x in xs)) for pp in range(lp)], axis=-1)

    k = k_ref[0]
    xw = w0 + _mm(jnp.tanh(lw_ref[0]), wup_ref[0], NN, 3)
    ld_scr[...] = -jnp.exp(-_softplus(-xw) - 0.5)
    a = jax.nn.sigmoid(a0 + _mm(la_ref[0], wup_ref[1], NN, 3))
    a_scr[...] = a
    kkr = k * kk_w
    ssq = per_pair(lambda x: _mm(x * x, ones_bd, NN, 3), kkr)
    kk_scr[...] = kkr / jnp.maximum(jnp.sqrt(ssq), 1e-12)
    k2_scr[...] = k * (1.0 + (a - 1.0) * ka_w)

    rr = lax.broadcasted_iota(jnp.int32, (n, n), 0)
    cc = lax.broadcasted_iota(jnp.int32, (n, n), 1)
    strict = cc < rr
    incl = cc <= rr
    eye = (cc == rr).astype(F32)
    tri = _tri(C)
    nsteps = int(math.log2(C)) - 1

    def stage_b(gi, carry):
        units = []
        for cu in range(unroll):
            ci = gi * unroll + cu
            rows = pl.ds(pl.multiple_of(ci * C, C), C)
            ld = ld_scr[rows, :]
            b = _mm_exact_lhs(tri, ld)
            eb = jnp.exp(b)
            enb = jnp.exp(-b)
            kk = kk_scr[rows, :]
            al = -kk * jnp.exp(b - ld)
            be = kk * a_scr[rows, :] * enb
            kt = k2_scr[rows, :] * enb
            rb = r_ref[0, rows, :] * eb
            v = v_ref[0, rows, :]
            e_c = eb[C - 1:C, :]
            ec_scr[ci] = jnp.broadcast_to(e_c, (8, e_c.shape[1]))
            for pp in range(lp):
                ls = slice(pp * LANES, (pp + 1) * LANES)
                cat = lambda x: jnp.concatenate([x[:, ls] * lo, x[:, ls] * hi], axis=0)
                units.append(dict(idx=ci * lp + pp, ecp=e_c[:, ls], la=cat(al), lr=cat(rb), rb=cat(be),
                                  rk=cat(kt), vb=cat(v)))
        each = lambda fn: [fn(u) for u in units]
        m_ab = each(lambda u: jnp.where(strict, _mm(u["la"], u["rb"], NT), 0.0))
        m_ak = each(lambda u: jnp.where(strict, _mm(u["la"], u["rk"], NT), 0.0))
        m_rb = each(lambda u: jnp.where(incl, _mm(u["lr"], u["rb"], NT), 0.0))
        m_rk = each(lambda u: jnp.where(incl, _mm(u["lr"], u["rk"], NT), 0.0))
        akv = [_mm(m, u["vb"]) for m, u in zip(m_ak, units)]
        rkv = [_mm(m, u["vb"]) for m, u in zip(m_rk, units)]
        apow = m_ab
        tinv = [eye + a_ for a_ in apow]
        for _ in range(nsteps):
            apow = [_mm(a_, a_) for a_ in apow]
            tinv = [t_ + _mm(a_, t_) for a_, t_ in zip(apow, tinv)]
        w = [_mm(t_, u["la"]) for t_, u in zip(tinv, units)]
        u0 = [_mm(t_, x_) for t_, x_ in zip(tinv, akv)]
        rbe = [u["rb"] * u["ecp"] for u in units]
        for i, u in enumerate(units):
            q_scr[u["idx"]] = u["lr"] + _mm(m_rb[i], w[i])
        for i, u in enumerate(units):
            o0_scr[u["idx"]] = _mm(m_rb[i], u0[i]) + rkv[i]
        for i, u in enumerate(units):
            gp_scr[u["idx"]] = _mm(w[i], rbe[i], TN)
        for i, u in enumerate(units):
            z_scr[u["idx"]] = _mm(u0[i], rbe[i], TN) + _mm(u["vb"], u["rk"] * u["ecp"], TN)
        return carry

    lax.fori_loop(0, nchunks // unroll, stage_b, 0)

    for ci in range(nchunks):
        for pp in range(lp):
            ls = slice(pp * LANES, (pp + 1) * LANES)
            idx = ci * lp + pp
            s = s_ref[pp]
            o_bd = _mm(q_scr[idx], s, NT) + o0_scr[idx]
            o_scr[ci * C:(ci + 1) * C, ls] = o_bd[0:C] + o_bd[C:n]
            s_ref[pp] = s * ec_scr[ci, 0:1, ls] + _mm(s, gp_scr[idx]) + z_scr[idx]

    o = o_scr[...]
    v = v_ref[0]
    mean = per_pair(lambda x: _mm(x, mean_bd, NN, 3), o)
    dlt = o - mean
    var = per_pair(lambda x: _mm(x, mean_bd, NN, 3), dlt * dlt)
    on = dlt * lax.rsqrt(var + B_LN_EPS) * ln_g + ln_b
    on = on + per_pair(lambda x: _mm(x, ones_bd, NN, 3), r_ref[0] * k2_scr[...] * rk_w) * v
    og_ref[...] = (on * _silu(g_ref[0])).astype(og_ref.dtype)

    @pl.when(t == nt - 1)
    def _():
        for pp in range(lp):
            sout_ref[0, 2 * pp] = s_ref[pp, 0:HEAD, 0:HEAD]
            sout_ref[0, 2 * pp + 1] = s_ref[pp, HEAD:LANES, HEAD:LANES]


def _rwkv_mix(rkvg, lora, wup, vec, s0, bn, tn_, C, tb, lp, unroll, og_dtype):
    m = rkvg.shape[1]
    nch = rkvg.shape[2]
    wl = lp * LANES
    ngrp = nch // wl
    nt = tn_ // tb
    nchunks = tb // C
    rank = lora.shape[2]
    nh = s0.shape[1]
    row = lambda b, t: b * nt + t

    def xspec(c):
        return pl.BlockSpec((1, tb, wl), lambda b, p, t: (c, row(b, t), p))

    def lspec(c):
        return pl.BlockSpec((1, tb, rank), lambda b, p, t: (c, row(b, t), 0))

    og, sout = pl.pallas_call(
        functools.partial(_rwkv_kernel, C=C, tb=tb, lp=lp, nt=nt, unroll=unroll),
        grid=(bn, ngrp, nt),
        in_specs=[
            xspec(0), xspec(1), xspec(2), xspec(3), lspec(0), lspec(1),
            pl.BlockSpec((2, rank, wl), lambda b, p, t: (0, 0, p)),
            pl.BlockSpec((8, wl), lambda b, p, t: (0, p)),
            pl.BlockSpec((1, 2 * lp, HEAD, HEAD), lambda b, p, t: (b, p, 0, 0)),
        ],
        out_specs=[
            pl.BlockSpec((tb, wl), lambda b, p, t: (row(b, t), p)),
            pl.BlockSpec((1, 2 * lp, HEAD, HEAD), lambda b, p, t: (b, p, 0, 0)),
        ],
        out_shape=[
            jax.ShapeDtypeStruct((m, nch), og_dtype),
            jax.ShapeDtypeStruct((bn, nh, HEAD, HEAD), F32),
        ],
        scratch_shapes=[
            pltpu.VMEM((lp, LANES, LANES), F32),
            pltpu.VMEM((tb, wl), F32), pltpu.VMEM((tb, wl), F32), pltpu.VMEM((tb, wl), F32), pltpu.VMEM((tb, wl), F32),
            pltpu.VMEM((nchunks * lp, 2 * C, LANES), F32), pltpu.VMEM((nchunks * lp, 2 * C, LANES), F32),
            pltpu.VMEM((nchunks * lp, LANES, LANES), F32), pltpu.VMEM((nchunks * lp, LANES, LANES), F32),
            pltpu.VMEM((nchunks, 8, wl), F32),
            pltpu.VMEM((tb, wl), F32),
        ],
        compiler_params=_params(("parallel", "parallel", "arbitrary")),
        name="rwkv7_chunked",
    )(rkvg, rkvg, rkvg, rkvg, lora, lora, wup, vec, s0)
    return og, sout


def _rwkv_layer(x2d, bn, tn_, g, shift, s0, mu, w_rkvg, w_down, w_up, w0, a0, k_k, k_a, r_k, ln_g, ln_b,
                w_out, tm, C, tb, lp, unroll, og_dtype):
    m, d = x2d.shape
    nblk = m // tm
    if tm <= tn_:
        starts = jnp.arange(nblk) * tm
        prev = _rms_rows(x2d[jnp.maximum(starts - 1, 0)], g)
        first = jnp.where((starts % tn_ == 0)[:, None], shift[starts // tn_], prev).reshape(nblk, 1, d)
        period = None
    else:
        assert nblk == 1
        first = jnp.zeros((bn, tn_, d), F32).at[:, 0].set(shift).reshape(1, m, d)
        period = tn_
    shift_out = _rms_rows(x2d[tn_ - 1::tn_], g)
    rkvg = _proj(x2d, g, w_rkvg, 0, w_rkvg.shape[2], tm, 512, first=first, mu=mu[:4], period=period)
    rank = w_down.shape[2]
    lora = _proj(x2d, g, w_down, 0, rank, tm, rank, first=first, mu=mu[4:6], period=period)
    nch = w_rkvg.shape[2]
    vec = jnp.stack([w0, a0, k_k, k_a, r_k.reshape(nch), ln_g, ln_b, jnp.zeros((nch,), F32)])
    og, sout = _rwkv_mix(rkvg, lora, w_up, vec, s0, bn, tn_, C, tb, lp, unroll, og_dtype)
    y = _outproj(og, w_out, x2d, tm, 512)
    return y, sout, shift_out


GLA_SUB = 16


def _gla_kernel(q_ref, k_ref, v_ref, gate_ref, gl_ref, wup_ref, bgk_ref, og_w_ref, s0_ref,
                og_ref, sout_ref, st_ref, z_scr, qd_scr, o_scr, ec_scr, *, C, tb, nt):
    t = pl.program_id(2)
    dk = q_ref.shape[-1]
    nchunks = tb // C
    sub = min(GLA_SUB, C)
    nsub = C // sub
    qscale = dk ** -0.5

    @pl.when(t == 0)
    def _():
        st_ref[...] = s0_ref[0, 0].T

    tri = _tri(C)
    glog_all = -_softplus(-(_mm(gl_ref[...], wup_ref[...], NN, 3) + bgk_ref[...])) * (1.0 / C_GATE_NORM)
    chunks = range(nchunks)
    rows = [slice(c * C, (c + 1) * C) for c in chunks]
    glog = [glog_all[rs] for rs in rows]
    b = [_mm_exact_lhs(tri, gl_) for gl_ in glog]
    q = [q_ref[rs, :] * qscale for rs in rows]
    k = [k_ref[rs, :] for rs in rows]
    v = [v_ref[rs, :] for rs in rows]
    for c in chunks:
        bl = b[c][C - 1:C, :]
        qd_scr[rows[c], :] = q[c] * jnp.exp(b[c])
        ec_scr[c] = jnp.broadcast_to(jnp.exp(bl), (8, dk))
        z_scr[c] = _mm(v[c], k[c] * jnp.exp(bl - b[c]), TN)
    parts = [[] for _ in chunks]
    for i in range(nsub):
        r0 = i * sub
        nk = r0 + sub
        ar = lax.broadcasted_iota(jnp.int32, (sub, nk), 0) + r0
        ac = lax.broadcasted_iota(jnp.int32, (sub, nk), 1)
        att = []
        for c in chunks:
            ref = b[c][r0:r0 + 1, :] - glog[c][r0:r0 + 1, :]
            qi = q[c][r0:nk] * jnp.exp(b[c][r0:nk] - ref)
            ki = k[c][0:nk] * jnp.exp(ref - b[c][0:nk])
            att.append(jnp.where(ac <= ar, _mm(qi, ki, NT), 0.0))
        for c in chunks:
            parts[c].append(_mm(att[c], v[c][0:nk]))
    for c in chunks:
        o_scr[rows[c], :] = jnp.concatenate(parts[c], axis=0) if nsub > 1 else parts[c][0]

    for c in chunks:
        st = st_ref[...]
        o_scr[rows[c], :] = o_scr[rows[c], :] + _mm(qd_scr[rows[c], :], st, NT)
        st_ref[...] = st * ec_scr[c, 0:1, :] + z_scr[c]

    o = o_scr[...]
    on = o * lax.rsqrt(jnp.mean(o * o, axis=-1, keepdims=True) + NORM_EPS) * og_w_ref[...]
    og_ref[...] = (on * _silu(gate_ref[...])).astype(og_ref.dtype)

    @pl.when(t == nt - 1)
    def _():
        sout_ref[0, 0] = st_ref[...].T


def _gla_mix(qk, v, gate, gl, w_up, b_gk, o_g, s0, bn, tn_, C, tb, og_dtype):
    m = qk.shape[0]
    nh, dk, dv = s0.shape[1:]
    nt = tn_ // tb
    rank = gl.shape[1]
    row = lambda b, t: b * nt + t
    og, sout = pl.pallas_call(
        functools.partial(_gla_kernel, C=C, tb=tb, nt=nt),
        grid=(bn, nh, nt),
        in_specs=[
            pl.BlockSpec((tb, dk), lambda b, h, t: (row(b, t), h)),
            pl.BlockSpec((tb, dk), lambda b, h, t: (row(b, t), nh + h)),
            pl.BlockSpec((tb, dv), lambda b, h, t: (row(b, t), h)),
            pl.BlockSpec((tb, dv), lambda b, h, t: (row(b, t), h)),
            pl.BlockSpec((tb, rank), lambda b, h, t: (row(b, t), 0)),
            pl.BlockSpec((rank, dk), lambda b, h, t: (0, h)),
            pl.BlockSpec((1, dk), lambda b, h, t: (0, h)),
            pl.BlockSpec((1, dv), lambda b, h, t: (0, 0)),
            pl.BlockSpec((1, 1, dk, dv), lambda b, h, t: (b, h, 0, 0)),
        ],
        out_specs=[
            pl.BlockSpec((tb, dv), lambda b, h, t: (row(b, t), h)),
            pl.BlockSpec((1, 1, dk, dv), lambda b, h, t: (b, h, 0, 0)),
        ],
        out_shape=[
            jax.ShapeDtypeStruct((m, nh * dv), og_dtype),
            jax.ShapeDtypeStruct((bn, nh, dk, dv), F32),
        ],
        scratch_shapes=[
            pltpu.VMEM((dv, dk), F32),
            pltpu.VMEM((tb // C, dv, dk), F32),
            pltpu.VMEM((tb, dk), F32),
            pltpu.VMEM((tb, dv), F32),
            pltpu.VMEM((tb // C, 8, dk), F32),
        ],
        compiler_params=_params(("parallel", "parallel", "arbitrary")),
        name="gla_chunked",
    )(qk, qk, v, gate, gl, w_up, b_gk.reshape(1, -1), o_g.reshape(1, -1), s0)
    return og, sout


def _gla_layer(x2d, bn, tn_, g, s0, w_in, w_up, b_gk, o_g, w_out, tm, C, tb, og_dtype):
    d = x2d.shape[1]
    nh, dk, dv = s0.shape[1:]
    ckey = nh * dk
    dinner = nh * dv
    rank = w_up.shape[0]
    w3 = w_in.reshape(1, d, w_in.shape[1])
    qk = _proj(x2d, g, w3, 0, 2 * ckey, tm, 512)[0]
    v = _proj(x2d, g, w3, 2 * ckey, dinner, tm, 512)[0]
    gate = _proj(x2d, g, w3, 2 * ckey + dinner, dinner, tm, 512)[0]
    gl = _proj(x2d, g, w_in[:, 2 * ckey + 2 * dinner:].reshape(1, d, rank), 0, rank, tm, rank)[0]
    og, sout = _gla_mix(qk, v, gate, gl, w_up, b_gk, o_g, s0, bn, tn_, C, tb, og_dtype)
    y = _outproj(og, w_out, x2d, tm, 512)
    return y, sout


PROMPT_TM = 1024
RWKV_CHUNK = 64
RWKV_TB = 512
RWKV_PAIRS = 4
RWKV_UNROLL = 2
GLA_CHUNK = 64
GLA_TB = 512


def kernel(x_prompt, x_sample, cache_k_win, cache_v_win, state_wkv, state_shift, state_gla, norm_g, rel_bias, w_in_a, q_norm_g, k_norm_g, sinks, w_out_a, mu_b, w_rkvg_b, w_lora_down_b, w_lora_up_b, w0_b, a0_b, k_k_b, k_a_b, r_k_b, ln_x_g_b, ln_x_b_b, w_out_b, w_in_c, w_gk_up_c, b_gk_c, o_norm_g_c, w_out_c):
    bp, tp, d = x_prompt.shape
    bs, ts, _ = x_sample.shape
    depth = norm_g.shape[0]
    xp = x_prompt.reshape(bp * tp, d)
    xs = x_sample.reshape(bs * ts, d)
    ms = bs * ts
    kwp, vwp, kws, vws, wkvp, shp, wkvs, shs, glap, glas = ([] for _ in range(10))
    for layer in range(depth):
        kind, j = layer % 3, layer // 3
        g = norm_g[layer]
        if kind == 0:
            wa = (w_in_a[j], q_norm_g[j], k_norm_g[j], sinks[j], w_out_a[j], rel_bias)
            xp, kp_, vp_ = _attn_layer(xp, bp, tp, g, *wa, None, None, PROMPT_TM, BF16)
            xs, ks_, vs_ = _attn_layer(xs, bs, ts, g, *wa, cache_k_win[j], cache_v_win[j], ms, F32)
            kwp.append(kp_); vwp.append(vp_); kws.append(ks_); vws.append(vs_)
        elif kind == 1:
            wb = (mu_b[j], w_rkvg_b[j], w_lora_down_b[j], w_lora_up_b[j], w0_b[j], a0_b[j], k_k_b[j], k_a_b[j],
                  r_k_b[j], ln_x_g_b[j], ln_x_b_b[j], w_out_b[j])
            nh = w_rkvg_b.shape[3] // HEAD
            xp, sp_, lp_ = _rwkv_layer(xp, bp, tp, g, jnp.zeros((bp, d), F32), jnp.zeros((bp, nh, HEAD, HEAD), F32),
                                       *wb, PROMPT_TM, RWKV_CHUNK, RWKV_TB, RWKV_PAIRS, RWKV_UNROLL, BF16)
            xs, ss_, ls_ = _rwkv_layer(xs, bs, ts, g, state_shift[j], state_wkv[j], *wb, ms, ts, ts, RWKV_PAIRS, 1, F32)
            wkvp.append(sp_); shp.append(lp_); wkvs.append(ss_); shs.append(ls_)
        else:
            wc = (w_in_c[j], w_gk_up_c[j], b_gk_c[j], o_norm_g_c[j], w_out_c[j])
            xp, sp_ = _gla_layer(xp, bp, tp, g, jnp.zeros((bp,) + state_gla.shape[2:], F32), *wc,
                                 PROMPT_TM, GLA_CHUNK, GLA_TB, BF16)
            xs, ss_ = _gla_layer(xs, bs, ts, g, state_gla[j], *wc, ms, ts, ts, F32)
            glap.append(sp_); glas.append(ss_)
    return (xp.reshape(bp, tp, d), xs.reshape(bs, ts, d),
            jnp.stack(kwp), jnp.stack(vwp), jnp.stack(kws), jnp.stack(vws),
            jnp.stack(wkvp), jnp.stack(shp), jnp.stack(wkvs), jnp.stack(shs),
            jnp.stack(glap), jnp.stack(glas))
```

```python
import functools
import math

import jax
import jax.numpy as jnp
from jax import lax
from jax.experimental import pallas as pl
from jax.experimental.pallas import tpu as pltpu

F32 = jnp.float32
BF16 = jnp.bfloat16

NORM_EPS = 1e-6
HEAD = 64
LANES = 128
WINDOW = 128
N_BUCKETS = 32
MAX_EXACT = N_BUCKETS // 2
MAX_DISTANCE = 128
B_LN_EPS = 64e-5
C_GATE_NORM = 16.0
VMEM_LIMIT = 56 * 1024 * 1024

NN = (((1,), (0,)), ((), ()))
NT = (((1,), (1,)), ((), ()))
TN = (((0,), (0,)), ((), ()))


def _params(sem):
    return pltpu.CompilerParams(dimension_semantics=sem, vmem_limit_bytes=VMEM_LIMIT)


def _split(x):
    hi = x.astype(BF16)
    lo = (x - hi.astype(F32)).astype(BF16)
    return hi, lo


def _mm(a, b, dims=NN, passes=1):
    if passes == 1:
        return lax.dot_general(a.astype(BF16), b.astype(BF16), dims, preferred_element_type=F32)
    a_hi, a_lo = _split(a)
    b_hi, b_lo = _split(b)
    dg = functools.partial(lax.dot_general, dimension_numbers=dims, preferred_element_type=F32)
    return dg(a_hi, b_hi) + (dg(a_hi, b_lo) + dg(a_lo, b_hi))


def _mm_exact_lhs(a_bf16, b):
    b0 = b.astype(BF16)
    r1 = b - b0.astype(F32)
    b1 = r1.astype(BF16)
    b2 = (r1 - b1.astype(F32)).astype(BF16)
    dg = functools.partial(lax.dot_general, dimension_numbers=NN, preferred_element_type=F32)
    return dg(a_bf16, b0) + (dg(a_bf16, b1) + dg(a_bf16, b2))


def _softplus(z):
    return jnp.maximum(z, 0.0) + jnp.log1p(jnp.exp(-jnp.abs(z)))


def _silu(g):
    return g * jax.nn.sigmoid(g)


def _tri(c):
    r = lax.broadcasted_iota(jnp.int32, (c, c), 0)
    col = lax.broadcasted_iota(jnp.int32, (c, c), 1)
    return (col <= r).astype(BF16)


def _rms_rows_kernel(x_ref, g_ref, o_ref):
    x = x_ref[...]
    o_ref[...] = x * lax.rsqrt(jnp.mean(x * x, axis=-1, keepdims=True) + NORM_EPS) * g_ref[...]


def _rms_rows(rows, g):
    n, d = rows.shape
    npad = -(-n // 8) * 8
    rows_p = jnp.pad(rows, ((0, npad - n), (0, 0)))
    out = pl.pallas_call(
        _rms_rows_kernel,
        out_shape=jax.ShapeDtypeStruct((npad, d), F32),
        name="rms_rows",
    )(rows_p, g.reshape(1, d))
    return out[:n]


PROLOGUE_ROWS = 256


def _proj_kernel(*refs, lerp, period, tm):
    if lerp:
        x_ref, g_ref, w_ref, first_ref, mu_ref, o_ref, xm_ref = refs
    else:
        x_ref, g_ref, w_ref, o_ref, xm_ref = refs
    j = pl.program_id(2)

    @pl.when(j == 0)
    def _():
        g = g_ref[...]
        rc = min(PROLOGUE_ROWS, tm)
        carry = first_ref[0, 0:1, :] if lerp else None
        for c0 in range(0, tm, rc):
            x = x_ref[c0:c0 + rc, :]
            h = x * lax.rsqrt(jnp.mean(x * x, axis=-1, keepdims=True) + NORM_EPS) * g
            if lerp:
                row = lax.broadcasted_iota(jnp.int32, h.shape, 0)
                hs = jnp.where(row == 0, carry, pltpu.roll(h, 1, 0))
                if period is not None:
                    hs = jnp.where(row % period == 0, first_ref[0, c0:c0 + rc, :], hs)
                carry = h[rc - 1:rc, :]
                h = h + (hs - h) * mu_ref[0]
            xm_ref[c0:c0 + rc, :] = h.astype(BF16)

    o_ref[0] = jnp.dot(xm_ref[...], w_ref[0], preferred_element_type=F32).astype(o_ref.dtype)


def _proj(x2d, g, w3, col_off, n_out, tm, tn, out_dtype, first=None, mu=None, period=None):
    m, d = x2d.shape
    nc = w3.shape[0]
    lerp = first is not None
    assert m % tm == 0 and n_out % tn == 0 and col_off % tn == 0
    joff = col_off // tn
    in_specs = [
        pl.BlockSpec((tm, d), lambda i, c, j: (i, 0)),
        pl.BlockSpec((1, d), lambda i, c, j: (0, 0)),
        pl.BlockSpec((1, d, tn), lambda i, c, j: (c, 0, joff + j)),
    ]
    args = [x2d, g.reshape(1, d), w3]
    if lerp:
        fr = first.shape[1]
        in_specs += [
            pl.BlockSpec((1, fr, d), lambda i, c, j: (i, 0, 0)),
            pl.BlockSpec((1, 1, d), lambda i, c, j: (c, 0, 0)),
        ]
        args += [first, mu.reshape(nc, 1, d)]
    return pl.pallas_call(
        functools.partial(_proj_kernel, lerp=lerp, period=period, tm=tm),
        grid=(m // tm, nc, n_out // tn),
        in_specs=in_specs,
        out_specs=pl.BlockSpec((1, tm, tn), lambda i, c, j: (c, i, j)),
        out_shape=jax.ShapeDtypeStruct((nc, m, n_out), out_dtype),
        scratch_shapes=[pltpu.VMEM((tm, d), BF16)],
        compiler_params=_params(("parallel", "arbitrary", "arbitrary")),
        name="proj",
    )(*args)


def _outproj_kernel(a_ref, w_ref, x_ref, o_ref):
    o_ref[...] = x_ref[...] + jnp.dot(a_ref[...].astype(BF16), w_ref[...], preferred_element_type=F32)


def _outproj(a2d, w, x2d, tm, tn):
    m, kdim = a2d.shape
    n = w.shape[1]
    return pl.pallas_call(
        _outproj_kernel,
        grid=(m // tm, n // tn),
        in_specs=[
            pl.BlockSpec((tm, kdim), lambda i, j: (i, 0)),
            pl.BlockSpec((kdim, tn), lambda i, j: (0, j)),
            pl.BlockSpec((tm, tn), lambda i, j: (i, j)),
        ],
        out_specs=pl.BlockSpec((tm, tn), lambda i, j: (i, j)),
        out_shape=jax.ShapeDtypeStruct((m, n), F32),
        compiler_params=_params(("parallel", "arbitrary")),
        name="outproj",
    )(a2d, w, x2d)


A_KV_HEADS = 8
A_GROUP = 8


def _attn_kernel(q_ref, kc_ref, vc_ref, kp_ref, vp_ref, gate_ref, bias_ref, qg_ref, kg_ref,
                 og_ref, kwin_ref, vwin_ref, k_scr, v_scr, s_scr, p_scr, *, tq, prompt, nblocks):
    i = pl.program_id(1)
    tk = WINDOW + tq
    kg = kg_ref[...]
    qg = qg_ref[...] * (HEAD ** -0.5)
    nslab = kg.shape[1]
    jr = lax.broadcasted_iota(jnp.int32, (nslab, nslab), 0)
    jc = lax.broadcasted_iota(jnp.int32, (nslab, nslab), 1)
    mean_bd = jnp.where((jr // HEAD) == (jc // HEAD), 1.0 / HEAD, 0.0).astype(BF16)
    ones_v = jnp.ones((tk, HEAD), BF16)

    def slab_norm(x, gain):
        ms = jnp.dot((x * x).astype(BF16), mean_bd, preferred_element_type=F32)
        return x * lax.rsqrt(ms + NORM_EPS) * gain

    kprev = kp_ref[...].astype(F32)
    k_scr[0:WINDOW, :] = slab_norm(kprev, kg) if prompt else kprev
    k_scr[WINDOW:tk, :] = slab_norm(kc_ref[...].astype(F32), kg)
    v_scr[0:WINDOW, :] = vp_ref[...].astype(F32)
    v_scr[WINDOW:tk, :] = vc_ref[...].astype(F32)
    k_scr[0:1, :] = jnp.zeros((1, k_scr.shape[1]), F32)
    v_scr[0:1, :] = jnp.zeros((1, v_scr.shape[1]), F32)

    c = lax.broadcasted_iota(jnp.int32, (tq, tk), 1)
    no_prev = (c >= WINDOW) | (c == 0)
    lo_half = lax.broadcasted_iota(jnp.int32, (tq, LANES), 1) < HEAD

    def group(kh, first_block):
        kk = k_scr[:, kh * HEAD:(kh + 1) * HEAD].astype(BF16)
        vv = v_scr[:, kh * HEAD:(kh + 1) * HEAD].astype(BF16)
        h0 = kh * A_GROUP
        lanes = slice(h0 * HEAD, (h0 + A_GROUP) * HEAD)
        qn = slab_norm(q_ref[:, lanes].astype(F32), qg)
        qs = jnp.concatenate([qn[:, e * HEAD:(e + 1) * HEAD] for e in range(A_GROUP)], axis=0)
        s_scr[...] = lax.dot_general(qs.astype(BF16), kk, NT, preferred_element_type=F32)
        for e in range(A_GROUP):
            rs = slice(e * tq, (e + 1) * tq)
            s = s_scr[rs, :] + bias_ref[h0 + e]
            if first_block:
                s = jnp.where(no_prev, s, -jnp.inf)
            p_scr[rs, :] = jnp.exp(s - jnp.max(s, axis=-1, keepdims=True)).astype(p_scr.dtype)
        x = jnp.dot(p_scr[...].astype(BF16), jnp.concatenate([vv, ones_v, ones_v, vv], axis=-1),
                    preferred_element_type=F32)
        pairs = []
        for e in range(0, A_GROUP, 2):
            xe, xo = x[e * tq:(e + 1) * tq], x[(e + 1) * tq:(e + 2) * tq]
            num = jnp.where(lo_half, xe[:, 0:LANES], xo[:, LANES:2 * LANES])
            den = jnp.where(lo_half, xe[:, LANES:2 * LANES], xo[:, 0:LANES])
            pairs.append(num / den)
        o = jnp.concatenate(pairs, axis=-1)
        og_ref[:, lanes] = (o * _silu(gate_ref[:, lanes].astype(F32))).astype(og_ref.dtype)

    if prompt:
        @pl.when(i == 0)
        def _():
            for kh in range(A_KV_HEADS):
                group(kh, True)

        @pl.when(i > 0)
        def _():
            for kh in range(A_KV_HEADS):
                group(kh, False)
    else:
        for kh in range(A_KV_HEADS):
            group(kh, False)

    @pl.when(i == nblocks - 1)
    def _():
        kwin_ref[0] = k_scr[tk - WINDOW:tk, :]
        vwin_ref[0] = v_scr[tk - WINDOW:tk, :]


def _t5_bucket(dist):
    d = jnp.maximum(dist, 0)
    large = MAX_EXACT + (jnp.log(jnp.maximum(d, 1).astype(F32) / MAX_EXACT)
                         / math.log(MAX_DISTANCE / MAX_EXACT) * (N_BUCKETS - MAX_EXACT)).astype(jnp.int32)
    return jnp.where(d < MAX_EXACT, d, jnp.minimum(large, N_BUCKETS - 1))


def _attn_mix(q, kv, gate, rel_bias, q_g, k_g, sinks, bn, tn_, cache_k, cache_v, og_dtype):
    m = q.shape[0]
    prompt = cache_k is None
    tq = WINDOW if prompt else tn_
    nb = tn_ // tq
    tk = WINDOW + tq
    col = jnp.arange(tk)[None, :]
    dist = WINDOW + jnp.arange(tq)[:, None] - col
    onehot = (_t5_bucket(dist)[None] == jnp.arange(N_BUCKETS)[:, None, None]).astype(F32)
    bias = jnp.einsum("bh,bqk->hqk", rel_bias.astype(F32), onehot, precision=lax.Precision.HIGHEST)
    bias = jnp.where(((dist >= 0) & (dist < WINDOW))[None], bias, -jnp.inf)
    bias = jnp.where((col == 0)[None], sinks.astype(F32)[:, None, None], bias)
    nkv = A_KV_HEADS * HEAD
    row = lambda b, i: b * nb + i
    if prompt:
        kp_arr, vp_arr = kv, kv
        kp_spec = pl.BlockSpec((WINDOW, nkv), lambda b, i: (jnp.maximum(row(b, i) - 1, 0), 0))
        vp_spec = pl.BlockSpec((WINDOW, nkv), lambda b, i: (jnp.maximum(row(b, i) - 1, 0), 1))
    else:
        kp_arr = cache_k.reshape(bn * WINDOW, nkv)
        vp_arr = cache_v.reshape(bn * WINDOW, nkv)
        kp_spec = pl.BlockSpec((WINDOW, nkv), lambda b, i: (b, 0))
        vp_spec = pl.BlockSpec((WINDOW, nkv), lambda b, i: (b, 0))
    nq = q.shape[1]
    og, kwin, vwin = pl.pallas_call(
        functools.partial(_attn_kernel, tq=tq, prompt=prompt, nblocks=nb),
        grid=(bn, nb),
        in_specs=[
            pl.BlockSpec((tq, nq), lambda b, i: (row(b, i), 0)),
            pl.BlockSpec((tq, nkv), lambda b, i: (row(b, i), 0)),
            pl.BlockSpec((tq, nkv), lambda b, i: (row(b, i), 1)),
            kp_spec, vp_spec,
            pl.BlockSpec((tq, nq), lambda b, i: (row(b, i), 0)),
            pl.BlockSpec((nq // HEAD, tq, tk), lambda b, i: (0, 0, 0)),
            pl.BlockSpec((1, nkv), lambda b, i: (0, 0)),
            pl.BlockSpec((1, nkv), lambda b, i: (0, 0)),
        ],
        out_specs=[
            pl.BlockSpec((tq, nq), lambda b, i: (row(b, i), 0)),
            pl.BlockSpec((1, WINDOW, nkv), lambda b, i: (b, 0, 0)),
            pl.BlockSpec((1, WINDOW, nkv), lambda b, i: (b, 0, 0)),
        ],
        out_shape=[
            jax.ShapeDtypeStruct((m, nq), og_dtype),
            jax.ShapeDtypeStruct((bn, WINDOW, nkv), F32),
            jax.ShapeDtypeStruct((bn, WINDOW, nkv), F32),
        ],
        scratch_shapes=[
            pltpu.VMEM((tk, nkv), F32), pltpu.VMEM((tk, nkv), F32),
            pltpu.VMEM((A_GROUP * tq, tk), F32),
            pltpu.VMEM((A_GROUP * tq, tk), BF16 if tq % 16 == 0 else F32),
        ],
        compiler_params=_params(("parallel", "arbitrary")),
        name="swa_attention",
    )(q, kv, kv, kp_arr, vp_arr, gate, bias,
      jnp.tile(q_g.astype(F32), A_GROUP).reshape(1, nkv), jnp.tile(k_g.astype(F32), A_KV_HEADS).reshape(1, nkv))
    return og, kwin, vwin


def _attn_layer(x2d, bn, tn_, g, w_in, q_g, k_g, sinks, w_out, rel_bias, cache_k, cache_v, tm, og_dtype):
    d = x2d.shape[1]
    w3 = w_in.reshape(1, d, w_in.shape[1])
    nq = w_out.shape[0]
    nkv = A_KV_HEADS * HEAD
    q = _proj(x2d, g, w3, 0, nq, tm, 512, og_dtype)[0]
    kv = _proj(x2d, g, w3, nq, 2 * nkv, tm, 512, og_dtype)[0]
    gate = _proj(x2d, g, w3, nq + 2 * nkv, nq, tm, 512, og_dtype)[0]
    og, kwin, vwin = _attn_mix(q, kv, gate, rel_bias, q_g, k_g, sinks, bn, tn_, cache_k, cache_v, og_dtype)
    y = _outproj(og, w_out, x2d, tm, 512)
    shape = (bn, WINDOW, A_KV_HEADS, HEAD)
    return y, kwin.reshape(shape), vwin.reshape(shape)


def _rwkv_kernel(r_ref, k_ref, v_ref, g_ref, lw_ref, la_ref, wup_ref, vec_ref, s0_ref,
                 og_ref, sout_ref,
                 s_ref, kk_scr, k2_scr, a_scr, ld_scr, q_scr, o0_scr, gp_scr, z_scr, ec_scr, o_scr,
                 *, C, tb, lp, nt, unroll):
    t = pl.program_id(2)
    n = 2 * C
    nchunks = tb // C
    lane = lax.broadcasted_iota(jnp.int32, (1, LANES), 1)
    lo = (lane < HEAD).astype(F32)
    hi = 1.0 - lo

    @pl.when(t == 0)
    def _():
        z = jnp.zeros((HEAD, HEAD), F32)
        for pp in range(lp):
            s_ref[pp, 0:HEAD, :] = jnp.concatenate([s0_ref[0, 2 * pp], z], axis=-1)
            s_ref[pp, HEAD:LANES, :] = jnp.concatenate([z, s0_ref[0, 2 * pp + 1]], axis=-1)

    vec = vec_ref[...]
    w0, a0, kk_w, ka_w, rk_w, ln_g, ln_b = (vec[i:i + 1, :] for i in range(7))

    jr = lax.broadcasted_iota(jnp.int32, (LANES, LANES), 0)
    jc = lax.broadcasted_iota(jnp.int32, (LANES, LANES), 1)
    ones_bd = ((jr // HEAD) == (jc // HEAD)).astype(F32)
    mean_bd = ones_bd * (1.0 / HEAD)

    def per_pair(fn, *xs):
        return jnp.concatenate([fn(*(x[:, pp * LANES:(pp + 1) * LANES] for x in xs)) for pp in range(lp)], axis=-1)

    k = k_ref[0].astype(F32)
    xw = w0 + _mm(jnp.tanh(lw_ref[0]), wup_ref[0], NN, 3)
    ld_scr[...] = -jnp.exp(-_softplus(-xw) - 0.5)
    a = jax.nn.sigmoid(a0 + _mm(la_ref[0], wup_ref[1], NN, 3))
    a_scr[...] = a
    kkr = k * kk_w
    ssq = per_pair(lambda x: _mm(x * x, ones_bd, NN, STAT_PASSES), kkr)
    kk_scr[...] = kkr / jnp.maximum(jnp.sqrt(ssq), 1e-12)
    k2_scr[...] = k * (1.0 + (a - 1.0) * ka_w)

    rr = lax.broadcasted_iota(jnp.int32, (n, n), 0)
    cc = lax.broadcasted_iota(jnp.int32, (n, n), 1)
    strict = cc < rr
    incl = cc <= rr
    eye = (cc == rr).astype(F32)
    tri = _tri(C)
    nsteps = int(math.log2(C)) - 1

    def stage_b(gi, carry):
        units = []
        for cu in range(unroll):
            ci = gi * unroll + cu
            rows = pl.ds(pl.multiple_of(ci * C, C), C)
            ld = ld_scr[rows, :]
            b = _mm_exact_lhs(tri, ld)
            eb = jnp.exp(b)
            enb = jnp.exp(-b)
            kk = kk_scr[rows, :]
            al = -kk * jnp.exp(b - ld)
            be = kk * a_scr[rows, :] * enb
            kt = k2_scr[rows, :] * enb
            rb = r_ref[0, rows, :].astype(F32) * eb
            v = v_ref[0, rows, :].astype(F32)
            e_c = eb[C - 1:C, :]
            ec_scr[ci] = jnp.broadcast_to(e_c, (8, e_c.shape[1]))
            for pp in range(lp):
                ls = slice(pp * LANES, (pp + 1) * LANES)
                cat = lambda x: jnp.concatenate([x[:, ls] * lo, x[:, ls] * hi], axis=0)
                units.append(dict(idx=ci * lp + pp, ecp=e_c[:, ls], la=cat(al), lr=cat(rb), rb=cat(be),
                                  rk=cat(kt), vb=cat(v)))
        nu = range(len(units))
        if n % LANES == 0:
            prod = [_mm(jnp.concatenate([u["la"], u["lr"]], axis=0),
                        jnp.concatenate([u["rb"], u["rk"]], axis=0), NT) for u in units]
            m_ab = [jnp.where(strict, x[0:n, 0:n], 0.0) for x in prod]
            m_ak = [jnp.where(strict, x[0:n, n:2 * n], 0.0) for x in prod]
            m_rb = [jnp.where(incl, x[n:2 * n, 0:n], 0.0) for x in prod]
            m_rk = [jnp.where(incl, x[n:2 * n, n:2 * n], 0.0) for x in prod]
            kv = [_mm(jnp.concatenate([m_ak[i], m_rk[i]], axis=0), units[i]["vb"]) for i in nu]
            akv = [x[0:n] for x in kv]
            rkv = [x[n:2 * n] for x in kv]
            tinv = [eye + a_ for a_ in m_ab]
            apow = [_mm(a_, a_) for a_ in m_ab]
            for step in range(nsteps):
                if step < nsteps - 1:
                    both = [_mm(apow[i], jnp.concatenate([tinv[i], apow[i]], axis=1)) for i in nu]
                    tinv = [tinv[i] + both[i][:, 0:n] for i in nu]
                    apow = [x[:, n:2 * n] for x in both]
                else:
                    tinv = [tinv[i] + _mm(apow[i], tinv[i]) for i in nu]
            wu = [_mm(tinv[i], jnp.concatenate([units[i]["la"], akv[i]], axis=1)) for i in nu]
            qo = [_mm(m_rb[i], wu[i]) for i in nu]
            rbe = [u["rb"] * u["ecp"] for u in units]
            gz = [_mm(wu[i], rbe[i], TN) for i in nu]
            for i, u in enumerate(units):
                q_scr[u["idx"]] = u["lr"] + qo[i][:, 0:LANES]
                o0_scr[u["idx"]] = qo[i][:, LANES:2 * LANES] + rkv[i]
                gp_scr[u["idx"]] = gz[i][0:LANES]
            for i, u in enumerate(units):
                z_scr[u["idx"]] = gz[i][LANES:2 * LANES] + _mm(u["vb"], u["rk"] * u["ecp"], TN)
            return carry
        each = lambda fn: [fn(u) for u in units]
        m_ab = each(lambda u: jnp.where(strict, _mm(u["la"], u["rb"], NT), 0.0))
        m_ak = each(lambda u: jnp.where(strict, _mm(u["la"], u["rk"], NT), 0.0))
        m_rb = each(lambda u: jnp.where(incl, _mm(u["lr"], u["rb"], NT), 0.0))
        m_rk = each(lambda u: jnp.where(incl, _mm(u["lr"], u["rk"], NT), 0.0))
        akv = [_mm(m, u["vb"]) for m, u in zip(m_ak, units)]
        rkv = [_mm(m, u["vb"]) for m, u in zip(m_rk, units)]
        apow = m_ab
        tinv = [eye + a_ for a_ in apow]
        for _ in range(nsteps):
            apow = [_mm(a_, a_) for a_ in apow]
            tinv = [t_ + _mm(a_, t_) for a_, t_ in zip(apow, tinv)]
        w = [_mm(t_, u["la"]) for t_, u in zip(tinv, units)]
        u0 = [_mm(t_, x_) for t_, x_ in zip(tinv, akv)]
        rbe = [u["rb"] * u["ecp"] for u in units]
        for i, u in enumerate(units):
            q_scr[u["idx"]] = u["lr"] + _mm(m_rb[i], w[i])
        for i, u in enumerate(units):
            o0_scr[u["idx"]] = _mm(m_rb[i], u0[i]) + rkv[i]
        for i, u in enumerate(units):
            gp_scr[u["idx"]] = _mm(w[i], rbe[i], TN)
        for i, u in enumerate(units):
            z_scr[u["idx"]] = _mm(u0[i], rbe[i], TN) + _mm(u["vb"], u["rk"] * u["ecp"], TN)
        return carry

    lax.fori_loop(0, nchunks // unroll, stage_b, 0)

    for ci in range(nchunks):
        for pp in range(lp):
            ls = slice(pp * LANES, (pp + 1) * LANES)
            idx = ci * lp + pp
            s = s_ref[pp]
            o_bd = _mm(q_scr[idx], s, NT) + o0_scr[idx]
            o_scr[ci * C:(ci + 1) * C, ls] = o_bd[0:C] + o_bd[C:n]
            s_ref[pp] = s * ec_scr[ci, 0:1, ls] + _mm(s, gp_scr[idx]) + z_scr[idx]

    o = o_scr[...]
    v = v_ref[0].astype(F32)
    mean = per_pair(lambda x: _mm(x, mean_bd, NN, STAT_PASSES), o)
    dlt = o - mean
    var = per_pair(lambda x: _mm(x, mean_bd, NN, STAT_PASSES), dlt * dlt)
    on = dlt * lax.rsqrt(var + B_LN_EPS) * ln_g + ln_b
    bonus = per_pair(lambda x: _mm(x, ones_bd, NN, STAT_PASSES), r_ref[0].astype(F32) * k2_scr[...] * rk_w)
    on = on + bonus * v
    og_ref[...] = (on * _silu(g_ref[0].astype(F32))).astype(og_ref.dtype)

    @pl.when(t == nt - 1)
    def _():
        for pp in range(lp):
            sout_ref[0, 2 * pp] = s_ref[pp, 0:HEAD, 0:HEAD]
            sout_ref[0, 2 * pp + 1] = s_ref[pp, HEAD:LANES, HEAD:LANES]


def _rwkv_mix(rkvg, lora, wup, vec, s0, bn, tn_, C, tb, lp, unroll, og_dtype):
    m = rkvg.shape[1]
    nch = rkvg.shape[2]
    wl = lp * LANES
    ngrp = nch // wl
    nt = tn_ // tb
    nchunks = tb // C
    rank = lora.shape[2]
    nh = s0.shape[1]
    row = lambda b, t: b * nt + t

    def xspec(c):
        return pl.BlockSpec((1, tb, wl), lambda b, p, t: (c, row(b, t), p))

    def lspec(c):
        return pl.BlockSpec((1, tb, rank), lambda b, p, t: (c, row(b, t), 0))

    og, sout = pl.pallas_call(
        functools.partial(_rwkv_kernel, C=C, tb=tb, lp=lp, nt=nt, unroll=unroll),
        grid=(bn, ngrp, nt),
        in_specs=[
            xspec(0), xspec(1), xspec(2), xspec(3), lspec(0), lspec(1),
            pl.BlockSpec((2, rank, wl), lambda b, p, t: (0, 0, p)),
            pl.BlockSpec((8, wl), lambda b, p, t: (0, p)),
            pl.BlockSpec((1, 2 * lp, HEAD, HEAD), lambda b, p, t: (b, p, 0, 0)),
        ],
        out_specs=[
            pl.BlockSpec((tb, wl), lambda b, p, t: (row(b, t), p)),
            pl.BlockSpec((1, 2 * lp, HEAD, HEAD), lambda b, p, t: (b, p, 0, 0)),
        ],
        out_shape=[
            jax.ShapeDtypeStruct((m, nch), og_dtype),
            jax.ShapeDtypeStruct((bn, nh, HEAD, HEAD), F32),
        ],
        scratch_shapes=[
            pltpu.VMEM((lp, LANES, LANES), F32),
            pltpu.VMEM((tb, wl), F32), pltpu.VMEM((tb, wl), F32), pltpu.VMEM((tb, wl), F32), pltpu.VMEM((tb, wl), F32),
            pltpu.VMEM((nchunks * lp, 2 * C, LANES), F32), pltpu.VMEM((nchunks * lp, 2 * C, LANES), F32),
            pltpu.VMEM((nchunks * lp, LANES, LANES), F32), pltpu.VMEM((nchunks * lp, LANES, LANES), F32),
            pltpu.VMEM((nchunks, 8, wl), F32),
            pltpu.VMEM((tb, wl), F32),
        ],
        compiler_params=_params(("parallel", "parallel", "arbitrary")),
        name="rwkv7_chunked",
    )(rkvg, rkvg, rkvg, rkvg, lora, lora, wup, vec, s0)
    return og, sout


def _rwkv_layer(x2d, bn, tn_, g, shift, s0, mu, w_rkvg, w_down, w_up, w0, a0, k_k, k_a, r_k, ln_g, ln_b,
                w_out, tm, C, tb, lp, unroll, og_dtype):
    m, d = x2d.shape
    nblk = m // tm
    if tm <= tn_:
        starts = jnp.arange(nblk) * tm
        prev = _rms_rows(x2d[jnp.maximum(starts - 1, 0)], g)
        first = jnp.where((starts % tn_ == 0)[:, None], shift[starts // tn_], prev).reshape(nblk, 1, d)
        period = None
    else:
        assert nblk == 1
        first = jnp.zeros((bn, tn_, d), F32).at[:, 0].set(shift).reshape(1, m, d)
        period = tn_
    shift_out = _rms_rows(x2d[tn_ - 1::tn_], g)
    rkvg = _proj(x2d, g, w_rkvg, 0, w_rkvg.shape[2], tm, 512, og_dtype, first=first, mu=mu[:4], period=period)
    rank = w_down.shape[2]
    lora = _proj(x2d, g, w_down, 0, rank, tm, rank, F32, first=first, mu=mu[4:6], period=period)
    nch = w_rkvg.shape[2]
    vec = jnp.stack([w0, a0, k_k, k_a, r_k.reshape(nch), ln_g, ln_b, jnp.zeros((nch,), F32)])
    og, sout = _rwkv_mix(rkvg, lora, w_up, vec, s0, bn, tn_, C, tb, lp, unroll, og_dtype)
    y = _outproj(og, w_out, x2d, tm, 512)
    return y, sout, shift_out


GLA_SUB = 16


def _gla_kernel(q_ref, k_ref, v_ref, gate_ref, gl_ref, wup_ref, bgk_ref, og_w_ref, s0_ref,
                og_ref, sout_ref, st_ref, z_scr, qd_scr, o_scr, ec_scr, *, C, tb, nt):
    t = pl.program_id(2)
    dk = q_ref.shape[-1]
    nchunks = tb // C
    sub = min(GLA_SUB, C)
    nsub = C // sub
    qscale = dk ** -0.5

    @pl.when(t == 0)
    def _():
        st_ref[...] = s0_ref[0, 0].T

    tri = _tri(C)
    glog_all = -_softplus(-(_mm(gl_ref[...], wup_ref[...], NN, 3) + bgk_ref[...])) * (1.0 / C_GATE_NORM)
    chunks = range(nchunks)
    rows = [slice(c * C, (c + 1) * C) for c in chunks]
    glog = [glog_all[rs] for rs in rows]
    b = [_mm_exact_lhs(tri, gl_) for gl_ in glog]
    q = [q_ref[rs, :].astype(F32) * qscale for rs in rows]
    k = [k_ref[rs, :].astype(F32) for rs in rows]
    v = [v_ref[rs, :].astype(F32) for rs in rows]
    for c in chunks:
        bl = b[c][C - 1:C, :]
        qd_scr[rows[c], :] = q[c] * jnp.exp(b[c])
        ec_scr[c] = jnp.broadcast_to(jnp.exp(bl), (8, dk))
        z_scr[c] = _mm(v[c], k[c] * jnp.exp(bl - b[c]), TN)
    parts = [[] for _ in chunks]
    for i in range(nsub):
        r0 = i * sub
        nk = r0 + sub
        ar = lax.broadcasted_iota(jnp.int32, (sub, nk), 0) + r0
        ac = lax.broadcasted_iota(jnp.int32, (sub, nk), 1)
        att = []
        for c in chunks:
            ref = b[c][r0:r0 + 1, :] - glog[c][r0:r0 + 1, :]
            qi = q[c][r0:nk] * jnp.exp(b[c][r0:nk] - ref)
            ki = k[c][0:nk] * jnp.exp(ref - b[c][0:nk])
            att.append(jnp.where(ac <= ar, _mm(qi, ki, NT), 0.0))
        for c in chunks:
            parts[c].append(_mm(att[c], v[c][0:nk]))
    for c in chunks:
        o_scr[rows[c], :] = jnp.concatenate(parts[c], axis=0) if nsub > 1 else parts[c][0]

    for c in chunks:
        st = st_ref[...]
        o_scr[rows[c], :] = o_scr[rows[c], :] + _mm(qd_scr[rows[c], :], st, NT)
        st_ref[...] = st * ec_scr[c, 0:1, :] + z_scr[c]

    o = o_scr[...]
    on = o * lax.rsqrt(jnp.mean(o * o, axis=-1, keepdims=True) + NORM_EPS) * og_w_ref[...]
    og_ref[...] = (on * _silu(gate_ref[...].astype(F32))).astype(og_ref.dtype)

    @pl.when(t == nt - 1)
    def _():
        sout_ref[0, 0] = st_ref[...].T


def _gla_mix(qk, v, gate, gl, w_up, b_gk, o_g, s0, bn, tn_, C, tb, og_dtype):
    m = qk.shape[0]
    nh, dk, dv = s0.shape[1:]
    nt = tn_ // tb
    rank = gl.shape[1]
    row = lambda b, t: b * nt + t
    og, sout = pl.pallas_call(
        functools.partial(_gla_kernel, C=C, tb=tb, nt=nt),
        grid=(bn, nh, nt),
        in_specs=[
            pl.BlockSpec((tb, dk), lambda b, h, t: (row(b, t), h)),
            pl.BlockSpec((tb, dk), lambda b, h, t: (row(b, t), nh + h)),
            pl.BlockSpec((tb, dv), lambda b, h, t: (row(b, t), h)),
            pl.BlockSpec((tb, dv), lambda b, h, t: (row(b, t), h)),
            pl.BlockSpec((tb, rank), lambda b, h, t: (row(b, t), 0)),
            pl.BlockSpec((rank, dk), lambda b, h, t: (0, h)),
            pl.BlockSpec((1, dk), lambda b, h, t: (0, h)),
            pl.BlockSpec((1, dv), lambda b, h, t: (0, 0)),
            pl.BlockSpec((1, 1, dk, dv), lambda b, h, t: (b, h, 0, 0)),
        ],
        out_specs=[
            pl.BlockSpec((tb, dv), lambda b, h, t: (row(b, t), h)),
            pl.BlockSpec((1, 1, dk, dv), lambda b, h, t: (b, h, 0, 0)),
        ],
        out_shape=[
            jax.ShapeDtypeStruct((m, nh * dv), og_dtype),
            jax.ShapeDtypeStruct((bn, nh, dk, dv), F32),
        ],
        scratch_shapes=[
            pltpu.VMEM((dv, dk), F32),
            pltpu.VMEM((tb // C, dv, dk), F32),
            pltpu.VMEM((tb, dk), F32),
            pltpu.VMEM((tb, dv), F32),
            pltpu.VMEM((tb // C, 8, dk), F32),
        ],
        compiler_params=_params(("parallel", "parallel", "arbitrary")),
        name="gla_chunked",
    )(qk, qk, v, gate, gl, w_up, b_gk.reshape(1, -1), o_g.reshape(1, -1), s0)
    return og, sout


def _gla_layer(x2d, bn, tn_, g, s0, w_in, w_up, b_gk, o_g, w_out, tm, C, tb, og_dtype):
    d = x2d.shape[1]
    nh, dk, dv = s0.shape[1:]
    ckey = nh * dk
    dinner = nh * dv
    rank = w_up.shape[0]
    w3 = w_in.reshape(1, d, w_in.shape[1])
    qk = _proj(x2d, g, w3, 0, 2 * ckey, tm, 512, og_dtype)[0]
    v = _proj(x2d, g, w3, 2 * ckey, dinner, tm, 512, og_dtype)[0]
    gate = _proj(x2d, g, w3, 2 * ckey + dinner, dinner, tm, 512, og_dtype)[0]
    gl = _proj(x2d, g, w_in[:, 2 * ckey + 2 * dinner:].reshape(1, d, rank), 0, rank, tm, rank, F32)[0]
    og, sout = _gla_mix(qk, v, gate, gl, w_up, b_gk, o_g, s0, bn, tn_, C, tb, og_dtype)
    y = _outproj(og, w_out, x2d, tm, 512)
    return y, sout


PROMPT_TM = 1024
RWKV_CHUNK = 64
RWKV_TB = 512
RWKV_PAIRS = 4
RWKV_UNROLL = 2
STAT_PASSES = 1
GLA_CHUNK = 64
GLA_TB = 512


def kernel(x_prompt, x_sample, cache_k_win, cache_v_win, state_wkv, state_shift, state_gla, norm_g, rel_bias, w_in_a, q_norm_g, k_norm_g, sinks, w_out_a, mu_b, w_rkvg_b, w_lora_down_b, w_lora_up_b, w0_b, a0_b, k_k_b, k_a_b, r_k_b, ln_x_g_b, ln_x_b_b, w_out_b, w_in_c, w_gk_up_c, b_gk_c, o_norm_g_c, w_out_c):
    bp, tp, d = x_prompt.shape
    bs, ts, _ = x_sample.shape
    depth = norm_g.shape[0]
    w_in_a, w_out_a, w_rkvg_b, w_lora_down_b, w_out_b, w_in_c, w_out_c = (
        w.astype(BF16) for w in (w_in_a, w_out_a, w_rkvg_b, w_lora_down_b, w_out_b, w_in_c, w_out_c))
    xp = x_prompt.reshape(bp * tp, d)
    xs = x_sample.reshape(bs * ts, d)
    ms = bs * ts
    kwp, vwp, kws, vws, wkvp, shp, wkvs, shs, glap, glas = ([] for _ in range(10))
    for layer in range(depth):
        kind, j = layer % 3, layer // 3
        g = norm_g[layer]
        if kind == 0:
            wa = (w_in_a[j], q_norm_g[j], k_norm_g[j], sinks[j], w_out_a[j], rel_bias)
            xp, kp_, vp_ = _attn_layer(xp, bp, tp, g, *wa, None, None, PROMPT_TM, BF16)
            xs, ks_, vs_ = _attn_layer(xs, bs, ts, g, *wa, cache_k_win[j], cache_v_win[j], ms, F32)
            kwp.append(kp_); vwp.append(vp_); kws.append(ks_); vws.append(vs_)
        elif kind == 1:
            wb = (mu_b[j], w_rkvg_b[j], w_lora_down_b[j], w_lora_up_b[j], w0_b[j], a0_b[j], k_k_b[j], k_a_b[j],
                  r_k_b[j], ln_x_g_b[j], ln_x_b_b[j], w_out_b[j])
            nh = w_rkvg_b.shape[3] // HEAD
            xp, sp_, lp_ = _rwkv_layer(xp, bp, tp, g, jnp.zeros((bp, d), F32), jnp.zeros((bp, nh, HEAD, HEAD), F32),
                                       *wb, PROMPT_TM, RWKV_CHUNK, RWKV_TB, RWKV_PAIRS, RWKV_UNROLL, BF16)
            xs, ss_, ls_ = _rwkv_layer(xs, bs, ts, g, state_shift[j], state_wkv[j], *wb, ms, ts, ts, RWKV_PAIRS, 1, F32)
            wkvp.append(sp_); shp.append(lp_); wkvs.append(ss_); shs.append(ls_)
        else:
            wc = (w_in_c[j], w_gk_up_c[j], b_gk_c[j], o_norm_g_c[j], w_out_c[j])
            xp, sp_ = _gla_layer(xp, bp, tp, g, jnp.zeros((bp,) + state_gla.shape[2:], F32), *wc,
                                 PROMPT_TM, GLA_CHUNK, GLA_TB, BF16)
            xs, ss_ = _gla_layer(xs, bs, ts, g, state_gla[j], *wc, ms, ts, ts, F32)
            glap.append(sp_); glas.append(ss_)
    return (xp.reshape(bp, tp, d), xs.reshape(bs, ts, d),
            jnp.stack(kwp), jnp.stack(vwp), jnp.stack(kws), jnp.stack(vws),
            jnp.stack(wkvp), jnp.stack(shp), jnp.stack(wkvs), jnp.stack(shs),
            jnp.stack(glap), jnp.stack(glas))
```

```python
import functools
import math

import jax
import jax.numpy as jnp
from jax import lax
from jax.experimental import pallas as pl
from jax.experimental.pallas import tpu as pltpu

F32 = jnp.float32
BF16 = jnp.bfloat16

NORM_EPS = 1e-6
HEAD = 64
LANES = 128
WINDOW = 128
N_BUCKETS = 32
MAX_EXACT = N_BUCKETS // 2
MAX_DISTANCE = 128
B_LN_EPS = 64e-5
C_GATE_NORM = 16.0
VMEM_LIMIT = 56 * 1024 * 1024

NN = (((1,), (0,)), ((), ()))
NT = (((1,), (1,)), ((), ()))
TN = (((0,), (0,)), ((), ()))


def _params(sem):
    return pltpu.CompilerParams(dimension_semantics=sem, vmem_limit_bytes=VMEM_LIMIT)


def _split(x):
    hi = x.astype(BF16)
    lo = (x - hi.astype(F32)).astype(BF16)
    return hi, lo


def _mm(a, b, dims=NN, passes=1):
    if passes == 1:
        return lax.dot_general(a.astype(BF16), b.astype(BF16), dims, preferred_element_type=F32)
    a_hi, a_lo = _split(a)
    b_hi, b_lo = _split(b)
    dg = functools.partial(lax.dot_general, dimension_numbers=dims, preferred_element_type=F32)
    return dg(a_hi, b_hi) + (dg(a_hi, b_lo) + dg(a_lo, b_hi))


def _mm_exact_lhs(a_bf16, b):
    b0 = b.astype(BF16)
    r1 = b - b0.astype(F32)
    b1 = r1.astype(BF16)
    b2 = (r1 - b1.astype(F32)).astype(BF16)
    dg = functools.partial(lax.dot_general, dimension_numbers=NN, preferred_element_type=F32)
    return dg(a_bf16, b0) + (dg(a_bf16, b1) + dg(a_bf16, b2))


def _softplus(z):
    return jnp.maximum(z, 0.0) + jnp.log1p(jnp.exp(-jnp.abs(z)))


def _silu(g):
    return g * jax.nn.sigmoid(g)


def _tri(c):
    r = lax.broadcasted_iota(jnp.int32, (c, c), 0)
    col = lax.broadcasted_iota(jnp.int32, (c, c), 1)
    return (col <= r).astype(BF16)


def _rms_rows_kernel(x_ref, g_ref, o_ref):
    x = x_ref[...]
    o_ref[...] = x * lax.rsqrt(jnp.mean(x * x, axis=-1, keepdims=True) + NORM_EPS) * g_ref[...]


def _rms_rows(rows, g):
    n, d = rows.shape
    npad = -(-n // 8) * 8
    rows_p = jnp.pad(rows, ((0, npad - n), (0, 0)))
    out = pl.pallas_call(
        _rms_rows_kernel,
        out_shape=jax.ShapeDtypeStruct((npad, d), F32),
        name="rms_rows",
    )(rows_p, g.reshape(1, d))
    return out[:n]


PROLOGUE_ROWS = 256


def _proj_kernel(*refs, lerp, period, tm):
    if lerp:
        x_ref, g_ref, w_ref, first_ref, mu_ref, o_ref, xm_ref = refs
    else:
        x_ref, g_ref, w_ref, o_ref, xm_ref = refs
    j = pl.program_id(2)

    @pl.when(j == 0)
    def _():
        g = g_ref[...]
        rc = min(PROLOGUE_ROWS, tm)
        carry = first_ref[0, 0:1, :] if lerp else None
        for c0 in range(0, tm, rc):
            x = x_ref[c0:c0 + rc, :]
            h = x * lax.rsqrt(jnp.mean(x * x, axis=-1, keepdims=True) + NORM_EPS) * g
            if lerp:
                row = lax.broadcasted_iota(jnp.int32, h.shape, 0)
                hs = jnp.where(row == 0, carry, pltpu.roll(h, 1, 0))
                if period is not None:
                    hs = jnp.where(row % period == 0, first_ref[0, c0:c0 + rc, :], hs)
                carry = h[rc - 1:rc, :]
                h = h + (hs - h) * mu_ref[0]
            xm_ref[c0:c0 + rc, :] = h.astype(BF16)

    o_ref[0] = jnp.dot(xm_ref[...], w_ref[0], preferred_element_type=F32).astype(o_ref.dtype)


def _proj(x2d, g, w3, col_off, n_out, tm, tn, out_dtype, first=None, mu=None, period=None):
    m, d = x2d.shape
    nc = w3.shape[0]
    lerp = first is not None
    assert m % tm == 0 and n_out % tn == 0 and col_off % tn == 0
    joff = col_off // tn
    in_specs = [
        pl.BlockSpec((tm, d), lambda i, c, j: (i, 0)),
        pl.BlockSpec((1, d), lambda i, c, j: (0, 0)),
        pl.BlockSpec((1, d, tn), lambda i, c, j: (c, 0, joff + j)),
    ]
    args = [x2d, g.reshape(1, d), w3]
    if lerp:
        fr = first.shape[1]
        in_specs += [
            pl.BlockSpec((1, fr, d), lambda i, c, j: (i, 0, 0)),
            pl.BlockSpec((1, 1, d), lambda i, c, j: (c, 0, 0)),
        ]
        args += [first, mu.reshape(nc, 1, d)]
    return pl.pallas_call(
        functools.partial(_proj_kernel, lerp=lerp, period=period, tm=tm),
        grid=(m // tm, nc, n_out // tn),
        in_specs=in_specs,
        out_specs=pl.BlockSpec((1, tm, tn), lambda i, c, j: (c, i, j)),
        out_shape=jax.ShapeDtypeStruct((nc, m, n_out), out_dtype),
        scratch_shapes=[pltpu.VMEM((tm, d), BF16)],
        compiler_params=_params(("parallel", "arbitrary", "arbitrary")),
        name="proj",
    )(*args)


def _outproj_kernel(a_ref, w_ref, x_ref, o_ref):
    o_ref[...] = x_ref[...] + jnp.dot(a_ref[...].astype(BF16), w_ref[...], preferred_element_type=F32)


def _outproj(a2d, w, x2d, tm, tn):
    m, kdim = a2d.shape
    n = w.shape[1]
    return pl.pallas_call(
        _outproj_kernel,
        grid=(m // tm, n // tn),
        in_specs=[
            pl.BlockSpec((tm, kdim), lambda i, j: (i, 0)),
            pl.BlockSpec((kdim, tn), lambda i, j: (0, j)),
            pl.BlockSpec((tm, tn), lambda i, j: (i, j)),
        ],
        out_specs=pl.BlockSpec((tm, tn), lambda i, j: (i, j)),
        out_shape=jax.ShapeDtypeStruct((m, n), F32),
        compiler_params=_params(("parallel", "arbitrary")),
        name="outproj",
    )(a2d, w, x2d)


A_KV_HEADS = 8
A_GROUP = 8


def _attn_kernel(q_ref, kc_ref, vc_ref, kp_ref, vp_ref, gate_ref, bias_ref, qg_ref, kg_ref,
                 og_ref, kwin_ref, vwin_ref, k_scr, v_scr, s_scr, p_scr, *, tq, prompt, nblocks):
    i = pl.program_id(1)
    tk = WINDOW + tq
    kg = kg_ref[...]
    qg = qg_ref[...] * (HEAD ** -0.5)
    nslab = kg.shape[1]
    jr = lax.broadcasted_iota(jnp.int32, (nslab, nslab), 0)
    jc = lax.broadcasted_iota(jnp.int32, (nslab, nslab), 1)
    mean_bd = jnp.where((jr // HEAD) == (jc // HEAD), 1.0 / HEAD, 0.0).astype(BF16)
    ones_v = jnp.ones((tk, HEAD), BF16)

    def slab_norm(x, gain):
        ms = jnp.dot((x * x).astype(BF16), mean_bd, preferred_element_type=F32)
        return x * lax.rsqrt(ms + NORM_EPS) * gain

    kprev = kp_ref[...].astype(F32)
    k_scr[0:WINDOW, :] = slab_norm(kprev, kg) if prompt else kprev
    k_scr[WINDOW:tk, :] = slab_norm(kc_ref[...].astype(F32), kg)
    v_scr[0:WINDOW, :] = vp_ref[...].astype(F32)
    v_scr[WINDOW:tk, :] = vc_ref[...].astype(F32)
    k_scr[0:1, :] = jnp.zeros((1, k_scr.shape[1]), F32)
    v_scr[0:1, :] = jnp.zeros((1, v_scr.shape[1]), F32)

    c = lax.broadcasted_iota(jnp.int32, (tq, tk), 1)
    no_prev = (c >= WINDOW) | (c == 0)
    lo_half = lax.broadcasted_iota(jnp.int32, (tq, LANES), 1) < HEAD

    def group(kh, first_block):
        kk = k_scr[:, kh * HEAD:(kh + 1) * HEAD].astype(BF16)
        vv = v_scr[:, kh * HEAD:(kh + 1) * HEAD].astype(BF16)
        h0 = kh * A_GROUP
        lanes = slice(h0 * HEAD, (h0 + A_GROUP) * HEAD)
        qn = slab_norm(q_ref[:, lanes].astype(F32), qg)
        qs = jnp.concatenate([qn[:, e * HEAD:(e + 1) * HEAD] for e in range(A_GROUP)], axis=0)
        s_scr[...] = lax.dot_general(qs.astype(BF16), kk, NT, preferred_element_type=F32)
        for e in range(A_GROUP):
            rs = slice(e * tq, (e + 1) * tq)
            s = s_scr[rs, :] + bias_ref[h0 + e]
            if first_block:
                s = jnp.where(no_prev, s, -jnp.inf)
            p_scr[rs, :] = jnp.exp(s - jnp.max(s, axis=-1, keepdims=True)).astype(p_scr.dtype)
        x = jnp.dot(p_scr[...].astype(BF16), jnp.concatenate([vv, ones_v, ones_v, vv], axis=-1),
                    preferred_element_type=F32)
        pairs = []
        for e in range(0, A_GROUP, 2):
            xe, xo = x[e * tq:(e + 1) * tq], x[(e + 1) * tq:(e + 2) * tq]
            num = jnp.where(lo_half, xe[:, 0:LANES], xo[:, LANES:2 * LANES])
            den = jnp.where(lo_half, xe[:, LANES:2 * LANES], xo[:, 0:LANES])
            pairs.append(num / den)
        o = jnp.concatenate(pairs, axis=-1)
        og_ref[:, lanes] = (o * _silu(gate_ref[:, lanes].astype(F32))).astype(og_ref.dtype)

    if prompt:
        @pl.when(i == 0)
        def _():
            for kh in range(A_KV_HEADS):
                group(kh, True)

        @pl.when(i > 0)
        def _():
            for kh in range(A_KV_HEADS):
                group(kh, False)
    else:
        for kh in range(A_KV_HEADS):
            group(kh, False)

    @pl.when(i == nblocks - 1)
    def _():
        kwin_ref[0] = k_scr[tk - WINDOW:tk, :]
        vwin_ref[0] = v_scr[tk - WINDOW:tk, :]


def _t5_bucket(dist):
    d = jnp.maximum(dist, 0)
    large = MAX_EXACT + (jnp.log(jnp.maximum(d, 1).astype(F32) / MAX_EXACT)
                         / math.log(MAX_DISTANCE / MAX_EXACT) * (N_BUCKETS - MAX_EXACT)).astype(jnp.int32)
    return jnp.where(d < MAX_EXACT, d, jnp.minimum(large, N_BUCKETS - 1))


def _attn_mix(qgkv, nq, rel_bias, q_g, k_g, sinks, bn, tn_, cache_k, cache_v, og_dtype):
    m = qgkv.shape[0]
    prompt = cache_k is None
    tq = WINDOW if prompt else tn_
    nb = tn_ // tq
    tk = WINDOW + tq
    col = jnp.arange(tk)[None, :]
    dist = WINDOW + jnp.arange(tq)[:, None] - col
    onehot = (_t5_bucket(dist)[None] == jnp.arange(N_BUCKETS)[:, None, None]).astype(F32)
    bias = jnp.einsum("bh,bqk->hqk", rel_bias.astype(F32), onehot, precision=lax.Precision.HIGHEST)
    bias = jnp.where(((dist >= 0) & (dist < WINDOW))[None], bias, -jnp.inf)
    bias = jnp.where((col == 0)[None], sinks.astype(F32)[:, None, None], bias)
    nkv = A_KV_HEADS * HEAD
    row = lambda b, i: b * nb + i
    kcol = 2 * nq // nkv
    if prompt:
        kp_arr, vp_arr = qgkv, qgkv
        kp_spec = pl.BlockSpec((WINDOW, nkv), lambda b, i: (jnp.maximum(row(b, i) - 1, 0), kcol))
        vp_spec = pl.BlockSpec((WINDOW, nkv), lambda b, i: (jnp.maximum(row(b, i) - 1, 0), kcol + 1))
    else:
        kp_arr = cache_k.reshape(bn * WINDOW, nkv)
        vp_arr = cache_v.reshape(bn * WINDOW, nkv)
        kp_spec = pl.BlockSpec((WINDOW, nkv), lambda b, i: (b, 0))
        vp_spec = pl.BlockSpec((WINDOW, nkv), lambda b, i: (b, 0))
    og, kwin, vwin = pl.pallas_call(
        functools.partial(_attn_kernel, tq=tq, prompt=prompt, nblocks=nb),
        grid=(bn, nb),
        in_specs=[
            pl.BlockSpec((tq, nq), lambda b, i: (row(b, i), 0)),
            pl.BlockSpec((tq, nkv), lambda b, i: (row(b, i), kcol)),
            pl.BlockSpec((tq, nkv), lambda b, i: (row(b, i), kcol + 1)),
            kp_spec, vp_spec,
            pl.BlockSpec((tq, nq), lambda b, i: (row(b, i), 1)),
            pl.BlockSpec((nq // HEAD, tq, tk), lambda b, i: (0, 0, 0)),
            pl.BlockSpec((1, nkv), lambda b, i: (0, 0)),
            pl.BlockSpec((1, nkv), lambda b, i: (0, 0)),
        ],
        out_specs=[
            pl.BlockSpec((tq, nq), lambda b, i: (row(b, i), 0)),
            pl.BlockSpec((1, WINDOW, nkv), lambda b, i: (b, 0, 0)),
            pl.BlockSpec((1, WINDOW, nkv), lambda b, i: (b, 0, 0)),
        ],
        out_shape=[
            jax.ShapeDtypeStruct((m, nq), og_dtype),
            jax.ShapeDtypeStruct((bn, WINDOW, nkv), F32),
            jax.ShapeDtypeStruct((bn, WINDOW, nkv), F32),
        ],
        scratch_shapes=[
            pltpu.VMEM((tk, nkv), F32), pltpu.VMEM((tk, nkv), F32),
            pltpu.VMEM((A_GROUP * tq, tk), F32),
            pltpu.VMEM((A_GROUP * tq, tk), BF16 if tq % 16 == 0 else F32),
        ],
        compiler_params=_params(("parallel", "arbitrary")),
        name="swa_attention",
    )(qgkv, qgkv, qgkv, kp_arr, vp_arr, qgkv, bias,
      jnp.tile(q_g.astype(F32), A_GROUP).reshape(1, nkv), jnp.tile(k_g.astype(F32), A_KV_HEADS).reshape(1, nkv))
    return og, kwin, vwin


def _attn_in_weight(w_in, nq):
    nkv2 = w_in.shape[1] - 2 * nq
    w = jnp.concatenate([w_in[:, :nq], w_in[:, nq + nkv2:], w_in[:, nq:nq + nkv2]], axis=1)
    return w.astype(BF16)[None]


def _attn_layer(x2d, bn, tn_, g, w3, q_g, k_g, sinks, w_out, rel_bias, cache_k, cache_v, tm, og_dtype):
    nq = w_out.shape[0]
    qgkv = _proj(x2d, g, w3, 0, w3.shape[2], tm, PROJ_TN, og_dtype)[0]
    og, kwin, vwin = _attn_mix(qgkv, nq, rel_bias, q_g, k_g, sinks, bn, tn_, cache_k, cache_v, og_dtype)
    y = _outproj(og, w_out, x2d, tm, 512)
    shape = (bn, WINDOW, A_KV_HEADS, HEAD)
    return y, kwin.reshape(shape), vwin.reshape(shape)


def _rwkv_kernel(r_ref, k_ref, v_ref, g_ref, lw_ref, la_ref, wup_ref, vec_ref, s0_ref,
                 og_ref, sout_ref,
                 s_ref, kk_scr, k2_scr, a_scr, ld_scr, q_scr, o0_scr, gp_scr, z_scr, ec_scr, o_scr,
                 *, C, tb, lp, nt, unroll, nseq):
    t = pl.program_id(2)
    n = 2 * C
    nchunks = tb // C
    lane = lax.broadcasted_iota(jnp.int32, (1, LANES), 1)
    lo = (lane < HEAD).astype(F32)
    hi = 1.0 - lo

    @pl.when(t == 0)
    def _():
        z = jnp.zeros((HEAD, HEAD), F32)
        for si in range(nseq):
            for pp in range(lp):
                s_ref[si * lp + pp, 0:HEAD, :] = jnp.concatenate([s0_ref[si, 2 * pp], z], axis=-1)
                s_ref[si * lp + pp, HEAD:LANES, :] = jnp.concatenate([z, s0_ref[si, 2 * pp + 1]], axis=-1)

    vec = vec_ref[...]
    w0, a0, kk_w, ka_w, rk_w, ln_g, ln_b = (vec[i:i + 1, :] for i in range(7))

    jr = lax.broadcasted_iota(jnp.int32, (LANES, LANES), 0)
    jc = lax.broadcasted_iota(jnp.int32, (LANES, LANES), 1)
    ones_bd = ((jr // HEAD) == (jc // HEAD)).astype(F32)
    mean_bd = ones_bd * (1.0 / HEAD)

    def per_pair(fn, *xs):
        return jnp.concatenate([fn(*(x[:, pp * LANES:(pp + 1) * LANES] for x in xs)) for pp in range(lp)], axis=-1)

    k = k_ref[0].astype(F32)
    xw = w0 + _mm(jnp.tanh(lw_ref[0]), wup_ref[0], NN, 3)
    ld_scr[...] = -jnp.exp(-_softplus(-xw) - 0.5)
    a = jax.nn.sigmoid(a0 + _mm(la_ref[0], wup_ref[1], NN, 3))
    a_scr[...] = a
    kkr = k * kk_w
    ssq = per_pair(lambda x: _mm(x * x, ones_bd, NN, STAT_PASSES), kkr)
    kk_scr[...] = kkr / jnp.maximum(jnp.sqrt(ssq), 1e-12)
    k2_scr[...] = k * (1.0 + (a - 1.0) * ka_w)

    rr = lax.broadcasted_iota(jnp.int32, (n, n), 0)
    cc = lax.broadcasted_iota(jnp.int32, (n, n), 1)
    strict = cc < rr
    incl = cc <= rr
    eye = (cc == rr).astype(F32)
    tri = _tri(C)
    nsteps = int(math.log2(C)) - 1

    def stage_b(gi, carry):
        units = []
        for cu in range(unroll):
            ci = gi * unroll + cu
            rows = pl.ds(pl.multiple_of(ci * C, C), C)
            ld = ld_scr[rows, :]
            b = _mm_exact_lhs(tri, ld)
            eb = jnp.exp(b)
            enb = jnp.exp(-b)
            kk = kk_scr[rows, :]
            al = -kk * jnp.exp(b - ld)
            be = kk * a_scr[rows, :] * enb
            kt = k2_scr[rows, :] * enb
            rb = r_ref[0, rows, :].astype(F32) * eb
            v = v_ref[0, rows, :].astype(F32)
            e_c = eb[C - 1:C, :]
            ec_scr[ci] = jnp.broadcast_to(e_c, (8, e_c.shape[1]))
            for pp in range(lp):
                ls = slice(pp * LANES, (pp + 1) * LANES)
                cat = lambda x: jnp.concatenate([x[:, ls] * lo, x[:, ls] * hi], axis=0)
                units.append(dict(idx=ci * lp + pp, ecp=e_c[:, ls], la=cat(al), lr=cat(rb), rb=cat(be),
                                  rk=cat(kt), vb=cat(v)))
        nu = range(len(units))
        if n % LANES == 0:
            prod = [_mm(jnp.concatenate([u["la"], u["lr"]], axis=0),
                        jnp.concatenate([u["rb"], u["rk"]], axis=0), NT) for u in units]
            m_ab = [jnp.where(strict, x[0:n, 0:n], 0.0) for x in prod]
            m_ak = [jnp.where(strict, x[0:n, n:2 * n], 0.0) for x in prod]
            m_rb = [jnp.where(incl, x[n:2 * n, 0:n], 0.0) for x in prod]
            m_rk = [jnp.where(incl, x[n:2 * n, n:2 * n], 0.0) for x in prod]
            kv = [_mm(jnp.concatenate([m_ak[i], m_rk[i]], axis=0), units[i]["vb"]) for i in nu]
            akv = [x[0:n] for x in kv]
            rkv = [x[n:2 * n] for x in kv]
            tinv = [eye + a_ for a_ in m_ab]
            apow = [_mm(a_, a_) for a_ in m_ab]
            for step in range(nsteps):
                if step < nsteps - 1:
                    both = [_mm(apow[i], jnp.concatenate([tinv[i], apow[i]], axis=1)) for i in nu]
                    tinv = [tinv[i] + both[i][:, 0:n] for i in nu]
                    apow = [x[:, n:2 * n] for x in both]
                else:
                    tinv = [tinv[i] + _mm(apow[i], tinv[i]) for i in nu]
            wu = [_mm(tinv[i], jnp.concatenate([units[i]["la"], akv[i]], axis=1)) for i in nu]
            qo = [_mm(m_rb[i], wu[i]) for i in nu]
            rbe = [u["rb"] * u["ecp"] for u in units]
            gz = [_mm(wu[i], rbe[i], TN) for i in nu]
            for i, u in enumerate(units):
                q_scr[u["idx"]] = u["lr"] + qo[i][:, 0:LANES]
                o0_scr[u["idx"]] = qo[i][:, LANES:2 * LANES] + rkv[i]
                gp_scr[u["idx"]] = gz[i][0:LANES]
            for i, u in enumerate(units):
                z_scr[u["idx"]] = gz[i][LANES:2 * LANES] + _mm(u["vb"], u["rk"] * u["ecp"], TN)
            return carry
        each = lambda fn: [fn(u) for u in units]
        m_ab = each(lambda u: jnp.where(strict, _mm(u["la"], u["rb"], NT), 0.0))
        m_ak = each(lambda u: jnp.where(strict, _mm(u["la"], u["rk"], NT), 0.0))
        m_rb = each(lambda u: jnp.where(incl, _mm(u["lr"], u["rb"], NT), 0.0))
        m_rk = each(lambda u: jnp.where(incl, _mm(u["lr"], u["rk"], NT), 0.0))
        akv = [_mm(m, u["vb"]) for m, u in zip(m_ak, units)]
        rkv = [_mm(m, u["vb"]) for m, u in zip(m_rk, units)]
        apow = m_ab
        tinv = [eye + a_ for a_ in apow]
        for _ in range(nsteps):
            apow = [_mm(a_, a_) for a_ in apow]
            tinv = [t_ + _mm(a_, t_) for a_, t_ in zip(apow, tinv)]
        w = [_mm(t_, u["la"]) for t_, u in zip(tinv, units)]
        u0 = [_mm(t_, x_) for t_, x_ in zip(tinv, akv)]
        rbe = [u["rb"] * u["ecp"] for u in units]
        for i, u in enumerate(units):
            q_scr[u["idx"]] = u["lr"] + _mm(m_rb[i], w[i])
        for i, u in enumerate(units):
            o0_scr[u["idx"]] = _mm(m_rb[i], u0[i]) + rkv[i]
        for i, u in enumerate(units):
            gp_scr[u["idx"]] = _mm(w[i], rbe[i], TN)
        for i, u in enumerate(units):
            z_scr[u["idx"]] = _mm(u0[i], rbe[i], TN) + _mm(u["vb"], u["rk"] * u["ecp"], TN)
        return carry

    lax.fori_loop(0, nchunks // unroll, stage_b, 0)

    for ci in range(nchunks):
        for pp in range(lp):
            ls = slice(pp * LANES, (pp + 1) * LANES)
            idx = ci * lp + pp
            sidx = idx if nseq > 1 else pp
            s = s_ref[sidx]
            o_bd = _mm(q_scr[idx], s, NT) + o0_scr[idx]
            o_scr[ci * C:(ci + 1) * C, ls] = o_bd[0:C] + o_bd[C:n]
            s_ref[sidx] = s * ec_scr[ci, 0:1, ls] + _mm(s, gp_scr[idx]) + z_scr[idx]

    o = o_scr[...]
    v = v_ref[0].astype(F32)
    mean = per_pair(lambda x: _mm(x, mean_bd, NN, STAT_PASSES), o)
    dlt = o - mean
    var = per_pair(lambda x: _mm(x, mean_bd, NN, STAT_PASSES), dlt * dlt)
    on = dlt * lax.rsqrt(var + B_LN_EPS) * ln_g + ln_b
    bonus = per_pair(lambda x: _mm(x, ones_bd, NN, STAT_PASSES), r_ref[0].astype(F32) * k2_scr[...] * rk_w)
    on = on + bonus * v
    og_ref[...] = (on * _silu(g_ref[0].astype(F32))).astype(og_ref.dtype)

    @pl.when(t == nt - 1)
    def _():
        for si in range(nseq):
            for pp in range(lp):
                sout_ref[si, 2 * pp] = s_ref[si * lp + pp, 0:HEAD, 0:HEAD]
                sout_ref[si, 2 * pp + 1] = s_ref[si * lp + pp, HEAD:LANES, HEAD:LANES]


def _rwkv_mix(rkvg, lora, wup, vec, s0, bn, tn_, C, tb, lp, unroll, nseq, og_dtype):
    m = rkvg.shape[1]
    nch = rkvg.shape[2]
    wl = lp * LANES
    ngrp = nch // wl
    nt = tn_ * nseq // tb
    nchunks = tb // C
    rank = lora.shape[2]
    nh = s0.shape[1]
    row = lambda b, t: b * nt + t

    def xspec(c):
        return pl.BlockSpec((1, tb, wl), lambda b, p, t: (c, row(b, t), p))

    def lspec(c):
        return pl.BlockSpec((1, tb, rank), lambda b, p, t: (c, row(b, t), 0))

    og, sout = pl.pallas_call(
        functools.partial(_rwkv_kernel, C=C, tb=tb, lp=lp, nt=nt, unroll=unroll, nseq=nseq),
        grid=(bn // nseq, ngrp, nt),
        in_specs=[
            xspec(0), xspec(1), xspec(2), xspec(3), lspec(0), lspec(1),
            pl.BlockSpec((2, rank, wl), lambda b, p, t: (0, 0, p)),
            pl.BlockSpec((8, wl), lambda b, p, t: (0, p)),
            pl.BlockSpec((nseq, 2 * lp, HEAD, HEAD), lambda b, p, t: (b, p, 0, 0)),
        ],
        out_specs=[
            pl.BlockSpec((tb, wl), lambda b, p, t: (row(b, t), p)),
            pl.BlockSpec((nseq, 2 * lp, HEAD, HEAD), lambda b, p, t: (b, p, 0, 0)),
        ],
        out_shape=[
            jax.ShapeDtypeStruct((m, nch), og_dtype),
            jax.ShapeDtypeStruct((bn, nh, HEAD, HEAD), F32),
        ],
        scratch_shapes=[
            pltpu.VMEM((nseq * lp, LANES, LANES), F32),
            pltpu.VMEM((tb, wl), F32), pltpu.VMEM((tb, wl), F32), pltpu.VMEM((tb, wl), F32), pltpu.VMEM((tb, wl), F32),
            pltpu.VMEM((nchunks * lp, 2 * C, LANES), F32), pltpu.VMEM((nchunks * lp, 2 * C, LANES), F32),
            pltpu.VMEM((nchunks * lp, LANES, LANES), F32), pltpu.VMEM((nchunks * lp, LANES, LANES), F32),
            pltpu.VMEM((nchunks, 8, wl), F32),
            pltpu.VMEM((tb, wl), F32),
        ],
        compiler_params=_params(("parallel", "parallel", "arbitrary")),
        name="rwkv7_chunked",
    )(rkvg, rkvg, rkvg, rkvg, lora, lora, wup, vec, s0)
    return og, sout


def _rwkv_layer(x2d, bn, tn_, g, shift, s0, mu, w_rkvg, w_down, w_up, w0, a0, k_k, k_a, r_k, ln_g, ln_b,
                w_out, tm, C, tb, lp, unroll, nseq, og_dtype):
    m, d = x2d.shape
    nblk = m // tm
    if tm <= tn_:
        starts = jnp.arange(nblk) * tm
        prev = _rms_rows(x2d[jnp.maximum(starts - 1, 0)], g)
        first = jnp.where((starts % tn_ == 0)[:, None], shift[starts // tn_], prev).reshape(nblk, 1, d)
        period = None
    else:
        assert nblk == 1
        first = jnp.zeros((bn, tn_, d), F32).at[:, 0].set(shift).reshape(1, m, d)
        period = tn_
    shift_out = _rms_rows(x2d[tn_ - 1::tn_], g)
    rkvg = _proj(x2d, g, w_rkvg, 0, w_rkvg.shape[2], tm, PROJ_TN, og_dtype, first=first, mu=mu[:4], period=period)
    rank = w_down.shape[2]
    lora = _proj(x2d, g, w_down, 0, rank, tm, rank, F32, first=first, mu=mu[4:6], period=period)
    nch = w_rkvg.shape[2]
    vec = jnp.stack([w0, a0, k_k, k_a, r_k.reshape(nch), ln_g, ln_b, jnp.zeros((nch,), F32)])
    og, sout = _rwkv_mix(rkvg, lora, w_up, vec, s0, bn, tn_, C, tb, lp, unroll, nseq, og_dtype)
    y = _outproj(og, w_out, x2d, tm, 512)
    return y, sout, shift_out


GLA_SUB = 16


def _gla_kernel(q_ref, k_ref, v_ref, gate_ref, gl_ref, wup_ref, bgk_ref, og_w_ref, s0_ref,
                og_ref, sout_ref, st_ref, z_scr, qd_scr, o_scr, ec_scr, *, C, tb, nt, nseq):
    t = pl.program_id(2)
    dk = q_ref.shape[-1]
    nchunks = tb // C
    sub = min(GLA_SUB, C)
    nsub = C // sub
    qscale = dk ** -0.5

    @pl.when(t == 0)
    def _():
        for si in range(nseq):
            st_ref[si] = s0_ref[si, 0].T

    tri = _tri(C)
    glog_all = -_softplus(-(_mm(gl_ref[...], wup_ref[...], NN, 3) + bgk_ref[...])) * (1.0 / C_GATE_NORM)
    chunks = range(nchunks)
    rows = [slice(c * C, (c + 1) * C) for c in chunks]
    glog = [glog_all[rs] for rs in rows]
    b = [_mm_exact_lhs(tri, gl_) for gl_ in glog]
    q = [q_ref[rs, :].astype(F32) * qscale for rs in rows]
    k = [k_ref[rs, :].astype(F32) for rs in rows]
    v = [v_ref[rs, :].astype(F32) for rs in rows]
    for c in chunks:
        bl = b[c][C - 1:C, :]
        qd_scr[rows[c], :] = q[c] * jnp.exp(b[c])
        ec_scr[c] = jnp.broadcast_to(jnp.exp(bl), (8, dk))
        z_scr[c] = _mm(v[c], k[c] * jnp.exp(bl - b[c]), TN)
    parts = [[] for _ in chunks]
    for i in range(nsub):
        r0 = i * sub
        nk = r0 + sub
        ar = lax.broadcasted_iota(jnp.int32, (sub, nk), 0) + r0
        ac = lax.broadcasted_iota(jnp.int32, (sub, nk), 1)
        att = []
        for c in chunks:
            ref = b[c][r0:r0 + 1, :] - glog[c][r0:r0 + 1, :]
            qi = q[c][r0:nk] * jnp.exp(b[c][r0:nk] - ref)
            ki = k[c][0:nk] * jnp.exp(ref - b[c][0:nk])
            att.append(jnp.where(ac <= ar, _mm(qi, ki, NT), 0.0))
        for c in chunks:
            parts[c].append(_mm(att[c], v[c][0:nk]))
    for c in chunks:
        o_scr[rows[c], :] = jnp.concatenate(parts[c], axis=0) if nsub > 1 else parts[c][0]

    for c in chunks:
        si = c if nseq > 1 else 0
        st = st_ref[si]
        o_scr[rows[c], :] = o_scr[rows[c], :] + _mm(qd_scr[rows[c], :], st, NT)
        st_ref[si] = st * ec_scr[c, 0:1, :] + z_scr[c]

    o = o_scr[...]
    on = o * lax.rsqrt(jnp.mean(o * o, axis=-1, keepdims=True) + NORM_EPS) * og_w_ref[...]
    og_ref[...] = (on * _silu(gate_ref[...].astype(F32))).astype(og_ref.dtype)

    @pl.when(t == nt - 1)
    def _():
        for si in range(nseq):
            sout_ref[si, 0] = st_ref[si].T


def _gla_mix(qkvg, gl, w_up, b_gk, o_g, s0, bn, tn_, C, tb, nseq, og_dtype):
    m = qkvg.shape[0]
    nh, dk, dv = s0.shape[1:]
    nt = tn_ * nseq // tb
    rank = gl.shape[1]
    row = lambda b, t: b * nt + t
    kcol = nh
    vcol = 2 * nh * dk // dv
    gcol = vcol + nh
    og, sout = pl.pallas_call(
        functools.partial(_gla_kernel, C=C, tb=tb, nt=nt, nseq=nseq),
        grid=(bn // nseq, nh, nt),
        in_specs=[
            pl.BlockSpec((tb, dk), lambda b, h, t: (row(b, t), h)),
            pl.BlockSpec((tb, dk), lambda b, h, t: (row(b, t), kcol + h)),
            pl.BlockSpec((tb, dv), lambda b, h, t: (row(b, t), vcol + h)),
            pl.BlockSpec((tb, dv), lambda b, h, t: (row(b, t), gcol + h)),
            pl.BlockSpec((tb, rank), lambda b, h, t: (row(b, t), 0)),
            pl.BlockSpec((rank, dk), lambda b, h, t: (0, h)),
            pl.BlockSpec((1, dk), lambda b, h, t: (0, h)),
            pl.BlockSpec((1, dv), lambda b, h, t: (0, 0)),
            pl.BlockSpec((nseq, 1, dk, dv), lambda b, h, t: (b, h, 0, 0)),
        ],
        out_specs=[
            pl.BlockSpec((tb, dv), lambda b, h, t: (row(b, t), h)),
            pl.BlockSpec((nseq, 1, dk, dv), lambda b, h, t: (b, h, 0, 0)),
        ],
        out_shape=[
            jax.ShapeDtypeStruct((m, nh * dv), og_dtype),
            jax.ShapeDtypeStruct((bn, nh, dk, dv), F32),
        ],
        scratch_shapes=[
            pltpu.VMEM((nseq, dv, dk), F32),
            pltpu.VMEM((tb // C, dv, dk), F32),
            pltpu.VMEM((tb, dk), F32),
            pltpu.VMEM((tb, dv), F32),
            pltpu.VMEM((tb // C, 8, dk), F32),
        ],
        compiler_params=_params(("parallel", "parallel", "arbitrary")),
        name="gla_chunked",
    )(qkvg, qkvg, qkvg, qkvg, gl, w_up, b_gk.reshape(1, -1), o_g.reshape(1, -1), s0)
    return og, sout


def _gla_layer(x2d, bn, tn_, g, s0, w3, w_up, b_gk, o_g, w_out, tm, C, tb, nseq, og_dtype):
    d = x2d.shape[1]
    nh, dk, dv = s0.shape[1:]
    nfused = 2 * nh * dk + 2 * nh * dv
    rank = w_up.shape[0]
    qkvg = _proj(x2d, g, w3, 0, nfused, tm, PROJ_TN, og_dtype)[0]
    gl = _proj(x2d, g, w3[:, :, nfused:], 0, rank, tm, rank, F32)[0]
    og, sout = _gla_mix(qkvg, gl, w_up, b_gk, o_g, s0, bn, tn_, C, tb, nseq, og_dtype)
    y = _outproj(og, w_out, x2d, tm, 512)
    return y, sout


PROMPT_TM = 1024
RWKV_CHUNK = 64
RWKV_TB = 512
RWKV_PAIRS = 4
RWKV_UNROLL = 4
STAT_PASSES = 1
GLA_CHUNK = 64
GLA_TB = 512
PROJ_TN = 1024
SAMPLE_SEQS = 4


def kernel(x_prompt, x_sample, cache_k_win, cache_v_win, state_wkv, state_shift, state_gla, norm_g, rel_bias, w_in_a, q_norm_g, k_norm_g, sinks, w_out_a, mu_b, w_rkvg_b, w_lora_down_b, w_lora_up_b, w0_b, a0_b, k_k_b, k_a_b, r_k_b, ln_x_g_b, ln_x_b_b, w_out_b, w_in_c, w_gk_up_c, b_gk_c, o_norm_g_c, w_out_c):
    bp, tp, d = x_prompt.shape
    bs, ts, _ = x_sample.shape
    depth = norm_g.shape[0]
    w_out_a, w_rkvg_b, w_lora_down_b, w_out_b, w_in_c, w_out_c = (
        w.astype(BF16) for w in (w_out_a, w_rkvg_b, w_lora_down_b, w_out_b, w_in_c, w_out_c))
    xp = x_prompt.reshape(bp * tp, d)
    xs = x_sample.reshape(bs * ts, d)
    ms = bs * ts
    kwp, vwp, kws, vws, wkvp, shp, wkvs, shs, glap, glas = ([] for _ in range(10))
    for layer in range(depth):
        kind, j = layer % 3, layer // 3
        g = norm_g[layer]
        if kind == 0:
            wa = (_attn_in_weight(w_in_a[j], w_out_a.shape[1]), q_norm_g[j], k_norm_g[j], sinks[j], w_out_a[j], rel_bias)
            xp, kp_, vp_ = _attn_layer(xp, bp, tp, g, *wa, None, None, PROMPT_TM, BF16)
            xs, ks_, vs_ = _attn_layer(xs, bs, ts, g, *wa, cache_k_win[j], cache_v_win[j], ms, F32)
            kwp.append(kp_); vwp.append(vp_); kws.append(ks_); vws.append(vs_)
        elif kind == 1:
            wb = (mu_b[j], w_rkvg_b[j], w_lora_down_b[j], w_lora_up_b[j], w0_b[j], a0_b[j], k_k_b[j], k_a_b[j],
                  r_k_b[j], ln_x_g_b[j], ln_x_b_b[j], w_out_b[j])
            nh = w_rkvg_b.shape[3] // HEAD
            xp, sp_, lp_ = _rwkv_layer(xp, bp, tp, g, jnp.zeros((bp, d), F32), jnp.zeros((bp, nh, HEAD, HEAD), F32),
                                       *wb, PROMPT_TM, RWKV_CHUNK, RWKV_TB, RWKV_PAIRS, RWKV_UNROLL, 1, BF16)
            xs, ss_, ls_ = _rwkv_layer(xs, bs, ts, g, state_shift[j], state_wkv[j], *wb, ms, ts, ts * SAMPLE_SEQS,
                                       RWKV_PAIRS, SAMPLE_SEQS, SAMPLE_SEQS, F32)
            wkvp.append(sp_); shp.append(lp_); wkvs.append(ss_); shs.append(ls_)
        else:
            wc = (w_in_c[j][None], w_gk_up_c[j], b_gk_c[j], o_norm_g_c[j], w_out_c[j])
            xp, sp_ = _gla_layer(xp, bp, tp, g, jnp.zeros((bp,) + state_gla.shape[2:], F32), *wc,
                                 PROMPT_TM, GLA_CHUNK, GLA_TB, 1, BF16)
            xs, ss_ = _gla_layer(xs, bs, ts, g, state_gla[j], *wc, ms, ts, ts * SAMPLE_SEQS, SAMPLE_SEQS, F32)
            glap.append(sp_); glas.append(ss_)
    return (xp.reshape(bp, tp, d), xs.reshape(bs, ts, d),
            jnp.stack(kwp), jnp.stack(vwp), jnp.stack(kws), jnp.stack(vws),
            jnp.stack(wkvp), jnp.stack(shp), jnp.stack(wkvs), jnp.stack(shs),
            jnp.stack(glap), jnp.stack(glas))
```

```python
import functools
import math

import jax
import jax.numpy as jnp
from jax import lax
from jax.experimental import pallas as pl
from jax.experimental.pallas import tpu as pltpu

F32 = jnp.float32
BF16 = jnp.bfloat16

NORM_EPS = 1e-6
HEAD = 64
LANES = 128
WINDOW = 128
N_BUCKETS = 32
MAX_EXACT = N_BUCKETS // 2
MAX_DISTANCE = 128
B_LN_EPS = 64e-5
C_GATE_NORM = 16.0
VMEM_LIMIT = 56 * 1024 * 1024

NN = (((1,), (0,)), ((), ()))
NT = (((1,), (1,)), ((), ()))
TN = (((0,), (0,)), ((), ()))


def _params(sem):
    return pltpu.CompilerParams(dimension_semantics=sem, vmem_limit_bytes=VMEM_LIMIT)


def _split(x):
    hi = x.astype(BF16)
    lo = (x - hi.astype(F32)).astype(BF16)
    return hi, lo


def _mm(a, b, dims=NN, passes=1):
    if passes == 1:
        return lax.dot_general(a.astype(BF16), b.astype(BF16), dims, preferred_element_type=F32)
    a_hi, a_lo = _split(a)
    b_hi, b_lo = _split(b)
    dg = functools.partial(lax.dot_general, dimension_numbers=dims, preferred_element_type=F32)
    return dg(a_hi, b_hi) + (dg(a_hi, b_lo) + dg(a_lo, b_hi))


def _mm_exact_lhs(a_bf16, b):
    b0 = b.astype(BF16)
    r1 = b - b0.astype(F32)
    b1 = r1.astype(BF16)
    b2 = (r1 - b1.astype(F32)).astype(BF16)
    dg = functools.partial(lax.dot_general, dimension_numbers=NN, preferred_element_type=F32)
    return dg(a_bf16, b0) + (dg(a_bf16, b1) + dg(a_bf16, b2))


def _softplus(z):
    return jnp.maximum(z, 0.0) + jnp.log1p(jnp.exp(-jnp.abs(z)))


def _silu(g):
    return g * jax.nn.sigmoid(g)


def _tri(c):
    r = lax.broadcasted_iota(jnp.int32, (c, c), 0)
    col = lax.broadcasted_iota(jnp.int32, (c, c), 1)
    return (col <= r).astype(BF16)


def _rms_rows_kernel(x_ref, g_ref, o_ref):
    x = x_ref[...]
    o_ref[...] = x * lax.rsqrt(jnp.mean(x * x, axis=-1, keepdims=True) + NORM_EPS) * g_ref[...]


def _rms_rows(rows, g):
    n, d = rows.shape
    npad = -(-n // 8) * 8
    rows_p = jnp.pad(rows, ((0, npad - n), (0, 0)))
    out = pl.pallas_call(
        _rms_rows_kernel,
        out_shape=jax.ShapeDtypeStruct((npad, d), F32),
        name="rms_rows",
    )(rows_p, g.reshape(1, d))
    return out[:n]


PROLOGUE_ROWS = 256


def _proj_kernel(*refs, lerp, period, tm):
    if lerp:
        x_ref, g_ref, w_ref, first_ref, mu_ref, o_ref, xm_ref, h_ref, d_ref = refs
    else:
        x_ref, g_ref, w_ref, o_ref, xm_ref = refs
    c = pl.program_id(1)
    j = pl.program_id(2)
    rc = min(PROLOGUE_ROWS, tm)

    def normed(c0):
        x = x_ref[c0:c0 + rc, :]
        return x * lax.rsqrt(jnp.mean(x * x, axis=-1, keepdims=True) + NORM_EPS) * g_ref[...]

    if lerp:
        @pl.when((c == 0) & (j == 0))
        def _():
            carry = first_ref[0, 0:1, :]
            for c0 in range(0, tm, rc):
                h = normed(c0)
                row = lax.broadcasted_iota(jnp.int32, h.shape, 0)
                hs = jnp.where(row == 0, carry, pltpu.roll(h, 1, 0))
                if period is not None:
                    hs = jnp.where(row % period == 0, first_ref[0, c0:c0 + rc, :], hs)
                carry = h[rc - 1:rc, :]
                h_ref[c0:c0 + rc, :] = h
                d_ref[c0:c0 + rc, :] = hs - h

        @pl.when(j == 0)
        def _():
            for c0 in range(0, tm, rc):
                xm_ref[c0:c0 + rc, :] = (h_ref[c0:c0 + rc, :] + d_ref[c0:c0 + rc, :] * mu_ref[0]).astype(BF16)
    else:
        @pl.when(j == 0)
        def _():
            for c0 in range(0, tm, rc):
                xm_ref[c0:c0 + rc, :] = normed(c0).astype(BF16)

    o_ref[0] = jnp.dot(xm_ref[...], w_ref[0], preferred_element_type=F32).astype(o_ref.dtype)


def _pick_tn(n, cap):
    if n < LANES:
        return n
    units = n // LANES
    return LANES * max(u for u in range(1, units + 1) if units % u == 0 and u * LANES <= cap)


def _proj(x2d, g, w3, col_off, n_out, tm, tn, out_dtype, first=None, mu=None, period=None):
    m, d = x2d.shape
    nc = w3.shape[0]
    lerp = first is not None
    assert m % tm == 0 and n_out % tn == 0 and col_off % tn == 0
    joff = col_off // tn
    in_specs = [
        pl.BlockSpec((tm, d), lambda i, c, j: (i, 0)),
        pl.BlockSpec((1, d), lambda i, c, j: (0, 0)),
        pl.BlockSpec((1, d, tn), lambda i, c, j: (c, 0, joff + j)),
    ]
    args = [x2d, g.reshape(1, d), w3]
    if lerp:
        fr = first.shape[1]
        in_specs += [
            pl.BlockSpec((1, fr, d), lambda i, c, j: (i, 0, 0)),
            pl.BlockSpec((1, 1, d), lambda i, c, j: (c, 0, 0)),
        ]
        args += [first, mu.reshape(nc, 1, d)]
    return pl.pallas_call(
        functools.partial(_proj_kernel, lerp=lerp, period=period, tm=tm),
        grid=(m // tm, nc, n_out // tn),
        in_specs=in_specs,
        out_specs=pl.BlockSpec((1, tm, tn), lambda i, c, j: (c, i, j)),
        out_shape=jax.ShapeDtypeStruct((nc, m, n_out), out_dtype),
        scratch_shapes=[pltpu.VMEM((tm, d), BF16)] + ([pltpu.VMEM((tm, d), F32)] * 2 if lerp else []),
        compiler_params=_params(("parallel", "arbitrary", "arbitrary")),
        name="proj",
    )(*args)


def _outproj_kernel(a_ref, w_ref, x_ref, o_ref):
    o_ref[...] = x_ref[...] + jnp.dot(a_ref[...].astype(BF16), w_ref[...], preferred_element_type=F32)


def _outproj(a2d, w, x2d, tm, tn):
    m, kdim = a2d.shape
    n = w.shape[1]
    return pl.pallas_call(
        _outproj_kernel,
        grid=(m // tm, n // tn),
        in_specs=[
            pl.BlockSpec((tm, kdim), lambda i, j: (i, 0)),
            pl.BlockSpec((kdim, tn), lambda i, j: (0, j)),
            pl.BlockSpec((tm, tn), lambda i, j: (i, j)),
        ],
        out_specs=pl.BlockSpec((tm, tn), lambda i, j: (i, j)),
        out_shape=jax.ShapeDtypeStruct((m, n), F32),
        compiler_params=_params(("parallel", "arbitrary")),
        name="outproj",
    )(a2d, w, x2d)


A_KV_HEADS = 8
A_GROUP = 8


def _attn_kernel(*refs, tq, prompt, nblocks):
    q_ref, kc_ref, vc_ref, kp_ref, vp_ref = refs[:5]
    gate_refs = refs[5:5 + A_KV_HEADS]
    bias_ref, qg_ref, kg_ref, og_ref, kwin_ref, vwin_ref, k_scr, v_scr, s_scr, p_scr = refs[5 + A_KV_HEADS:]
    return _attn_body(q_ref, kc_ref, vc_ref, kp_ref, vp_ref, gate_refs, bias_ref, qg_ref, kg_ref,
                      og_ref, kwin_ref, vwin_ref, k_scr, v_scr, s_scr, p_scr, tq=tq, prompt=prompt, nblocks=nblocks)


def _attn_body(q_ref, kc_ref, vc_ref, kp_ref, vp_ref, gate_refs, bias_ref, qg_ref, kg_ref,
               og_ref, kwin_ref, vwin_ref, k_scr, v_scr, s_scr, p_scr, *, tq, prompt, nblocks):
    i = pl.program_id(1)
    tk = WINDOW + tq
    kg = kg_ref[...]
    qg = qg_ref[...] * (HEAD ** -0.5)
    nslab = kg.shape[1]
    jr = lax.broadcasted_iota(jnp.int32, (nslab, nslab), 0)
    jc = lax.broadcasted_iota(jnp.int32, (nslab, nslab), 1)
    mean_bd = jnp.where((jr // HEAD) == (jc // HEAD), 1.0 / HEAD, 0.0).astype(BF16)
    ones_v = jnp.ones((tk, HEAD), BF16)

    def slab_norm(x, gain):
        ms = jnp.dot((x * x).astype(BF16), mean_bd, preferred_element_type=F32)
        return x * lax.rsqrt(ms + NORM_EPS) * gain

    if prompt:
        k_scr[0:WINDOW, :] = slab_norm(kp_ref[...].astype(F32), kg)
        v_scr[0:WINDOW, :] = vp_ref[...].astype(F32)
    else:
        for kh in range(A_KV_HEADS):
            k_scr[0:WINDOW, kh * HEAD:(kh + 1) * HEAD] = kp_ref[0, :, kh, :]
            v_scr[0:WINDOW, kh * HEAD:(kh + 1) * HEAD] = vp_ref[0, :, kh, :]
    k_scr[WINDOW:tk, :] = slab_norm(kc_ref[...].astype(F32), kg)
    v_scr[WINDOW:tk, :] = vc_ref[...].astype(F32)

    @pl.when(i == nblocks - 1)
    def _():
        for kh in range(A_KV_HEADS):
            kwin_ref[0, :, kh, :] = k_scr[tk - WINDOW:tk, kh * HEAD:(kh + 1) * HEAD]
            vwin_ref[0, :, kh, :] = v_scr[tk - WINDOW:tk, kh * HEAD:(kh + 1) * HEAD]

    k_scr[0:1, :] = jnp.zeros((1, k_scr.shape[1]), F32)
    v_scr[0:1, :] = jnp.zeros((1, v_scr.shape[1]), F32)

    c = lax.broadcasted_iota(jnp.int32, (tq, tk), 1)
    no_prev = (c >= WINDOW) | (c == 0)
    lo_half = lax.broadcasted_iota(jnp.int32, (tq, LANES), 1) < HEAD

    def group(kh, first_block):
        kk = k_scr[:, kh * HEAD:(kh + 1) * HEAD].astype(BF16)
        vv = v_scr[:, kh * HEAD:(kh + 1) * HEAD].astype(BF16)
        h0 = kh * A_GROUP
        lanes = slice(h0 * HEAD, (h0 + A_GROUP) * HEAD)
        qn = slab_norm(q_ref[:, lanes].astype(F32), qg)
        qs = jnp.concatenate([qn[:, e * HEAD:(e + 1) * HEAD] for e in range(A_GROUP)], axis=0)
        s_scr[...] = lax.dot_general(qs.astype(BF16), kk, NT, preferred_element_type=F32)
        for e in range(A_GROUP):
            rs = slice(e * tq, (e + 1) * tq)
            s = s_scr[rs, :] + bias_ref[h0 + e]
            if first_block:
                s = jnp.where(no_prev, s, -jnp.inf)
            p_scr[rs, :] = jnp.exp(s - jnp.max(s, axis=-1, keepdims=True)).astype(p_scr.dtype)
        x = jnp.dot(p_scr[...].astype(BF16), jnp.concatenate([vv, ones_v, ones_v, vv], axis=-1),
                    preferred_element_type=F32)
        pairs = []
        for e in range(0, A_GROUP, 2):
            xe, xo = x[e * tq:(e + 1) * tq], x[(e + 1) * tq:(e + 2) * tq]
            num = jnp.where(lo_half, xe[:, 0:LANES], xo[:, LANES:2 * LANES])
            den = jnp.where(lo_half, xe[:, LANES:2 * LANES], xo[:, 0:LANES])
            pairs.append(num / den)
        o = jnp.concatenate(pairs, axis=-1)
        og_ref[:, lanes] = (o * _silu(gate_refs[kh][...].astype(F32))).astype(og_ref.dtype)

    if prompt:
        @pl.when(i == 0)
        def _():
            for kh in range(A_KV_HEADS):
                group(kh, True)

        @pl.when(i > 0)
        def _():
            for kh in range(A_KV_HEADS):
                group(kh, False)
    else:
        for kh in range(A_KV_HEADS):
            group(kh, False)


def _t5_bucket(dist):
    d = jnp.maximum(dist, 0)
    large = MAX_EXACT + (jnp.log(jnp.maximum(d, 1).astype(F32) / MAX_EXACT)
                         / math.log(MAX_DISTANCE / MAX_EXACT) * (N_BUCKETS - MAX_EXACT)).astype(jnp.int32)
    return jnp.where(d < MAX_EXACT, d, jnp.minimum(large, N_BUCKETS - 1))


def _attn_mix(qkvg, nq, rel_bias, q_g, k_g, sinks, bn, tn_, cache_k, cache_v, og_dtype):
    m = qkvg.shape[0]
    prompt = cache_k is None
    tq = WINDOW if prompt else tn_
    nb = tn_ // tq
    tk = WINDOW + tq
    col = jnp.arange(tk)[None, :]
    dist = WINDOW + jnp.arange(tq)[:, None] - col
    onehot = (_t5_bucket(dist)[None] == jnp.arange(N_BUCKETS)[:, None, None]).astype(F32)
    bias = jnp.einsum("bh,bqk->hqk", rel_bias.astype(F32), onehot, precision=lax.Precision.HIGHEST)
    bias = jnp.where(((dist >= 0) & (dist < WINDOW))[None], bias, -jnp.inf)
    bias = jnp.where((col == 0)[None], sinks.astype(F32)[:, None, None], bias)
    nkv = A_KV_HEADS * HEAD
    row = lambda b, i: b * nb + i
    kcol = nq // nkv
    gcol = kcol + 2
    win_shape = (1, WINDOW, A_KV_HEADS, HEAD)
    if prompt:
        kp_arr, vp_arr = qkvg, qkvg
        kp_spec = pl.BlockSpec((WINDOW, nkv), lambda b, i: (jnp.maximum(row(b, i) - 1, 0), kcol))
        vp_spec = pl.BlockSpec((WINDOW, nkv), lambda b, i: (jnp.maximum(row(b, i) - 1, 0), kcol + 1))
    else:
        kp_arr, vp_arr = cache_k, cache_v
        kp_spec = pl.BlockSpec(win_shape, lambda b, i: (b, 0, 0, 0))
        vp_spec = pl.BlockSpec(win_shape, lambda b, i: (b, 0, 0, 0))

    def gate_spec(kh):
        return pl.BlockSpec((tq, nkv), lambda b, i: (row(b, i), gcol + kh))

    og, kwin, vwin = pl.pallas_call(
        functools.partial(_attn_kernel, tq=tq, prompt=prompt, nblocks=nb),
        grid=(bn, nb),
        in_specs=[
            pl.BlockSpec((tq, nq), lambda b, i: (row(b, i), 0)),
            pl.BlockSpec((tq, nkv), lambda b, i: (row(b, i), kcol)),
            pl.BlockSpec((tq, nkv), lambda b, i: (row(b, i), kcol + 1)),
            kp_spec, vp_spec,
            *[gate_spec(kh) for kh in range(A_KV_HEADS)],
            pl.BlockSpec((nq // HEAD, tq, tk), lambda b, i: (0, 0, 0)),
            pl.BlockSpec((1, nkv), lambda b, i: (0, 0)),
            pl.BlockSpec((1, nkv), lambda b, i: (0, 0)),
        ],
        out_specs=[
            pl.BlockSpec((tq, nq), lambda b, i: (row(b, i), 0)),
            pl.BlockSpec(win_shape, lambda b, i: (b, 0, 0, 0)),
            pl.BlockSpec(win_shape, lambda b, i: (b, 0, 0, 0)),
        ],
        out_shape=[
            jax.ShapeDtypeStruct((m, nq), og_dtype),
            jax.ShapeDtypeStruct((bn,) + win_shape[1:], F32),
            jax.ShapeDtypeStruct((bn,) + win_shape[1:], F32),
        ],
        scratch_shapes=[
            pltpu.VMEM((tk, nkv), F32), pltpu.VMEM((tk, nkv), F32),
            pltpu.VMEM((A_GROUP * tq, tk), F32),
            pltpu.VMEM((A_GROUP * tq, tk), BF16 if tq % 16 == 0 else F32),
        ],
        compiler_params=_params(("parallel", "arbitrary")),
        name="swa_attention",
    )(qkvg, qkvg, qkvg, kp_arr, vp_arr, *([qkvg] * A_KV_HEADS), bias,
      jnp.tile(q_g.astype(F32), A_GROUP).reshape(1, nkv), jnp.tile(k_g.astype(F32), A_KV_HEADS).reshape(1, nkv))
    return og, kwin, vwin


def _attn_layer(x2d, bn, tn_, g, w3, q_g, k_g, sinks, w_out, rel_bias, cache_k, cache_v, tm, tn_cap, og_dtype):
    nq = w_out.shape[0]
    nall = w3.shape[2]
    qkvg = _proj(x2d, g, w3, 0, nall, tm, _pick_tn(nall, tn_cap), og_dtype)[0]
    og, kwin, vwin = _attn_mix(qkvg, nq, rel_bias, q_g, k_g, sinks, bn, tn_, cache_k, cache_v, og_dtype)
    y = _outproj(og, w_out, x2d, tm, 512)
    return y, kwin, vwin


def _rwkv_kernel(r_ref, k_ref, v_ref, g_ref, lw_ref, la_ref, wup_ref, vec_ref, s0_ref,
                 og_ref, sout_ref,
                 s_ref, kk_scr, k2_scr, a_scr, ld_scr, q_scr, o0_scr, gp_scr, z_scr, ec_scr, o_scr,
                 *, C, tb, lp, nt, unroll, nseq):
    t = pl.program_id(2)
    n = 2 * C
    nchunks = tb // C
    lane = lax.broadcasted_iota(jnp.int32, (1, LANES), 1)
    lo = (lane < HEAD).astype(F32)
    hi = 1.0 - lo

    @pl.when(t == 0)
    def _():
        z = jnp.zeros((HEAD, HEAD), F32)
        for si in range(nseq):
            for pp in range(lp):
                s_ref[si * lp + pp, 0:HEAD, :] = jnp.concatenate([s0_ref[si, 2 * pp], z], axis=-1)
                s_ref[si * lp + pp, HEAD:LANES, :] = jnp.concatenate([z, s0_ref[si, 2 * pp + 1]], axis=-1)

    vec = vec_ref[...]
    w0, a0, kk_w, ka_w, rk_w, ln_g, ln_b = (vec[i:i + 1, :] for i in range(7))

    jr = lax.broadcasted_iota(jnp.int32, (LANES, LANES), 0)
    jc = lax.broadcasted_iota(jnp.int32, (LANES, LANES), 1)
    ones_bd = ((jr // HEAD) == (jc // HEAD)).astype(F32)
    mean_bd = ones_bd * (1.0 / HEAD)

    def per_pair(fn, *xs):
        return jnp.concatenate([fn(*(x[:, pp * LANES:(pp + 1) * LANES] for x in xs)) for pp in range(lp)], axis=-1)

    k = k_ref[0].astype(F32)
    xw = w0 + _mm(jnp.tanh(lw_ref[0]), wup_ref[0], NN, 3)
    ld_scr[...] = -jnp.exp(-_softplus(-xw) - 0.5)
    a = jax.nn.sigmoid(a0 + _mm(la_ref[0], wup_ref[1], NN, 3))
    a_scr[...] = a
    kkr = k * kk_w
    ssq = per_pair(lambda x: _mm(x * x, ones_bd, NN, STAT_PASSES), kkr)
    kk_scr[...] = kkr / jnp.maximum(jnp.sqrt(ssq), 1e-12)
    k2_scr[...] = k * (1.0 + (a - 1.0) * ka_w)

    rr = lax.broadcasted_iota(jnp.int32, (n, n), 0)
    cc = lax.broadcasted_iota(jnp.int32, (n, n), 1)
    strict = cc < rr
    incl = cc <= rr
    eye = (cc == rr).astype(F32)
    tri = _tri(C)
    nsteps = int(math.log2(C)) - 1

    def stage_b(gi, carry):
        units = []
        for cu in range(unroll):
            ci = gi * unroll + cu
            rows = pl.ds(pl.multiple_of(ci * C, C), C)
            ld = ld_scr[rows, :]
            b = _mm_exact_lhs(tri, ld)
            eb = jnp.exp(b)
            enb = jnp.exp(-b)
            kk = kk_scr[rows, :]
            al = -kk * jnp.exp(b - ld)
            be = kk * a_scr[rows, :] * enb
            kt = k2_scr[rows, :] * enb
            rb = r_ref[0, rows, :].astype(F32) * eb
            v = v_ref[0, rows, :].astype(F32)
            e_c = eb[C - 1:C, :]
            ec_scr[ci] = jnp.broadcast_to(e_c, (8, e_c.shape[1]))
            for pp in range(lp):
                ls = slice(pp * LANES, (pp + 1) * LANES)
                cat = lambda x: jnp.concatenate([x[:, ls] * lo, x[:, ls] * hi], axis=0)
                units.append(dict(idx=ci * lp + pp, ecp=e_c[:, ls], la=cat(al), lr=cat(rb), rb=cat(be),
                                  rk=cat(kt), vb=cat(v)))
        nu = range(len(units))
        if n % LANES == 0:
            prod = [_mm(jnp.concatenate([u["la"], u["lr"]], axis=0),
                        jnp.concatenate([u["rb"], u["rk"]], axis=0), NT) for u in units]
            m_ab = [jnp.where(strict, x[0:n, 0:n], 0.0) for x in prod]
            m_ak = [jnp.where(strict, x[0:n, n:2 * n], 0.0) for x in prod]
            m_rb = [jnp.where(incl, x[n:2 * n, 0:n], 0.0) for x in prod]
            m_rk = [jnp.where(incl, x[n:2 * n, n:2 * n], 0.0) for x in prod]
            akv = [_mm(m_ak[i], units[i]["vb"]) for i in nu]
            tinv = [eye + a_ for a_ in m_ab]
            apow = [_mm(a_, a_) for a_ in m_ab]
            for step in range(nsteps):
                if step < nsteps - 1:
                    both = [_mm(apow[i], jnp.concatenate([tinv[i], apow[i]], axis=1)) for i in nu]
                    tinv = [tinv[i] + both[i][:, 0:n] for i in nu]
                    apow = [x[:, n:2 * n] for x in both]
                else:
                    tinv = [tinv[i] + _mm(apow[i], tinv[i]) for i in nu]
            wu = [_mm(tinv[i], jnp.concatenate([units[i]["la"], akv[i]], axis=1)) for i in nu]
            w = [x[:, 0:LANES] for x in wu]
            uv = [jnp.concatenate([wu[i][:, LANES:2 * LANES], units[i]["vb"]], axis=0) for i in nu]
            rbe = [u["rb"] * u["ecp"] for u in units]
            for i, u in enumerate(units):
                q_scr[u["idx"]] = u["lr"] + _mm(m_rb[i], w[i])
            for i, u in enumerate(units):
                o0_scr[u["idx"]] = _mm(jnp.concatenate([m_rb[i], m_rk[i]], axis=1), uv[i])
            for i, u in enumerate(units):
                gp_scr[u["idx"]] = _mm(w[i], rbe[i], TN)
            for i, u in enumerate(units):
                z_scr[u["idx"]] = _mm(uv[i], jnp.concatenate([rbe[i], u["rk"] * u["ecp"]], axis=0), TN)
            return carry
        each = lambda fn: [fn(u) for u in units]
        m_ab = each(lambda u: jnp.where(strict, _mm(u["la"], u["rb"], NT), 0.0))
        m_ak = each(lambda u: jnp.where(strict, _mm(u["la"], u["rk"], NT), 0.0))
        m_rb = each(lambda u: jnp.where(incl, _mm(u["lr"], u["rb"], NT), 0.0))
        m_rk = each(lambda u: jnp.where(incl, _mm(u["lr"], u["rk"], NT), 0.0))
        akv = [_mm(m, u["vb"]) for m, u in zip(m_ak, units)]
        rkv = [_mm(m, u["vb"]) for m, u in zip(m_rk, units)]
        apow = m_ab
        tinv = [eye + a_ for a_ in apow]
        for _ in range(nsteps):
            apow = [_mm(a_, a_) for a_ in apow]
            tinv = [t_ + _mm(a_, t_) for a_, t_ in zip(apow, tinv)]
        w = [_mm(t_, u["la"]) for t_, u in zip(tinv, units)]
        u0 = [_mm(t_, x_) for t_, x_ in zip(tinv, akv)]
        rbe = [u["rb"] * u["ecp"] for u in units]
        for i, u in enumerate(units):
            q_scr[u["idx"]] = u["lr"] + _mm(m_rb[i], w[i])
        for i, u in enumerate(units):
            o0_scr[u["idx"]] = _mm(m_rb[i], u0[i]) + rkv[i]
        for i, u in enumerate(units):
            gp_scr[u["idx"]] = _mm(w[i], rbe[i], TN)
        for i, u in enumerate(units):
            z_scr[u["idx"]] = _mm(u0[i], rbe[i], TN) + _mm(u["vb"], u["rk"] * u["ecp"], TN)
        return carry

    lax.fori_loop(0, nchunks // unroll, stage_b, 0)

    for ci in range(nchunks):
        for pp in range(lp):
            ls = slice(pp * LANES, (pp + 1) * LANES)
            idx = ci * lp + pp
            sidx = idx if nseq > 1 else pp
            s = s_ref[sidx]
            o_bd = _mm(q_scr[idx], s, NT) + o0_scr[idx]
            o_scr[ci * C:(ci + 1) * C, ls] = o_bd[0:C] + o_bd[C:n]
            s_ref[sidx] = s * ec_scr[ci, 0:1, ls] + _mm(s, gp_scr[idx]) + z_scr[idx]

    o = o_scr[...]
    v = v_ref[0].astype(F32)
    mean = per_pair(lambda x: _mm(x, mean_bd, NN, STAT_PASSES), o)
    dlt = o - mean
    var = per_pair(lambda x: _mm(x, mean_bd, NN, STAT_PASSES), dlt * dlt)
    on = dlt * lax.rsqrt(var + B_LN_EPS) * ln_g + ln_b
    bonus = per_pair(lambda x: _mm(x, ones_bd, NN, STAT_PASSES), r_ref[0].astype(F32) * k2_scr[...] * rk_w)
    on = on + bonus * v
    og_ref[...] = (on * _silu(g_ref[0].astype(F32))).astype(og_ref.dtype)

    @pl.when(t == nt - 1)
    def _():
        for si in range(nseq):
            for pp in range(lp):
                sout_ref[si, 2 * pp] = s_ref[si * lp + pp, 0:HEAD, 0:HEAD]
                sout_ref[si, 2 * pp + 1] = s_ref[si * lp + pp, HEAD:LANES, HEAD:LANES]


def _rwkv_mix(rkvg, lora, wup, vec, s0, bn, tn_, C, tb, lp, unroll, nseq, og_dtype):
    m = rkvg.shape[1]
    nch = rkvg.shape[2]
    wl = lp * LANES
    ngrp = nch // wl
    nt = tn_ * nseq // tb
    nchunks = tb // C
    rank = lora.shape[2]
    nh = s0.shape[1]
    row = lambda b, t: b * nt + t

    def xspec(c):
        return pl.BlockSpec((1, tb, wl), lambda b, p, t: (c, row(b, t), p))

    def lspec(c):
        return pl.BlockSpec((1, tb, rank), lambda b, p, t: (c, row(b, t), 0))

    og, sout = pl.pallas_call(
        functools.partial(_rwkv_kernel, C=C, tb=tb, lp=lp, nt=nt, unroll=unroll, nseq=nseq),
        grid=(bn // nseq, ngrp, nt),
        in_specs=[
            xspec(0), xspec(1), xspec(2), xspec(3), lspec(0), lspec(1),
            pl.BlockSpec((2, rank, wl), lambda b, p, t: (0, 0, p)),
            pl.BlockSpec((8, wl), lambda b, p, t: (0, p)),
            pl.BlockSpec((nseq, 2 * lp, HEAD, HEAD), lambda b, p, t: (b, p, 0, 0)),
        ],
        out_specs=[
            pl.BlockSpec((tb, wl), lambda b, p, t: (row(b, t), p)),
            pl.BlockSpec((nseq, 2 * lp, HEAD, HEAD), lambda b, p, t: (b, p, 0, 0)),
        ],
        out_shape=[
            jax.ShapeDtypeStruct((m, nch), og_dtype),
            jax.ShapeDtypeStruct((bn, nh, HEAD, HEAD), F32),
        ],
        scratch_shapes=[
            pltpu.VMEM((nseq * lp, LANES, LANES), F32),
            pltpu.VMEM((tb, wl), F32), pltpu.VMEM((tb, wl), F32), pltpu.VMEM((tb, wl), F32), pltpu.VMEM((tb, wl), F32),
            pltpu.VMEM((nchunks * lp, 2 * C, LANES), F32), pltpu.VMEM((nchunks * lp, 2 * C, LANES), F32),
            pltpu.VMEM((nchunks * lp, LANES, LANES), F32), pltpu.VMEM((nchunks * lp, LANES, LANES), F32),
            pltpu.VMEM((nchunks, 8, wl), F32),
            pltpu.VMEM((tb, wl), F32),
        ],
        compiler_params=_params(("parallel", "parallel", "arbitrary")),
        name="rwkv7_chunked",
    )(rkvg, rkvg, rkvg, rkvg, lora, lora, wup, vec, s0)
    return og, sout


def _rwkv_layer(x2d, bn, tn_, g, shift, s0, mu, w_rkvg, w_down, w_up, w0, a0, k_k, k_a, r_k, ln_g, ln_b,
                w_out, tm, tn_cap, C, tb, lp, unroll, nseq, og_dtype):
    m, d = x2d.shape
    nblk = m // tm
    if tm <= tn_:
        starts = jnp.arange(nblk) * tm
        prev = _rms_rows(x2d[jnp.maximum(starts - 1, 0)], g)
        first = jnp.where((starts % tn_ == 0)[:, None], shift[starts // tn_], prev).reshape(nblk, 1, d)
        period = None
    else:
        assert nblk == 1
        first = jnp.zeros((bn, tn_, d), F32).at[:, 0].set(shift).reshape(1, m, d)
        period = tn_
    shift_out = _rms_rows(x2d[tn_ - 1::tn_], g)
    nch = w_rkvg.shape[2]
    rkvg = _proj(x2d, g, w_rkvg, 0, nch, tm, _pick_tn(nch, tn_cap), og_dtype, first=first, mu=mu[:4], period=period)
    rank = w_down.shape[2]
    lora = _proj(x2d, g, w_down, 0, rank, tm, rank, F32, first=first, mu=mu[4:6], period=period)
    vec =jnp.stack([w0, a0, k_k, k_a, r_k.reshape(nch), ln_g, ln_b, jnp.zeros((nch,), F32)])
    og, sout = _rwkv_mix(rkvg, lora, w_up, vec, s0, bn, tn_, C, tb, lp, unroll, nseq, og_dtype)
    y = _outproj(og, w_out, x2d, tm, 512)
    return y, sout, shift_out


GLA_SUB = 16


def _gla_kernel(q_ref, k_ref, v_ref, gate_ref, gl_ref, wup_ref, bgk_ref, og_w_ref, s0_ref,
                og_ref, sout_ref, st_ref, z_scr, qd_scr, o_scr, ec_scr, *, C, tb, nt, nseq):
    t = pl.program_id(2)
    dk = q_ref.shape[-1]
    nchunks = tb // C
    sub = min(GLA_SUB, C)
    nsub = C // sub
    qscale = dk ** -0.5

    @pl.when(t == 0)
    def _():
        for si in range(nseq):
            st_ref[si] = s0_ref[si, 0].T

    tri = _tri(C)
    glog_all = -_softplus(-(_mm(gl_ref[...], wup_ref[...], NN, 3) + bgk_ref[...])) * (1.0 / C_GATE_NORM)
    chunks = range(nchunks)
    rows = [slice(c * C, (c + 1) * C) for c in chunks]
    glog = [glog_all[rs] for rs in rows]
    b = [_mm_exact_lhs(tri, gl_) for gl_ in glog]
    q = [q_ref[rs, :].astype(F32) * qscale for rs in rows]
    k = [k_ref[rs, :].astype(F32) for rs in rows]
    v = [v_ref[rs, :].astype(F32) for rs in rows]
    for c in chunks:
        bl = b[c][C - 1:C, :]
        qd_scr[rows[c], :] = q[c] * jnp.exp(b[c])
        ec_scr[c] = jnp.broadcast_to(jnp.exp(bl), (8, dk))
        z_scr[c] = _mm(v[c], k[c] * jnp.exp(bl - b[c]), TN)
    parts = [[] for _ in chunks]
    for i in range(nsub):
        r0 = i * sub
        nk = r0 + sub
        ar = lax.broadcasted_iota(jnp.int32, (sub, nk), 0) + r0
        ac = lax.broadcasted_iota(jnp.int32, (sub, nk), 1)
        att = []
        for c in chunks:
            ref = b[c][r0:r0 + 1, :] - glog[c][r0:r0 + 1, :]
            qi = q[c][r0:nk] * jnp.exp(b[c][r0:nk] - ref)
            ki = k[c][0:nk] * jnp.exp(ref - b[c][0:nk])
            att.append(jnp.where(ac <= ar, _mm(qi, ki, NT), 0.0))
        for c in chunks:
            parts[c].append(_mm(att[c], v[c][0:nk]))
    for c in chunks:
        o_scr[rows[c], :] = jnp.concatenate(parts[c], axis=0) if nsub > 1 else parts[c][0]

    for c in chunks:
        si = c if nseq > 1 else 0
        st = st_ref[si]
        o_scr[rows[c], :] = o_scr[rows[c], :] + _mm(qd_scr[rows[c], :], st, NT)
        st_ref[si] = st * ec_scr[c, 0:1, :] + z_scr[c]

    o = o_scr[...]
    on = o * lax.rsqrt(jnp.mean(o * o, axis=-1, keepdims=True) + NORM_EPS) * og_w_ref[...]
    og_ref[...] = (on * _silu(gate_ref[...].astype(F32))).astype(og_ref.dtype)

    @pl.when(t == nt - 1)
    def _():
        for si in range(nseq):
            sout_ref[si, 0] = st_ref[si].T


def _gla_mix(qkvg, gl, w_up, b_gk, o_g, s0, bn, tn_, C, tb, nseq, og_dtype):
    m = qkvg.shape[0]
    nh, dk, dv = s0.shape[1:]
    nt = tn_ * nseq // tb
    rank = gl.shape[1]
    row = lambda b, t: b * nt + t
    kcol = nh
    vcol = 2 * nh * dk // dv
    gcol = vcol + nh
    og, sout = pl.pallas_call(
        functools.partial(_gla_kernel, C=C, tb=tb, nt=nt, nseq=nseq),
        grid=(bn // nseq, nh, nt),
        in_specs=[
            pl.BlockSpec((tb, dk), lambda b, h, t: (row(b, t), h)),
            pl.BlockSpec((tb, dk), lambda b, h, t: (row(b, t), kcol + h)),
            pl.BlockSpec((tb, dv), lambda b, h, t: (row(b, t), vcol + h)),
            pl.BlockSpec((tb, dv), lambda b, h, t: (row(b, t), gcol + h)),
            pl.BlockSpec((tb, rank), lambda b, h, t: (row(b, t), 0)),
            pl.BlockSpec((rank, dk), lambda b, h, t: (0, h)),
            pl.BlockSpec((1, dk), lambda b, h, t: (0, h)),
            pl.BlockSpec((1, dv), lambda b, h, t: (0, 0)),
            pl.BlockSpec((nseq, 1, dk, dv), lambda b, h, t: (b, h, 0, 0)),
        ],
        out_specs=[
            pl.BlockSpec((tb, dv), lambda b, h, t: (row(b, t), h)),
            pl.BlockSpec((nseq, 1, dk, dv), lambda b, h, t: (b, h, 0, 0)),
        ],
        out_shape=[
            jax.ShapeDtypeStruct((m, nh * dv), og_dtype),
            jax.ShapeDtypeStruct((bn, nh, dk, dv), F32),
        ],
        scratch_shapes=[
            pltpu.VMEM((nseq, dv, dk), F32),
            pltpu.VMEM((tb // C, dv, dk), F32),
            pltpu.VMEM((tb, dk), F32),
            pltpu.VMEM((tb, dv), F32),
            pltpu.VMEM((tb // C, 8, dk), F32),
        ],
        compiler_params=_params(("parallel", "parallel", "arbitrary")),
        name="gla_chunked",
    )(qkvg, qkvg, qkvg, qkvg, gl, w_up, b_gk.reshape(1, -1), o_g.reshape(1, -1), s0)
    return og, sout


def _gla_layer(x2d, bn, tn_, g, s0, w3, w_up, b_gk, o_g, w_out, tm, tn_cap, C, tb, nseq, og_dtype):
    nh, dk, dv = s0.shape[1:]
    nfused = 2 * nh * dk + 2 * nh * dv
    rank = w_up.shape[0]
    qkvg = _proj(x2d, g, w3, 0, nfused, tm, _pick_tn(nfused, tn_cap), og_dtype)[0]
    gl = _proj(x2d, g, w3[:, :, nfused:], 0, rank, tm, rank, F32)[0]
    og, sout = _gla_mix(qkvg, gl, w_up, b_gk, o_g, s0, bn, tn_, C, tb, nseq, og_dtype)
    y = _outproj(og, w_out, x2d, tm, 512)
    return y, sout


PROMPT_TM = 1024
RWKV_CHUNK = 64
RWKV_TB = 512
RWKV_PAIRS = 4
RWKV_UNROLL = 4
STAT_PASSES = 1
GLA_CHUNK = 64
GLA_TB = 512
PROMPT_TN = 1024
LERP_TN = 512
SAMPLE_TN = 2048
SAMPLE_SEQS = 4


def kernel(x_prompt, x_sample, cache_k_win, cache_v_win, state_wkv, state_shift, state_gla, norm_g, rel_bias, w_in_a, q_norm_g, k_norm_g, sinks, w_out_a, mu_b, w_rkvg_b, w_lora_down_b, w_lora_up_b, w0_b, a0_b, k_k_b, k_a_b, r_k_b, ln_x_g_b, ln_x_b_b, w_out_b, w_in_c, w_gk_up_c, b_gk_c, o_norm_g_c, w_out_c):
    bp, tp, d = x_prompt.shape
    bs, ts, _ = x_sample.shape
    depth = norm_g.shape[0]
    bf = lambda w: w.astype(BF16)
    xp = x_prompt.reshape(bp * tp, d)
    xs = x_sample.reshape(bs * ts, d)
    ms = bs * ts
    kwp, vwp, kws, vws, wkvp, shp, wkvs, shs, glap, glas = ([] for _ in range(10))
    for layer in range(depth):
        kind, j = layer % 3, layer // 3
        g = norm_g[layer]
        if kind == 0:
            wa = (bf(w_in_a[j])[None], q_norm_g[j], k_norm_g[j], sinks[j], bf(w_out_a[j]), rel_bias)
            xp, kp_, vp_ = _attn_layer(xp, bp, tp, g, *wa, None, None, PROMPT_TM, PROMPT_TN, BF16)
            xs, ks_, vs_ = _attn_layer(xs, bs, ts, g, *wa, cache_k_win[j], cache_v_win[j], ms, SAMPLE_TN, F32)
            kwp.append(kp_); vwp.append(vp_); kws.append(ks_); vws.append(vs_)
        elif kind == 1:
            wb = (mu_b[j], bf(w_rkvg_b[j]), bf(w_lora_down_b[j]), w_lora_up_b[j], w0_b[j], a0_b[j], k_k_b[j], k_a_b[j],
                  r_k_b[j], ln_x_g_b[j], ln_x_b_b[j], bf(w_out_b[j]))
            nh = w_rkvg_b.shape[3] // HEAD
            xp, sp_, lp_ = _rwkv_layer(xp, bp, tp, g, jnp.zeros((bp, d), F32), jnp.zeros((bp, nh, HEAD, HEAD), F32),
                                       *wb, PROMPT_TM, LERP_TN, RWKV_CHUNK, RWKV_TB, RWKV_PAIRS, RWKV_UNROLL, 1, BF16)
            xs, ss_, ls_ = _rwkv_layer(xs, bs, ts, g, state_shift[j], state_wkv[j], *wb, ms, SAMPLE_TN, ts,
                                       ts * SAMPLE_SEQS, RWKV_PAIRS, SAMPLE_SEQS, SAMPLE_SEQS, F32)
            wkvp.append(sp_); shp.append(lp_); wkvs.append(ss_); shs.append(ls_)
        else:
            wc = (bf(w_in_c[j])[None], w_gk_up_c[j], b_gk_c[j], o_norm_g_c[j], bf(w_out_c[j]))
            xp, sp_ = _gla_layer(xp, bp, tp, g, jnp.zeros((bp,) + state_gla.shape[2:], F32), *wc,
                                 PROMPT_TM, PROMPT_TN, GLA_CHUNK, GLA_TB, 1, BF16)
            xs, ss_ = _gla_layer(xs, bs, ts, g, state_gla[j], *wc, ms, SAMPLE_TN, ts, ts * SAMPLE_SEQS, SAMPLE_SEQS, F32)
            glap.append(sp_); glas.append(ss_)
    return (xp.reshape(bp, tp, d), xs.reshape(bs, ts, d),
            jnp.stack(kwp), jnp.stack(vwp), jnp.stack(kws), jnp.stack(vws),
            jnp.stack(wkvp), jnp.stack(shp), jnp.stack(wkvs), jnp.stack(shs),
            jnp.stack(glap), jnp.stack(glas))
```

```python
import functools
import math

import jax
import jax.numpy as jnp
from jax import lax
from jax.experimental import pallas as pl
from jax.experimental.pallas import tpu as pltpu

F32 = jnp.float32
BF16 = jnp.bfloat16

NORM_EPS = 1e-6
HEAD = 64
LANES = 128
WINDOW = 128
N_BUCKETS = 32
MAX_EXACT = N_BUCKETS // 2
MAX_DISTANCE = 128
B_LN_EPS = 64e-5
C_GATE_NORM = 16.0
VMEM_LIMIT = 56 * 1024 * 1024

NN = (((1,), (0,)), ((), ()))
NT = (((1,), (1,)), ((), ()))
TN = (((0,), (0,)), ((), ()))


def _params(sem):
    return pltpu.CompilerParams(dimension_semantics=sem, vmem_limit_bytes=VMEM_LIMIT)


def _split(x):
    hi = x.astype(BF16)
    lo = (x - hi.astype(F32)).astype(BF16)
    return hi, lo


def _mm(a, b, dims=NN, passes=1):
    if passes == 1:
        return lax.dot_general(a.astype(BF16), b.astype(BF16), dims, preferred_element_type=F32)
    a_hi, a_lo = _split(a)
    b_hi, b_lo = _split(b)
    dg = functools.partial(lax.dot_general, dimension_numbers=dims, preferred_element_type=F32)
    return dg(a_hi, b_hi) + (dg(a_hi, b_lo) + dg(a_lo, b_hi))


def _mm_exact_lhs(a_bf16, b):
    b0 = b.astype(BF16)
    r1 = b - b0.astype(F32)
    b1 = r1.astype(BF16)
    b2 = (r1 - b1.astype(F32)).astype(BF16)
    dg = functools.partial(lax.dot_general, dimension_numbers=NN, preferred_element_type=F32)
    return dg(a_bf16, b0) + (dg(a_bf16, b1) + dg(a_bf16, b2))


def _softplus(z):
    return jnp.maximum(z, 0.0) + jnp.log1p(jnp.exp(-jnp.abs(z)))


def _silu(g):
    return g * jax.nn.sigmoid(g)


def _tri(c):
    r = lax.broadcasted_iota(jnp.int32, (c, c), 0)
    col = lax.broadcasted_iota(jnp.int32, (c, c), 1)
    return (col <= r).astype(BF16)


def _rms_rows_kernel(x_ref, g_ref, o_ref):
    x = x_ref[...]
    o_ref[...] = x * lax.rsqrt(jnp.mean(x * x, axis=-1, keepdims=True) + NORM_EPS) * g_ref[...]


def _rms_rows(rows, g):
    n, d = rows.shape
    npad = -(-n // 8) * 8
    rows_p = jnp.pad(rows, ((0, npad - n), (0, 0)))
    out = pl.pallas_call(
        _rms_rows_kernel,
        out_shape=jax.ShapeDtypeStruct((npad, d), F32),
        name="rms_rows",
    )(rows_p, g.reshape(1, d))
    return out[:n]


PROLOGUE_ROWS = 256


def _proj_kernel(*refs, lerp, period, tm):
    if lerp:
        x_ref, g_ref, w_ref, first_ref, mu_ref, o_ref, xm_ref, h_ref, d_ref = refs
    else:
        x_ref, g_ref, w_ref, o_ref, xm_ref = refs
    c = pl.program_id(1)
    j = pl.program_id(2)
    rc = min(PROLOGUE_ROWS, tm)

    def normed(c0):
        x = x_ref[c0:c0 + rc, :]
        return x * lax.rsqrt(jnp.mean(x * x, axis=-1, keepdims=True) + NORM_EPS) * g_ref[...]

    if lerp:
        @pl.when((c == 0) & (j == 0))
        def _():
            carry = first_ref[0, 0:1, :]
            for c0 in range(0, tm, rc):
                h = normed(c0)
                row = lax.broadcasted_iota(jnp.int32, h.shape, 0)
                hs = jnp.where(row == 0, carry, pltpu.roll(h, 1, 0))
                if period is not None:
                    hs = jnp.where(row % period == 0, first_ref[0, c0:c0 + rc, :], hs)
                carry = h[rc - 1:rc, :]
                h_ref[c0:c0 + rc, :] = h
                d_ref[c0:c0 + rc, :] = hs - h

        @pl.when(j == 0)
        def _():
            for c0 in range(0, tm, rc):
                xm_ref[c0:c0 + rc, :] = (h_ref[c0:c0 + rc, :] + d_ref[c0:c0 + rc, :] * mu_ref[0]).astype(BF16)
    else:
        @pl.when(j == 0)
        def _():
            for c0 in range(0, tm, rc):
                xm_ref[c0:c0 + rc, :] = normed(c0).astype(BF16)

    o_ref[0] = jnp.dot(xm_ref[...], w_ref[0], preferred_element_type=F32).astype(o_ref.dtype)


def _pick_tn(n, cap):
    if n < LANES:
        return n
    units = n // LANES
    return LANES * max(u for u in range(1, units + 1) if units % u == 0 and u * LANES <= cap)


def _proj(x2d, g, w3, col_off, n_out, tm, tn, out_dtype, first=None, mu=None, period=None, w_base=0, nc=1):
    m, d = x2d.shape
    lerp = first is not None
    assert m % tm == 0 and n_out % tn == 0 and col_off % tn == 0
    joff = col_off // tn
    in_specs = [
        pl.BlockSpec((tm, d), lambda i, c, j: (i, 0)),
        pl.BlockSpec((1, d), lambda i, c, j: (0, 0)),
        pl.BlockSpec((1, d, tn), lambda i, c, j: (w_base + c, 0, joff + j)),
    ]
    args = [x2d, g.reshape(1, d), w3]
    if lerp:
        fr = first.shape[1]
        in_specs += [
            pl.BlockSpec((1, fr, d), lambda i, c, j: (i, 0, 0)),
            pl.BlockSpec((1, 1, d), lambda i, c, j: (c, 0, 0)),
        ]
        args += [first, mu.reshape(nc, 1, d)]
    return pl.pallas_call(
        functools.partial(_proj_kernel, lerp=lerp, period=period, tm=tm),
        grid=(m // tm, nc, n_out // tn),
        in_specs=in_specs,
        out_specs=pl.BlockSpec((1, tm, tn), lambda i, c, j: (c, i, j)),
        out_shape=jax.ShapeDtypeStruct((nc, m, n_out), out_dtype),
        scratch_shapes=[pltpu.VMEM((tm, d), BF16)] + ([pltpu.VMEM((tm, d), F32)] * 2 if lerp else []),
        compiler_params=_params(("parallel", "arbitrary", "arbitrary")),
        name="proj",
    )(*args)


def _outproj_kernel(a_ref, w_ref, x_ref, o_ref):
    o_ref[...] = x_ref[...] + jnp.dot(a_ref[...].astype(BF16), w_ref[0], preferred_element_type=F32)


def _outproj(a2d, w3, w_base, x2d, tm, tn):
    m, kdim = a2d.shape
    n = w3.shape[2]
    return pl.pallas_call(
        _outproj_kernel,
        grid=(m // tm, n // tn),
        in_specs=[
            pl.BlockSpec((tm, kdim), lambda i, j: (i, 0)),
            pl.BlockSpec((1, kdim, tn), lambda i, j: (w_base, 0, j)),
            pl.BlockSpec((tm, tn), lambda i, j: (i, j)),
        ],
        out_specs=pl.BlockSpec((tm, tn), lambda i, j: (i, j)),
        out_shape=jax.ShapeDtypeStruct((m, n), F32),
        compiler_params=_params(("parallel", "arbitrary")),
        name="outproj",
    )(a2d, w3, x2d)


A_KV_HEADS = 8
A_GROUP = 8


def _attn_kernel(*refs, tq, prompt, nblocks):
    q_ref, kc_ref, vc_ref, kp_ref, vp_ref = refs[:5]
    gate_refs = refs[5:5 + A_KV_HEADS]
    bias_ref, qg_ref, kg_ref, og_ref, kwin_ref, vwin_ref, k_scr, v_scr, s_scr, p_scr = refs[5 + A_KV_HEADS:]
    return _attn_body(q_ref, kc_ref, vc_ref, kp_ref, vp_ref, gate_refs, bias_ref, qg_ref, kg_ref,
                      og_ref, kwin_ref, vwin_ref, k_scr, v_scr, s_scr, p_scr, tq=tq, prompt=prompt, nblocks=nblocks)


def _attn_body(q_ref, kc_ref, vc_ref, kp_ref, vp_ref, gate_refs, bias_ref, qg_ref, kg_ref,
               og_ref, kwin_ref, vwin_ref, k_scr, v_scr, s_scr, p_scr, *, tq, prompt, nblocks):
    i = pl.program_id(1)
    tk = WINDOW + tq
    kg = kg_ref[...]
    qg = qg_ref[...] * (HEAD ** -0.5)
    nslab = kg.shape[1]
    jr = lax.broadcasted_iota(jnp.int32, (nslab, nslab), 0)
    jc = lax.broadcasted_iota(jnp.int32, (nslab, nslab), 1)
    mean_bd = jnp.where((jr // HEAD) == (jc // HEAD), 1.0 / HEAD, 0.0).astype(BF16)
    ones_v = jnp.ones((tk, HEAD), BF16)

    def slab_norm(x, gain):
        ms = jnp.dot((x * x).astype(BF16), mean_bd, preferred_element_type=F32)
        return x * lax.rsqrt(ms + NORM_EPS) * gain

    if prompt:
        k_scr[0:WINDOW, :] = slab_norm(kp_ref[...].astype(F32), kg)
        v_scr[0:WINDOW, :] = vp_ref[...].astype(F32)
    else:
        for kh in range(A_KV_HEADS):
            k_scr[0:WINDOW, kh * HEAD:(kh + 1) * HEAD] = kp_ref[0, :, kh, :]
            v_scr[0:WINDOW, kh * HEAD:(kh + 1) * HEAD] = vp_ref[0, :, kh, :]
    k_scr[WINDOW:tk, :] = slab_norm(kc_ref[...].astype(F32), kg)
    v_scr[WINDOW:tk, :] = vc_ref[...].astype(F32)

    @pl.when(i == nblocks - 1)
    def _():
        keep = 0 if prompt else WINDOW - tq
        if keep:
            kwin_ref[0, 0:keep] = kp_ref[0, tq:WINDOW]
            vwin_ref[0, 0:keep] = vp_ref[0, tq:WINDOW]
        for kh in range(A_KV_HEADS):
            sl = slice(kh * HEAD, (kh + 1) * HEAD)
            kwin_ref[0, keep:WINDOW, kh, :] = k_scr[tk - WINDOW + keep:tk, sl]
            vwin_ref[0, keep:WINDOW, kh, :] = v_scr[tk - WINDOW + keep:tk, sl]

    k_scr[0:1, :] = jnp.zeros((1, k_scr.shape[1]), F32)
    v_scr[0:1, :] = jnp.zeros((1, v_scr.shape[1]), F32)

    c = lax.broadcasted_iota(jnp.int32, (tq, tk), 1)
    no_prev = (c >= WINDOW) | (c == 0)
    lo_half = lax.broadcasted_iota(jnp.int32, (tq, LANES), 1) < HEAD

    def group(kh, first_block):
        kk = k_scr[:, kh * HEAD:(kh + 1) * HEAD].astype(BF16)
        vv = v_scr[:, kh * HEAD:(kh + 1) * HEAD].astype(BF16)
        h0 = kh * A_GROUP
        lanes = slice(h0 * HEAD, (h0 + A_GROUP) * HEAD)
        qn = slab_norm(q_ref[:, lanes].astype(F32), qg)
        qs = jnp.concatenate([qn[:, e * HEAD:(e + 1) * HEAD] for e in range(A_GROUP)], axis=0)
        s_scr[...] = lax.dot_general(qs.astype(BF16), kk, NT, preferred_element_type=F32)
        for e in range(A_GROUP):
            rs = slice(e * tq, (e + 1) * tq)
            s = s_scr[rs, :] + bias_ref[h0 + e]
            if first_block:
                s = jnp.where(no_prev, s, -jnp.inf)
            p_scr[rs, :] = jnp.exp(s - jnp.max(s, axis=-1, keepdims=True)).astype(p_scr.dtype)
        x = jnp.dot(p_scr[...].astype(BF16), jnp.concatenate([vv, ones_v, ones_v, vv], axis=-1),
                    preferred_element_type=F32)
        pairs = []
        for e in range(0, A_GROUP, 2):
            xe, xo = x[e * tq:(e + 1) * tq], x[(e + 1) * tq:(e + 2) * tq]
            num = jnp.where(lo_half, xe[:, 0:LANES], xo[:, LANES:2 * LANES])
            den = jnp.where(lo_half, xe[:, LANES:2 * LANES], xo[:, 0:LANES])
            pairs.append(num / den)
        o = jnp.concatenate(pairs, axis=-1)
        og_ref[:, lanes] = (o * _silu(gate_refs[kh][...].astype(F32))).astype(og_ref.dtype)

    if prompt:
        @pl.when(i == 0)
        def _():
            for kh in range(A_KV_HEADS):
                group(kh, True)

        @pl.when(i > 0)
        def _():
            for kh in range(A_KV_HEADS):
                group(kh, False)
    else:
        for kh in range(A_KV_HEADS):
            group(kh, False)


def _t5_bucket(dist):
    d = jnp.maximum(dist, 0)
    large = MAX_EXACT + (jnp.log(jnp.maximum(d, 1).astype(F32) / MAX_EXACT)
                         / math.log(MAX_DISTANCE / MAX_EXACT) * (N_BUCKETS - MAX_EXACT)).astype(jnp.int32)
    return jnp.where(d < MAX_EXACT, d, jnp.minimum(large, N_BUCKETS - 1))


def _attn_mix(qkvg, nq, rel_bias, q_g, k_g, sinks, bn, tn_, cache_k, cache_v, og_dtype):
    m = qkvg.shape[0]
    prompt = cache_k is None
    tq = WINDOW if prompt else tn_
    nb = tn_ // tq
    tk = WINDOW + tq
    col = jnp.arange(tk)[None, :]
    dist = WINDOW + jnp.arange(tq)[:, None] - col
    onehot = (_t5_bucket(dist)[None] == jnp.arange(N_BUCKETS)[:, None, None]).astype(F32)
    bias = jnp.einsum("bh,bqk->hqk", rel_bias.astype(F32), onehot, precision=lax.Precision.HIGHEST)
    bias = jnp.where(((dist >= 0) & (dist < WINDOW))[None], bias, -jnp.inf)
    bias = jnp.where((col == 0)[None], sinks.astype(F32)[:, None, None], bias)
    nkv = A_KV_HEADS * HEAD
    row = lambda b, i: b * nb + i
    kcol = nq // nkv
    gcol = kcol + 2
    win_shape = (1, WINDOW, A_KV_HEADS, HEAD)
    if prompt:
        kp_arr, vp_arr = qkvg, qkvg
        kp_spec = pl.BlockSpec((WINDOW, nkv), lambda b, i: (jnp.maximum(row(b, i) - 1, 0), kcol))
        vp_spec = pl.BlockSpec((WINDOW, nkv), lambda b, i: (jnp.maximum(row(b, i) - 1, 0), kcol + 1))
    else:
        kp_arr, vp_arr = cache_k, cache_v
        kp_spec = pl.BlockSpec(win_shape, lambda b, i: (b, 0, 0, 0))
        vp_spec = pl.BlockSpec(win_shape, lambda b, i: (b, 0, 0, 0))

    def gate_spec(kh):
        return pl.BlockSpec((tq, nkv), lambda b, i: (row(b, i), gcol + kh))

    og, kwin, vwin = pl.pallas_call(
        functools.partial(_attn_kernel, tq=tq, prompt=prompt, nblocks=nb),
        grid=(bn, nb),
        in_specs=[
            pl.BlockSpec((tq, nq), lambda b, i: (row(b, i), 0)),
            pl.BlockSpec((tq, nkv), lambda b, i: (row(b, i), kcol)),
            pl.BlockSpec((tq, nkv), lambda b, i: (row(b, i), kcol + 1)),
            kp_spec, vp_spec,
            *[gate_spec(kh) for kh in range(A_KV_HEADS)],
            pl.BlockSpec((nq // HEAD, tq, tk), lambda b, i: (0, 0, 0)),
            pl.BlockSpec((1, nkv), lambda b, i: (0, 0)),
            pl.BlockSpec((1, nkv), lambda b, i: (0, 0)),
        ],
        out_specs=[
            pl.BlockSpec((tq, nq), lambda b, i: (row(b, i), 0)),
            pl.BlockSpec(win_shape, lambda b, i: (b, 0, 0, 0)),
            pl.BlockSpec(win_shape, lambda b, i: (b, 0, 0, 0)),
        ],
        out_shape=[
            jax.ShapeDtypeStruct((m, nq), og_dtype),
            jax.ShapeDtypeStruct((bn,) + win_shape[1:], F32),
            jax.ShapeDtypeStruct((bn,) + win_shape[1:], F32),
        ],
        scratch_shapes=[
            pltpu.VMEM((tk, nkv), F32), pltpu.VMEM((tk, nkv), F32),
            pltpu.VMEM((A_GROUP * tq, tk), F32),
            pltpu.VMEM((A_GROUP * tq, tk), BF16 if tq % 16 == 0 else F32),
        ],
        compiler_params=_params(("parallel", "arbitrary")),
        name="swa_attention",
    )(qkvg, qkvg, qkvg, kp_arr, vp_arr, *([qkvg] * A_KV_HEADS), bias,
      jnp.tile(q_g.astype(F32), A_GROUP).reshape(1, nkv), jnp.tile(k_g.astype(F32), A_KV_HEADS).reshape(1, nkv))
    return og, kwin, vwin


def _attn_layer(x2d, bn, tn_, g, j, w3, q_g, k_g, sinks, w_out, rel_bias, cache_k, cache_v, tm, tn_cap, og_dtype):
    nq = w_out.shape[1]
    nall = w3.shape[2]
    qkvg = _proj(x2d, g, w3, 0, nall, tm, _pick_tn(nall, tn_cap), og_dtype, w_base=j)[0]
    og, kwin, vwin = _attn_mix(qkvg, nq, rel_bias, q_g, k_g, sinks, bn, tn_, cache_k, cache_v, og_dtype)
    y = _outproj(og, w_out, j, x2d, tm, 512)
    return y, kwin, vwin


def _rwkv_kernel(r_ref, k_ref, v_ref, g_ref, lw_ref, la_ref, wup_ref, vec_ref, s0_ref,
                 og_ref, sout_ref,
                 s_ref, kk_scr, k2_scr, a_scr, ld_scr, q_scr, o0_scr, gp_scr, z_scr, ec_scr, o_scr,
                 *, C, tb, lp, nt, unroll, nseq):
    t = pl.program_id(2)
    n = 2 * C
    nchunks = tb // C
    lane = lax.broadcasted_iota(jnp.int32, (1, LANES), 1)
    lo = (lane < HEAD).astype(F32)
    hi = 1.0 - lo

    @pl.when(t == 0)
    def _():
        z = jnp.zeros((HEAD, HEAD), F32)
        for si in range(nseq):
            for pp in range(lp):
                s_ref[si * lp + pp, 0:HEAD, :] = jnp.concatenate([s0_ref[si, 2 * pp], z], axis=-1)
                s_ref[si * lp + pp, HEAD:LANES, :] = jnp.concatenate([z, s0_ref[si, 2 * pp + 1]], axis=-1)

    vec = vec_ref[...]
    w0, a0, kk_w, ka_w, rk_w, ln_g, ln_b = (vec[i:i + 1, :] for i in range(7))

    jr = lax.broadcasted_iota(jnp.int32, (LANES, LANES), 0)
    jc = lax.broadcasted_iota(jnp.int32, (LANES, LANES), 1)
    ones_bd = ((jr // HEAD) == (jc // HEAD)).astype(F32)
    mean_bd = ones_bd * (1.0 / HEAD)

    def per_pair(fn, *xs):
        return jnp.concatenate([fn(*(x[:, pp * LANES:(pp + 1) * LANES] for x in xs)) for pp in range(lp)], axis=-1)

    k = k_ref[0].astype(F32)
    xw = w0 + _mm(jnp.tanh(lw_ref[0]), wup_ref[0], NN, 3)
    ld_scr[...] = -math.exp(-0.5) * jax.nn.sigmoid(xw)
    a = jax.nn.sigmoid(a0 + _mm(la_ref[0], wup_ref[1], NN, 3))
    a_scr[...] = a
    kkr = k * kk_w
    ssq = per_pair(lambda x: _mm(x * x, ones_bd, NN, STAT_PASSES), kkr)
    kk_scr[...] = kkr * lax.rsqrt(jnp.maximum(ssq, 1e-24))
    k2_scr[...] = k * (1.0 + (a - 1.0) * ka_w)

    rr = lax.broadcasted_iota(jnp.int32, (n, n), 0)
    cc = lax.broadcasted_iota(jnp.int32, (n, n), 1)
    strict = cc < rr
    incl = cc <= rr
    eye = (cc == rr).astype(F32)
    tri = _tri(C)
    nsteps = int(math.log2(C)) - 1

    def stage_b(gi, carry):
        units = []
        for cu in range(unroll):
            ci = gi * unroll + cu
            rows = pl.ds(pl.multiple_of(ci * C, C), C)
            ld = ld_scr[rows, :]
            b = _mm_exact_lhs(tri, ld)
            eb = jnp.exp(b)
            enb = jnp.exp(-b)
            kk = kk_scr[rows, :]
            al = -kk * jnp.exp(b - ld)
            be = kk * a_scr[rows, :] * enb
            kt = k2_scr[rows, :] * enb
            rb = r_ref[0, rows, :].astype(F32) * eb
            v = v_ref[0, rows, :].astype(F32)
            e_c = eb[C - 1:C, :]
            ec_scr[ci] = jnp.broadcast_to(e_c, (8, e_c.shape[1]))
            for pp in range(lp):
                ls = slice(pp * LANES, (pp + 1) * LANES)
                cat = lambda x: jnp.concatenate([x[:, ls] * lo, x[:, ls] * hi], axis=0)
                units.append(dict(idx=ci * lp + pp, ecp=e_c[:, ls], la=cat(al), lr=cat(rb), rb=cat(be),
                                  rk=cat(kt), vb=cat(v)))
        nu = range(len(units))
        if n % LANES == 0:
            prod = [_mm(jnp.concatenate([u["la"], u["lr"]], axis=0),
                        jnp.concatenate([u["rb"], u["rk"]], axis=0), NT) for u in units]
            m_ab = [jnp.where(strict, x[0:n, 0:n], 0.0) for x in prod]
            m_ak = [jnp.where(strict, x[0:n, n:2 * n], 0.0) for x in prod]
            m_rb = [jnp.where(incl, x[n:2 * n, 0:n], 0.0) for x in prod]
            m_rk = [jnp.where(incl, x[n:2 * n, n:2 * n], 0.0) for x in prod]
            akv = [_mm(m_ak[i], units[i]["vb"]) for i in nu]
            tinv = [eye + a_ for a_ in m_ab]
            apow = [_mm(a_, a_) for a_ in m_ab]
            for step in range(nsteps):
                if step < nsteps - 1:
                    both = [_mm(apow[i], jnp.concatenate([tinv[i], apow[i]], axis=1)) for i in nu]
                    tinv = [tinv[i] + both[i][:, 0:n] for i in nu]
                    apow = [x[:, n:2 * n] for x in both]
                else:
                    tinv = [tinv[i] + _mm(apow[i], tinv[i]) for i in nu]
            wu = [_mm(tinv[i], jnp.concatenate([units[i]["la"], akv[i]], axis=1)) for i in nu]
            w = [x[:, 0:LANES] for x in wu]
            uv = [jnp.concatenate([wu[i][:, LANES:2 * LANES], units[i]["vb"]], axis=0) for i in nu]
            rbe = [u["rb"] * u["ecp"] for u in units]
            for i, u in enumerate(units):
                q_scr[u["idx"]] = u["lr"] + _mm(m_rb[i], w[i])
            for i, u in enumerate(units):
                o0_scr[u["idx"]] = _mm(jnp.concatenate([m_rb[i], m_rk[i]], axis=1), uv[i])
            for i, u in enumerate(units):
                gp_scr[u["idx"]] = _mm(w[i], rbe[i], TN)
            for i, u in enumerate(units):
                z_scr[u["idx"]] = _mm(uv[i], jnp.concatenate([rbe[i], u["rk"] * u["ecp"]], axis=0), TN)
            return carry
        each = lambda fn: [fn(u) for u in units]
        m_ab = each(lambda u: jnp.where(strict, _mm(u["la"], u["rb"], NT), 0.0))
        m_ak = each(lambda u: jnp.where(strict, _mm(u["la"], u["rk"], NT), 0.0))
        m_rb = each(lambda u: jnp.where(incl, _mm(u["lr"], u["rb"], NT), 0.0))
        m_rk = each(lambda u: jnp.where(incl, _mm(u["lr"], u["rk"], NT), 0.0))
        akv = [_mm(m, u["vb"]) for m, u in zip(m_ak, units)]
        rkv = [_mm(m, u["vb"]) for m, u in zip(m_rk, units)]
        apow = m_ab
        tinv = [eye + a_ for a_ in apow]
        for _ in range(nsteps):
            apow = [_mm(a_, a_) for a_ in apow]
            tinv = [t_ + _mm(a_, t_) for a_, t_ in zip(apow, tinv)]
        w = [_mm(t_, u["la"]) for t_, u in zip(tinv, units)]
        u0 = [_mm(t_, x_) for t_, x_ in zip(tinv, akv)]
        rbe = [u["rb"] * u["ecp"] for u in units]
        for i, u in enumerate(units):
            q_scr[u["idx"]] = u["lr"] + _mm(m_rb[i], w[i])
        for i, u in enumerate(units):
            o0_scr[u["idx"]] = _mm(m_rb[i], u0[i]) + rkv[i]
        for i, u in enumerate(units):
            gp_scr[u["idx"]] = _mm(w[i], rbe[i], TN)
        for i, u in enumerate(units):
            z_scr[u["idx"]] = _mm(u0[i], rbe[i], TN) + _mm(u["vb"], u["rk"] * u["ecp"], TN)
        return carry

    lax.fori_loop(0, nchunks // unroll, stage_b, 0)

    for ci in range(nchunks):
        for pp in range(lp):
            ls = slice(pp * LANES, (pp + 1) * LANES)
            idx = ci * lp + pp
            sidx = idx if nseq > 1 else pp
            s = s_ref[sidx]
            o_bd = _mm(q_scr[idx], s, NT) + o0_scr[idx]
            o_scr[ci * C:(ci + 1) * C, ls] = o_bd[0:C] + o_bd[C:n]
            s_ref[sidx] = s * ec_scr[ci, 0:1, ls] + _mm(s, gp_scr[idx]) + z_scr[idx]

    o = o_scr[...]
    v = v_ref[0].astype(F32)
    mean = per_pair(lambda x: _mm(x, mean_bd, NN, STAT_PASSES), o)
    dlt = o - mean
    var = per_pair(lambda x: _mm(x, mean_bd, NN, STAT_PASSES), dlt * dlt)
    on = dlt * lax.rsqrt(var + B_LN_EPS) * ln_g + ln_b
    bonus = per_pair(lambda x: _mm(x, ones_bd, NN, STAT_PASSES), r_ref[0].astype(F32) * k2_scr[...] * rk_w)
    on = on + bonus * v
    og_ref[...] = (on * _silu(g_ref[0].astype(F32))).astype(og_ref.dtype)

    @pl.when(t == nt - 1)
    def _():
        for si in range(nseq):
            for pp in range(lp):
                sout_ref[si, 2 * pp] = s_ref[si * lp + pp, 0:HEAD, 0:HEAD]
                sout_ref[si, 2 * pp + 1] = s_ref[si * lp + pp, HEAD:LANES, HEAD:LANES]


def _rwkv_mix(rkvg, lora, wup, vec, s0, bn, tn_, C, tb, lp, unroll, nseq, og_dtype):
    m = rkvg.shape[1]
    nch = rkvg.shape[2]
    wl = lp * LANES
    ngrp = nch // wl
    nt = tn_ * nseq // tb
    nchunks = tb // C
    rank = lora.shape[2]
    nh = s0.shape[1]
    row = lambda b, t: b * nt + t

    def xspec(c):
        return pl.BlockSpec((1, tb, wl), lambda b, p, t: (c, row(b, t), p))

    def lspec(c):
        return pl.BlockSpec((1, tb, rank), lambda b, p, t: (c, row(b, t), 0))

    og, sout = pl.pallas_call(
        functools.partial(_rwkv_kernel, C=C, tb=tb, lp=lp, nt=nt, unroll=unroll, nseq=nseq),
        grid=(bn // nseq, ngrp, nt),
        in_specs=[
            xspec(0), xspec(1), xspec(2), xspec(3), lspec(0), lspec(1),
            pl.BlockSpec((2, rank, wl), lambda b, p, t: (0, 0, p)),
            pl.BlockSpec((8, wl), lambda b, p, t: (0, p)),
            pl.BlockSpec((nseq, 2 * lp, HEAD, HEAD), lambda b, p, t: (b, p, 0, 0)),
        ],
        out_specs=[
            pl.BlockSpec((tb, wl), lambda b, p, t: (row(b, t), p)),
            pl.BlockSpec((nseq, 2 * lp, HEAD, HEAD), lambda b, p, t: (b, p, 0, 0)),
        ],
        out_shape=[
            jax.ShapeDtypeStruct((m, nch), og_dtype),
            jax.ShapeDtypeStruct((bn, nh, HEAD, HEAD), F32),
        ],
        scratch_shapes=[
            pltpu.VMEM((nseq * lp, LANES, LANES), F32),
            pltpu.VMEM((tb, wl), F32), pltpu.VMEM((tb, wl), F32), pltpu.VMEM((tb, wl), F32), pltpu.VMEM((tb, wl), F32),
            pltpu.VMEM((nchunks * lp, 2 * C, LANES), F32), pltpu.VMEM((nchunks * lp, 2 * C, LANES), F32),
            pltpu.VMEM((nchunks * lp, LANES, LANES), F32), pltpu.VMEM((nchunks * lp, LANES, LANES), F32),
            pltpu.VMEM((nchunks, 8, wl), F32),
            pltpu.VMEM((tb, wl), F32),
        ],
        compiler_params=_params(("parallel", "parallel", "arbitrary")),
        name="rwkv7_chunked",
    )(rkvg, rkvg, rkvg, rkvg, lora, lora, wup, vec, s0)
    return og, sout


def _rwkv_layer(x2d, bn, tn_, g, j, shift, s0, mu, w_rkvg, w_down, w_up, w0, a0, k_k, k_a, r_k, ln_g, ln_b,
                w_out, tm, tn_cap, C, tb, lp, unroll, nseq, og_dtype):
    m, d = x2d.shape
    nblk = m // tm
    if tm <= tn_:
        starts = jnp.arange(nblk) * tm
        prev = _rms_rows(x2d[jnp.maximum(starts - 1, 0)], g)
        first = jnp.where((starts % tn_ == 0)[:, None], shift[starts // tn_], prev).reshape(nblk, 1, d)
        period = None
    else:
        assert nblk == 1
        first = jnp.zeros((bn, tn_, d), F32).at[:, 0].set(shift).reshape(1, m, d)
        period = tn_
    shift_out = _rms_rows(x2d[tn_ - 1::tn_], g)
    nch = w_rkvg.shape[2]
    rkvg = _proj(x2d, g, w_rkvg, 0, nch, tm, _pick_tn(nch, tn_cap), og_dtype, first=first, mu=mu[:4], period=period,
                 w_base=4 * j, nc=4)
    rank = w_down.shape[2]
    lora = _proj(x2d, g, w_down, 0, rank, tm, rank, F32, first=first, mu=mu[4:6], period=period, w_base=2 * j, nc=2)
    vec = jnp.stack([w0, a0, k_k, k_a, r_k.reshape(nch), ln_g, ln_b, jnp.zeros((nch,), F32)])
    og, sout = _rwkv_mix(rkvg, lora, w_up, vec, s0, bn, tn_, C, tb, lp, unroll, nseq, og_dtype)
    y = _outproj(og, w_out, j, x2d, tm, 512)
    return y, sout, shift_out


GLA_SUB = 16


def _gla_kernel(q_ref, k_ref, v_ref, gate_ref, gl_ref, wup_ref, bgk_ref, og_w_ref, s0_ref,
                og_ref, sout_ref, st_ref, z_scr, qd_scr, o_scr, ec_scr, *, C, tb, nt, nseq):
    t = pl.program_id(2)
    dk = q_ref.shape[-1]
    nchunks = tb // C
    sub = min(GLA_SUB, C)
    nsub = C // sub
    qscale = dk ** -0.5

    @pl.when(t == 0)
    def _():
        for si in range(nseq):
            st_ref[si] = s0_ref[si, 0].T

    tri = _tri(C)
    glog_all = -_softplus(-(_mm(gl_ref[...], wup_ref[...], NN, 3) + bgk_ref[...])) * (1.0 / C_GATE_NORM)
    chunks = range(nchunks)
    rows = [slice(c * C, (c + 1) * C) for c in chunks]
    glog = [glog_all[rs] for rs in rows]
    b = [_mm_exact_lhs(tri, gl_) for gl_ in glog]
    q = [q_ref[rs, :].astype(F32) * qscale for rs in rows]
    k = [k_ref[rs, :].astype(F32) for rs in rows]
    v = [v_ref[rs, :].astype(F32) for rs in rows]
    for c in chunks:
        bl = b[c][C - 1:C, :]
        qd_scr[rows[c], :] = q[c] * jnp.exp(b[c])
        ec_scr[c] = jnp.broadcast_to(jnp.exp(bl), (8, dk))
        z_scr[c] = _mm(v[c], k[c] * jnp.exp(bl - b[c]), TN)
    parts = [[] for _ in chunks]
    for i in range(nsub):
        r0 = i * sub
        nk = r0 + sub
        ar = lax.broadcasted_iota(jnp.int32, (sub, nk), 0) + r0
        ac = lax.broadcasted_iota(jnp.int32, (sub, nk), 1)
        att = []
        for c in chunks:
            ref = b[c][r0:r0 + 1, :] - glog[c][r0:r0 + 1, :]
            qi = q[c][r0:nk] * jnp.exp(b[c][r0:nk] - ref)
            ki = k[c][0:nk] * jnp.exp(ref - b[c][0:nk])
            att.append(jnp.where(ac <= ar, _mm(qi, ki, NT), 0.0))
        for c in chunks:
            parts[c].append(_mm(att[c], v[c][0:nk]))
    for c in chunks:
        o_scr[rows[c], :] = jnp.concatenate(parts[c], axis=0) if nsub > 1 else parts[c][0]

    for c in chunks:
        si = c if nseq > 1 else 0
        st = st_ref[si]
        o_scr[rows[c], :] = o_scr[rows[c], :] + _mm(qd_scr[rows[c], :], st, NT)
        st_ref[si] = st * ec_scr[c, 0:1, :] + z_scr[c]

    o = o_scr[...]
    on = o * lax.rsqrt(jnp.mean(o * o, axis=-1, keepdims=True) + NORM_EPS) * og_w_ref[...]
    og_ref[...] = (on * _silu(gate_ref[...].astype(F32))).astype(og_ref.dtype)

    @pl.when(t == nt - 1)
    def _():
        for si in range(nseq):
            sout_ref[si, 0] = st_ref[si].T


def _gla_mix(qkvg, gl, w_up, b_gk, o_g, s0, bn, tn_, C, tb, nseq, og_dtype):
    m = qkvg.shape[0]
    nh, dk, dv = s0.shape[1:]
    nt = tn_ * nseq // tb
    rank = gl.shape[1]
    row = lambda b, t: b * nt + t
    kcol = nh
    vcol = 2 * nh * dk // dv
    gcol = vcol + nh
    og, sout = pl.pallas_call(
        functools.partial(_gla_kernel, C=C, tb=tb, nt=nt, nseq=nseq),
        grid=(bn // nseq, nh, nt),
        in_specs=[
            pl.BlockSpec((tb, dk), lambda b, h, t: (row(b, t), h)),
            pl.BlockSpec((tb, dk), lambda b, h, t: (row(b, t), kcol + h)),
            pl.BlockSpec((tb, dv), lambda b, h, t: (row(b, t), vcol + h)),
            pl.BlockSpec((tb, dv), lambda b, h, t: (row(b, t), gcol + h)),
            pl.BlockSpec((tb, rank), lambda b, h, t: (row(b, t), 0)),
            pl.BlockSpec((rank, dk), lambda b, h, t: (0, h)),
            pl.BlockSpec((1, dk), lambda b, h, t: (0, h)),
            pl.BlockSpec((1, dv), lambda b, h, t: (0, 0)),
            pl.BlockSpec((nseq, 1, dk, dv), lambda b, h, t: (b, h, 0, 0)),
        ],
        out_specs=[
            pl.BlockSpec((tb, dv), lambda b, h, t: (row(b, t), h)),
            pl.BlockSpec((nseq, 1, dk, dv), lambda b, h, t: (b, h, 0, 0)),
        ],
        out_shape=[
            jax.ShapeDtypeStruct((m, nh * dv), og_dtype),
            jax.ShapeDtypeStruct((bn, nh, dk, dv), F32),
        ],
        scratch_shapes=[
            pltpu.VMEM((nseq, dv, dk), F32),
            pltpu.VMEM((tb // C, dv, dk), F32),
            pltpu.VMEM((tb, dk), F32),
            pltpu.VMEM((tb, dv), F32),
            pltpu.VMEM((tb // C, 8, dk), F32),
        ],
        compiler_params=_params(("parallel", "parallel", "arbitrary")),
        name="gla_chunked",
    )(qkvg, qkvg, qkvg, qkvg, gl, w_up, b_gk.reshape(1, -1), o_g.reshape(1, -1), s0)
    return og, sout


def _gla_layer(x2d, bn, tn_, g, j, s0, w3, w_gl, w_up, b_gk, o_g, w_out, tm, tn_cap, C, tb, nseq, og_dtype):
    nh, dk, dv = s0.shape[1:]
    nfused = 2 * nh * dk + 2 * nh * dv
    rank = w_up.shape[0]
    qkvg = _proj(x2d, g, w3, 0, nfused, tm, _pick_tn(nfused, tn_cap), og_dtype, w_base=j)[0]
    gl = _proj(x2d, g, w_gl, 0, rank, tm, rank, F32, w_base=j)[0]
    og, sout = _gla_mix(qkvg, gl, w_up, b_gk, o_g, s0, bn, tn_, C, tb, nseq, og_dtype)
    y = _outproj(og, w_out, j, x2d, tm, 512)
    return y, sout


PROMPT_TM = 1024
RWKV_CHUNK = 64
RWKV_TB = 512
RWKV_PAIRS = 4
RWKV_UNROLL = 4
STAT_PASSES = 1
GLA_CHUNK = 64
GLA_TB = 512
PROMPT_TN = 1024
LERP_TN = 512
SAMPLE_TN = 2048
SAMPLE_SEQS = 4


def kernel(x_prompt, x_sample, cache_k_win, cache_v_win, state_wkv, state_shift, state_gla, norm_g, rel_bias, w_in_a, q_norm_g, k_norm_g, sinks, w_out_a, mu_b, w_rkvg_b, w_lora_down_b, w_lora_up_b, w0_b, a0_b, k_k_b, k_a_b, r_k_b, ln_x_g_b, ln_x_b_b, w_out_b, w_in_c, w_gk_up_c, b_gk_c, o_norm_g_c, w_out_c):
    bp, tp, d = x_prompt.shape
    bs, ts, _ = x_sample.shape
    depth = norm_g.shape[0]
    bf = lambda w: w.astype(BF16)
    nfused_c = w_in_c.shape[2] - w_gk_up_c.shape[1]
    w_in_a, w_out_a, w_out_b, w_out_c = bf(w_in_a), bf(w_out_a), bf(w_out_b), bf(w_out_c)
    w_rkvg_b = bf(w_rkvg_b).reshape((-1,) + w_rkvg_b.shape[2:])
    w_lora_down_b = bf(w_lora_down_b).reshape((-1,) + w_lora_down_b.shape[2:])
    w_gl_c = bf(w_in_c[:, :, nfused_c:])
    w_in_c = bf(w_in_c)
    xp = x_prompt.reshape(bp * tp, d)
    xs = x_sample.reshape(bs * ts, d)
    ms = bs * ts
    kwp, vwp, kws, vws, wkvp, shp, wkvs, shs, glap, glas = ([] for _ in range(10))
    for layer in range(depth):
        kind, j = layer % 3, layer // 3
        g = norm_g[layer]
        if kind == 0:
            wa = (w_in_a, q_norm_g[j], k_norm_g[j], sinks[j], w_out_a, rel_bias)
            xp, kp_, vp_ = _attn_layer(xp, bp, tp, g, j, *wa, None, None, PROMPT_TM, PROMPT_TN, BF16)
            xs, ks_, vs_ = _attn_layer(xs, bs, ts, g, j, *wa, cache_k_win[j], cache_v_win[j], ms, SAMPLE_TN, F32)
            kwp.append(kp_); vwp.append(vp_); kws.append(ks_); vws.append(vs_)
        elif kind == 1:
            wb = (mu_b[j], w_rkvg_b, w_lora_down_b, w_lora_up_b[j], w0_b[j], a0_b[j], k_k_b[j], k_a_b[j],
                  r_k_b[j], ln_x_g_b[j], ln_x_b_b[j], w_out_b)
            nh = w_rkvg_b.shape[2] // HEAD
            xp, sp_, lp_ = _rwkv_layer(xp, bp, tp, g, j, jnp.zeros((bp, d), F32), jnp.zeros((bp, nh, HEAD, HEAD), F32),
                                       *wb, PROMPT_TM, LERP_TN, RWKV_CHUNK, RWKV_TB, RWKV_PAIRS, RWKV_UNROLL, 1, BF16)
            xs, ss_, ls_ = _rwkv_layer(xs, bs, ts, g, j, state_shift[j], state_wkv[j], *wb, ms, SAMPLE_TN, ts,
                                       ts * SAMPLE_SEQS, RWKV_PAIRS, SAMPLE_SEQS, SAMPLE_SEQS, F32)
            wkvp.append(sp_); shp.append(lp_); wkvs.append(ss_); shs.append(ls_)
        else:
            wc = (w_in_c, w_gl_c, w_gk_up_c[j], b_gk_c[j], o_norm_g_c[j], w_out_c)
            xp, sp_ = _gla_layer(xp, bp, tp, g, j, jnp.zeros((bp,) + state_gla.shape[2:], F32), *wc,
                                 PROMPT_TM, PROMPT_TN, GLA_CHUNK, GLA_TB, 1, BF16)
            xs, ss_ = _gla_layer(xs, bs, ts, g, j, state_gla[j], *wc, ms, SAMPLE_TN, ts, ts * SAMPLE_SEQS, SAMPLE_SEQS, F32)
            glap.append(sp_); glas.append(ss_)
    return (xp.reshape(bp, tp, d), xs.reshape(bs, ts, d),
            jnp.stack(kwp), jnp.stack(vwp), jnp.stack(kws), jnp.stack(vws),
            jnp.stack(wkvp), jnp.stack(shp), jnp.stack(wkvs), jnp.stack(shs),
            jnp.stack(glap), jnp.stack(glas))
```

```python
import functools
import math

import jax
import jax.numpy as jnp
from jax import lax
from jax.experimental import pallas as pl
from jax.experimental.pallas import tpu as pltpu

F32 = jnp.float32
BF16 = jnp.bfloat16

NORM_EPS = 1e-6
HEAD = 64
LANES = 128
WINDOW = 128
N_BUCKETS = 32
MAX_EXACT = N_BUCKETS // 2
MAX_DISTANCE = 128
B_LN_EPS = 64e-5
C_GATE_NORM = 16.0
VMEM_LIMIT = 56 * 1024 * 1024

NN = (((1,), (0,)), ((), ()))
NT = (((1,), (1,)), ((), ()))
TN = (((0,), (0,)), ((), ()))


def _params(sem):
    return pltpu.CompilerParams(dimension_semantics=sem, vmem_limit_bytes=VMEM_LIMIT)


def _split(x):
    hi = x.astype(BF16)
    lo = (x - hi.astype(F32)).astype(BF16)
    return hi, lo


def _mm(a, b, dims=NN, passes=1):
    if passes == 1:
        return lax.dot_general(a.astype(BF16), b.astype(BF16), dims, preferred_element_type=F32)
    a_hi, a_lo = _split(a)
    b_hi, b_lo = _split(b)
    dg = functools.partial(lax.dot_general, dimension_numbers=dims, preferred_element_type=F32)
    return dg(a_hi, b_hi) + (dg(a_hi, b_lo) + dg(a_lo, b_hi))


def _mm_exact_lhs(a_bf16, b):
    b0 = b.astype(BF16)
    r1 = b - b0.astype(F32)
    b1 = r1.astype(BF16)
    b2 = (r1 - b1.astype(F32)).astype(BF16)
    dg = functools.partial(lax.dot_general, dimension_numbers=NN, preferred_element_type=F32)
    return dg(a_bf16, b0) + (dg(a_bf16, b1) + dg(a_bf16, b2))


def _softplus(z):
    return jnp.maximum(z, 0.0) + jnp.log1p(jnp.exp(-jnp.abs(z)))


def _silu(g):
    return g * jax.nn.sigmoid(g)


def _tri(c):
    r = lax.broadcasted_iota(jnp.int32, (c, c), 0)
    col = lax.broadcasted_iota(jnp.int32, (c, c), 1)
    return (col <= r).astype(BF16)


def _rms_rows_kernel(x_ref, g_ref, o_ref):
    x = x_ref[...]
    o_ref[...] = x * lax.rsqrt(jnp.mean(x * x, axis=-1, keepdims=True) + NORM_EPS) * g_ref[...]


def _rms_rows(rows, g):
    n, d = rows.shape
    npad = -(-n // 8) * 8
    rows_p = jnp.pad(rows, ((0, npad - n), (0, 0)))
    out = pl.pallas_call(
        _rms_rows_kernel,
        out_shape=jax.ShapeDtypeStruct((npad, d), F32),
        name="rms_rows",
    )(rows_p, g.reshape(1, d))
    return out[:n]


PROLOGUE_ROWS = 256


def _proj_kernel(*refs, lerp, period, tm):
    if lerp:
        x_ref, g_ref, w_ref, first_ref, mu_ref, o_ref, xm_ref, h_ref, d_ref = refs
    else:
        x_ref, g_ref, w_ref, o_ref, xm_ref = refs
    c = pl.program_id(1)
    j = pl.program_id(2)
    rc = min(PROLOGUE_ROWS, tm)

    def normed(c0):
        x = x_ref[c0:c0 + rc, :]
        return x * lax.rsqrt(jnp.mean(x * x, axis=-1, keepdims=True) + NORM_EPS) * g_ref[...]

    if lerp:
        @pl.when((c == 0) & (j == 0))
        def _():
            carry = first_ref[0, 0:1, :]
            for c0 in range(0, tm, rc):
                h = normed(c0)
                row = lax.broadcasted_iota(jnp.int32, h.shape, 0)
                hs = jnp.where(row == 0, carry, pltpu.roll(h, 1, 0))
                if period is not None:
                    hs = jnp.where(row % period == 0, first_ref[0, c0:c0 + rc, :], hs)
                carry = h[rc - 1:rc, :]
                h_ref[c0:c0 + rc, :] = h
                d_ref[c0:c0 + rc, :] = hs - h

        @pl.when(j == 0)
        def _():
            for c0 in range(0, tm, rc):
                xm_ref[c0:c0 + rc, :] = (h_ref[c0:c0 + rc, :] + d_ref[c0:c0 + rc, :] * mu_ref[0]).astype(BF16)
    else:
        @pl.when(j == 0)
        def _():
            for c0 in range(0, tm, rc):
                xm_ref[c0:c0 + rc, :] = normed(c0).astype(BF16)

    o_ref[0] = jnp.dot(xm_ref[...], w_ref[0], preferred_element_type=F32).astype(o_ref.dtype)


def _pick_tn(n, cap):
    if n < LANES:
        return n
    units = n // LANES
    return LANES * max(u for u in range(1, units + 1) if units % u == 0 and u * LANES <= cap)


def _proj(x2d, g, w3, col_off, n_out, tm, tn, out_dtype, first=None, mu=None, period=None, w_base=0, nc=1):
    m, d = x2d.shape
    lerp = first is not None
    assert m % tm == 0 and n_out % tn == 0 and col_off % tn == 0
    joff = col_off // tn
    in_specs = [
        pl.BlockSpec((tm, d), lambda i, c, j: (i, 0)),
        pl.BlockSpec((1, d), lambda i, c, j: (0, 0)),
        pl.BlockSpec((1, d, tn), lambda i, c, j: (w_base + c, 0, joff + j)),
    ]
    args = [x2d, g.reshape(1, d), w3]
    if lerp:
        fr = first.shape[1]
        in_specs += [
            pl.BlockSpec((1, fr, d), lambda i, c, j: (i, 0, 0)),
            pl.BlockSpec((1, 1, d), lambda i, c, j: (c, 0, 0)),
        ]
        args += [first, mu.reshape(nc, 1, d)]
    return pl.pallas_call(
        functools.partial(_proj_kernel, lerp=lerp, period=period, tm=tm),
        grid=(m // tm, nc, n_out // tn),
        in_specs=in_specs,
        out_specs=pl.BlockSpec((1, tm, tn), lambda i, c, j: (c, i, j)),
        out_shape=jax.ShapeDtypeStruct((nc, m, n_out), out_dtype),
        scratch_shapes=[pltpu.VMEM((tm, d), BF16)] + ([pltpu.VMEM((tm, d), F32)] * 2 if lerp else []),
        compiler_params=_params(("parallel", "arbitrary", "arbitrary")),
        name="proj",
    )(*args)


def _outproj_kernel(a_ref, w_ref, x_ref, o_ref):
    o_ref[...] = x_ref[...] + jnp.dot(a_ref[...].astype(BF16), w_ref[0], preferred_element_type=F32)


def _outproj(a2d, w3, w_base, x2d, tm, tn):
    m, kdim = a2d.shape
    n = w3.shape[2]
    return pl.pallas_call(
        _outproj_kernel,
        grid=(m // tm, n // tn),
        in_specs=[
            pl.BlockSpec((tm, kdim), lambda i, j: (i, 0)),
            pl.BlockSpec((1, kdim, tn), lambda i, j: (w_base, 0, j)),
            pl.BlockSpec((tm, tn), lambda i, j: (i, j)),
        ],
        out_specs=pl.BlockSpec((tm, tn), lambda i, j: (i, j)),
        out_shape=jax.ShapeDtypeStruct((m, n), F32),
        compiler_params=_params(("parallel", "arbitrary")),
        name="outproj",
    )(a2d, w3, x2d)


A_KV_HEADS = 8
A_GROUP = 8


def _attn_kernel(*refs, tq, prompt, nblocks):
    q_ref, kc_ref, vc_ref, kp_ref, vp_ref = refs[:5]
    gate_refs = refs[5:5 + A_KV_HEADS]
    (bias_ref, qg_ref, kg_ref, og_ref, kwin_ref, vwin_ref,
     k_scr, v_scr, s_scr, p_scr, qn_scr) = refs[5 + A_KV_HEADS:]
    return _attn_body(q_ref, kc_ref, vc_ref, kp_ref, vp_ref, gate_refs, bias_ref, qg_ref, kg_ref,
                      og_ref, kwin_ref, vwin_ref, k_scr, v_scr, s_scr, p_scr, qn_scr,
                      tq=tq, prompt=prompt, nblocks=nblocks)


def _attn_body(q_ref, kc_ref, vc_ref, kp_ref, vp_ref, gate_refs, bias_ref, qg_ref, kg_ref,
               og_ref, kwin_ref, vwin_ref, k_scr, v_scr, s_scr, p_scr, qn_scr, *, tq, prompt, nblocks):
    i = pl.program_id(1)
    tk = WINDOW + tq
    kg = kg_ref[...]
    qg = qg_ref[...] * (HEAD ** -0.5)
    nslab = kg.shape[1]
    jr = lax.broadcasted_iota(jnp.int32, (nslab, nslab), 0)
    jc = lax.broadcasted_iota(jnp.int32, (nslab, nslab), 1)
    mean_bd = jnp.where((jr // HEAD) == (jc // HEAD), 1.0 / HEAD, 0.0).astype(BF16)
    ones_v = jnp.ones((tk, HEAD), BF16)

    def slab_norm(x, gain):
        ms = jnp.dot((x * x).astype(BF16), mean_bd, preferred_element_type=F32)
        return x * lax.rsqrt(ms + NORM_EPS) * gain

    if prompt:
        k_scr[0:WINDOW, :] = slab_norm(kp_ref[...].astype(F32), kg)
        k_scr[WINDOW:tk, :] = slab_norm(kc_ref[...].astype(F32), kg)
        v_scr[0:WINDOW, :] = vp_ref[...].astype(F32)
    else:
        q_all = q_ref[...].astype(F32)
        slabs = [q_all[:, g * nslab:(g + 1) * nslab] for g in range(A_KV_HEADS)] + [kc_ref[...].astype(F32)]
        inv = lax.rsqrt(jnp.dot(jnp.concatenate([x * x for x in slabs], axis=0).astype(BF16), mean_bd,
                                preferred_element_type=F32) + NORM_EPS)
        for g in range(A_KV_HEADS):
            qn_scr[:, g * nslab:(g + 1) * nslab] = slabs[g] * inv[g * tq:(g + 1) * tq] * qg
        k_scr[WINDOW:tk, :] = slabs[A_KV_HEADS] * inv[A_KV_HEADS * tq:(A_KV_HEADS + 1) * tq] * kg
        for kh in range(A_KV_HEADS):
            k_scr[0:WINDOW, kh * HEAD:(kh + 1) * HEAD] = kp_ref[0, :, kh, :]
            v_scr[0:WINDOW, kh * HEAD:(kh + 1) * HEAD] = vp_ref[0, :, kh, :]
    v_scr[WINDOW:tk, :] = vc_ref[...].astype(F32)

    @pl.when(i == nblocks - 1)
    def _():
        keep = 0 if prompt else WINDOW - tq
        if keep:
            kwin_ref[0, 0:keep] = kp_ref[0, tq:WINDOW]
            vwin_ref[0, 0:keep] = vp_ref[0, tq:WINDOW]
        for kh in range(A_KV_HEADS):
            sl = slice(kh * HEAD, (kh + 1) * HEAD)
            kwin_ref[0, keep:WINDOW, kh, :] = k_scr[tk - WINDOW + keep:tk, sl]
            vwin_ref[0, keep:WINDOW, kh, :] = v_scr[tk - WINDOW + keep:tk, sl]

    k_scr[0:1, :] = jnp.zeros((1, k_scr.shape[1]), F32)
    v_scr[0:1, :] = jnp.zeros((1, v_scr.shape[1]), F32)

    c = lax.broadcasted_iota(jnp.int32, (tq, tk), 1)
    no_prev = (c >= WINDOW) | (c == 0)
    lo_half = lax.broadcasted_iota(jnp.int32, (tq, LANES), 1) < HEAD

    def group(kh, first_block):
        kk = k_scr[:, kh * HEAD:(kh + 1) * HEAD].astype(BF16)
        vv = v_scr[:, kh * HEAD:(kh + 1) * HEAD].astype(BF16)
        h0 = kh * A_GROUP
        lanes = slice(h0 * HEAD, (h0 + A_GROUP) * HEAD)
        qn = slab_norm(q_ref[:, lanes].astype(F32), qg) if prompt else qn_scr[:, lanes]
        qs = jnp.concatenate([qn[:, e * HEAD:(e + 1) * HEAD] for e in range(A_GROUP)], axis=0)
        s_scr[...] = lax.dot_general(qs.astype(BF16), kk, NT, preferred_element_type=F32)
        for e in range(A_GROUP):
            rs = slice(e * tq, (e + 1) * tq)
            s = s_scr[rs, :] + bias_ref[h0 + e]
            if first_block:
                s = jnp.where(no_prev, s, -jnp.inf)
            p_scr[rs, :] = jnp.exp(s - jnp.max(s, axis=-1, keepdims=True)).astype(p_scr.dtype)
        x = jnp.dot(p_scr[...].astype(BF16), jnp.concatenate([vv, ones_v, ones_v, vv], axis=-1),
                    preferred_element_type=F32)
        pairs = []
        for e in range(0, A_GROUP, 2):
            xe, xo = x[e * tq:(e + 1) * tq], x[(e + 1) * tq:(e + 2) * tq]
            num = jnp.where(lo_half, xe[:, 0:LANES], xo[:, LANES:2 * LANES])
            den = jnp.where(lo_half, xe[:, LANES:2 * LANES], xo[:, 0:LANES])
            pairs.append(num / den)
        o = jnp.concatenate(pairs, axis=-1)
        og_ref[:, lanes] = (o * _silu(gate_refs[kh][...].astype(F32))).astype(og_ref.dtype)

    if prompt:
        @pl.when(i == 0)
        def _():
            for kh in range(A_KV_HEADS):
                group(kh, True)

        @pl.when(i > 0)
        def _():
            for kh in range(A_KV_HEADS):
                group(kh, False)
    else:
        for kh in range(A_KV_HEADS):
            group(kh, False)


def _t5_bucket(dist):
    d = jnp.maximum(dist, 0)
    large = MAX_EXACT + (jnp.log(jnp.maximum(d, 1).astype(F32) / MAX_EXACT)
                         / math.log(MAX_DISTANCE / MAX_EXACT) * (N_BUCKETS - MAX_EXACT)).astype(jnp.int32)
    return jnp.where(d < MAX_EXACT, d, jnp.minimum(large, N_BUCKETS - 1))


def _attn_mix(qkvg, nq, rel_bias, q_g, k_g, sinks, bn, tn_, cache_k, cache_v, og_dtype):
    m = qkvg.shape[0]
    prompt = cache_k is None
    tq = WINDOW if prompt else tn_
    nb = tn_ // tq
    tk = WINDOW + tq
    col = jnp.arange(tk)[None, :]
    dist = WINDOW + jnp.arange(tq)[:, None] - col
    onehot = (_t5_bucket(dist)[None] == jnp.arange(N_BUCKETS)[:, None, None]).astype(F32)
    bias = jnp.einsum("bh,bqk->hqk", rel_bias.astype(F32), onehot, precision=lax.Precision.HIGHEST)
    bias = jnp.where(((dist >= 0) & (dist < WINDOW))[None], bias, -jnp.inf)
    bias = jnp.where((col == 0)[None], sinks.astype(F32)[:, None, None], bias)
    nkv = A_KV_HEADS * HEAD
    row = lambda b, i: b * nb + i
    kcol = nq // nkv
    gcol = kcol + 2
    win_shape = (1, WINDOW, A_KV_HEADS, HEAD)
    if prompt:
        kp_arr, vp_arr = qkvg, qkvg
        kp_spec = pl.BlockSpec((WINDOW, nkv), lambda b, i: (jnp.maximum(row(b, i) - 1, 0), kcol))
        vp_spec = pl.BlockSpec((WINDOW, nkv), lambda b, i: (jnp.maximum(row(b, i) - 1, 0), kcol + 1))
    else:
        kp_arr, vp_arr = cache_k, cache_v
        kp_spec = pl.BlockSpec(win_shape, lambda b, i: (b, 0, 0, 0))
        vp_spec = pl.BlockSpec(win_shape, lambda b, i: (b, 0, 0, 0))

    def gate_spec(kh):
        return pl.BlockSpec((tq, nkv), lambda b, i: (row(b, i), gcol + kh))

    og, kwin, vwin = pl.pallas_call(
        functools.partial(_attn_kernel, tq=tq, prompt=prompt, nblocks=nb),
        grid=(bn, nb),
        in_specs=[
            pl.BlockSpec((tq, nq), lambda b, i: (row(b, i), 0)),
            pl.BlockSpec((tq, nkv), lambda b, i: (row(b, i), kcol)),
            pl.BlockSpec((tq, nkv), lambda b, i: (row(b, i), kcol + 1)),
            kp_spec, vp_spec,
            *[gate_spec(kh) for kh in range(A_KV_HEADS)],
            pl.BlockSpec((nq // HEAD, tq, tk), lambda b, i: (0, 0, 0)),
            pl.BlockSpec((1, nkv), lambda b, i: (0, 0)),
            pl.BlockSpec((1, nkv), lambda b, i: (0, 0)),
        ],
        out_specs=[
            pl.BlockSpec((tq, nq), lambda b, i: (row(b, i), 0)),
            pl.BlockSpec(win_shape, lambda b, i: (b, 0, 0, 0)),
            pl.BlockSpec(win_shape, lambda b, i: (b, 0, 0, 0)),
        ],
        out_shape=[
            jax.ShapeDtypeStruct((m, nq), og_dtype),
            jax.ShapeDtypeStruct((bn,) + win_shape[1:], F32),
            jax.ShapeDtypeStruct((bn,) + win_shape[1:], F32),
        ],
        scratch_shapes=[
            pltpu.VMEM((tk, nkv), F32), pltpu.VMEM((tk, nkv), F32),
            pltpu.VMEM((A_GROUP * tq, tk), F32),
            pltpu.VMEM((A_GROUP * tq, tk), BF16 if tq % 16 == 0 else F32),
            pltpu.VMEM((8, LANES) if prompt else (tq, nq), F32),
        ],
        compiler_params=_params(("parallel", "arbitrary")),
        name="swa_attention",
    )(qkvg, qkvg, qkvg, kp_arr, vp_arr, *([qkvg] * A_KV_HEADS), bias,
      jnp.tile(q_g.astype(F32), A_GROUP).reshape(1, nkv), jnp.tile(k_g.astype(F32), A_KV_HEADS).reshape(1, nkv))
    return og, kwin, vwin


def _attn_layer(x2d, bn, tn_, g, j, w3, q_g, k_g, sinks, w_out, rel_bias, cache_k, cache_v, tm, tn_cap, og_dtype):
    nq = w_out.shape[1]
    nall = w3.shape[2]
    qkvg = _proj(x2d, g, w3, 0, nall, tm, _pick_tn(nall, tn_cap), og_dtype, w_base=j)[0]
    og, kwin, vwin = _attn_mix(qkvg, nq, rel_bias, q_g, k_g, sinks, bn, tn_, cache_k, cache_v, og_dtype)
    y = _outproj(og, w_out, j, x2d, tm, 512)
    return y, kwin, vwin


MXU_TILE = 256


def _head_sum_mats():
    jr = lax.broadcasted_iota(jnp.int32, (MXU_TILE, MXU_TILE), 0)
    jc = lax.broadcasted_iota(jnp.int32, (MXU_TILE, MXU_TILE), 1)
    ones_bd = ((jr // HEAD) == (jc // HEAD)).astype(F32)
    return ones_bd, ones_bd * (1.0 / HEAD)


def _per_slab(fn, x):
    return jnp.concatenate([fn(x[:, p * MXU_TILE:(p + 1) * MXU_TILE]) for p in range(x.shape[1] // MXU_TILE)],
                           axis=-1)


def _rwkv_token_terms(k_ref, lw_ref, la_ref, wup_ref, vec_ref, kk_scr, k2_scr, a_scr, ld_scr):
    vec = vec_ref[...]
    w0, a0, kk_w, ka_w = (vec[i:i + 1, :] for i in range(4))
    ones_bd, _ = _head_sum_mats()
    k = k_ref[0].astype(F32)
    xw = w0 + _mm(jnp.tanh(lw_ref[0]), wup_ref[0], NN, 3)
    ld_scr[...] = -math.exp(-0.5) * jax.nn.sigmoid(xw)
    a = jax.nn.sigmoid(a0 + _mm(la_ref[0], wup_ref[1], NN, 3))
    a_scr[...] = a
    kkr = k * kk_w
    ssq = _per_slab(lambda x: _mm(x, ones_bd, NN, STAT_PASSES), kkr * kkr)
    kk_scr[...] = kkr * lax.rsqrt(jnp.maximum(ssq, 1e-24))
    k2_scr[...] = k * (1.0 + (a - 1.0) * ka_w)


def _rwkv_finish(o_scr, r_ref, v_ref, g_ref, k2_scr, vec_ref, og_ref):
    vec = vec_ref[...]
    rk_w, ln_g, ln_b = (vec[i:i + 1, :] for i in range(4, 7))
    ones_bd, mean_bd = _head_sum_mats()
    o = o_scr[...]
    v = v_ref[0].astype(F32)
    mean = _per_slab(lambda x: _mm(x, mean_bd, NN, STAT_PASSES), o)
    dlt = o - mean
    var = _per_slab(lambda x: _mm(x, mean_bd, NN, STAT_PASSES), dlt * dlt)
    on = dlt * lax.rsqrt(var + B_LN_EPS) * ln_g + ln_b
    bonus = _per_slab(lambda x: _mm(x, ones_bd, NN, STAT_PASSES), r_ref[0].astype(F32) * k2_scr[...] * rk_w)
    on = on + bonus * v
    og_ref[...] = (on * _silu(g_ref[0].astype(F32))).astype(og_ref.dtype)


def _rwkv_kernel(r_ref, k_ref, v_ref, g_ref, lw_ref, la_ref, wup_ref, vec_ref, s0_ref,
                 og_ref, sout_ref,
                 s_ref, kk_scr, k2_scr, a_scr, ld_scr, q_scr, o0_scr, gp_scr, z_scr, ec_scr, o_scr,
                 *, C, tb, lp, nt, unroll, nseq):
    t = pl.program_id(2)
    n = 2 * C
    nchunks = tb // C
    lane = lax.broadcasted_iota(jnp.int32, (1, LANES), 1)
    lo = (lane < HEAD).astype(F32)
    hi = 1.0 - lo

    @pl.when(t == 0)
    def _():
        z = jnp.zeros((HEAD, HEAD), F32)
        for si in range(nseq):
            for pp in range(lp):
                s_ref[si * lp + pp, 0:HEAD, :] = jnp.concatenate([s0_ref[si, 2 * pp], z], axis=-1)
                s_ref[si * lp + pp, HEAD:LANES, :] = jnp.concatenate([z, s0_ref[si, 2 * pp + 1]], axis=-1)

    _rwkv_token_terms(k_ref, lw_ref, la_ref, wup_ref, vec_ref, kk_scr, k2_scr, a_scr, ld_scr)

    rr = lax.broadcasted_iota(jnp.int32, (n, n), 0)
    cc = lax.broadcasted_iota(jnp.int32, (n, n), 1)
    strict = cc < rr
    incl = cc <= rr
    eye = (cc == rr).astype(F32)
    tri = _tri(C)
    nsteps = int(math.log2(C)) - 1

    def stage_b(gi, carry):
        units = []
        for cu in range(unroll):
            ci = gi * unroll + cu
            rows = pl.ds(pl.multiple_of(ci * C, C), C)
            ld = ld_scr[rows, :]
            b = _mm_exact_lhs(tri, ld)
            eb = jnp.exp(b)
            enb = jnp.exp(-b)
            kk = kk_scr[rows, :]
            al = -kk * jnp.exp(b - ld)
            be = kk * a_scr[rows, :] * enb
            kt = k2_scr[rows, :] * enb
            rb = r_ref[0, rows, :].astype(F32) * eb
            v = v_ref[0, rows, :].astype(F32)
            e_c = eb[C - 1:C, :]
            ec_scr[ci] = jnp.broadcast_to(e_c, (8, e_c.shape[1]))
            for pp in range(lp):
                ls = slice(pp * LANES, (pp + 1) * LANES)
                cat = lambda x: jnp.concatenate([x[:, ls] * lo, x[:, ls] * hi], axis=0)
                units.append(dict(idx=ci * lp + pp, ecp=e_c[:, ls], la=cat(al), lr=cat(rb), rb=cat(be),
                                  rk=cat(kt), vb=cat(v)))
        nu = range(len(units))
        if n % LANES == 0:
            prod = [_mm(jnp.concatenate([u["la"], u["lr"]], axis=0),
                        jnp.concatenate([u["rb"], u["rk"]], axis=0), NT) for u in units]
            m_ab = [jnp.where(strict, x[0:n, 0:n], 0.0) for x in prod]
            m_ak = [jnp.where(strict, x[0:n, n:2 * n], 0.0) for x in prod]
            m_rb = [jnp.where(incl, x[n:2 * n, 0:n], 0.0) for x in prod]
            m_rk = [jnp.where(incl, x[n:2 * n, n:2 * n], 0.0) for x in prod]
            akv = [_mm(m_ak[i], units[i]["vb"]) for i in nu]
            tinv = [eye + a_ for a_ in m_ab]
            apow = [_mm(a_, a_) for a_ in m_ab]
            for step in range(nsteps):
                if step < nsteps - 1:
                    both = [_mm(apow[i], jnp.concatenate([tinv[i], apow[i]], axis=1)) for i in nu]
                    tinv = [tinv[i] + both[i][:, 0:n] for i in nu]
                    apow = [x[:, n:2 * n] for x in both]
                else:
                    tinv = [tinv[i] + _mm(apow[i], tinv[i]) for i in nu]
            wu = [_mm(tinv[i], jnp.concatenate([units[i]["la"], akv[i]], axis=1)) for i in nu]
            w = [x[:, 0:LANES] for x in wu]
            uv = [jnp.concatenate([wu[i][:, LANES:2 * LANES], units[i]["vb"]], axis=0) for i in nu]
            rbe = [u["rb"] * u["ecp"] for u in units]
            for i, u in enumerate(units):
                q_scr[u["idx"]] = u["lr"] + _mm(m_rb[i], w[i])
            for i, u in enumerate(units):
                o0_scr[u["idx"]] = _mm(jnp.concatenate([m_rb[i], m_rk[i]], axis=1), uv[i])
            for i, u in enumerate(units):
                gp_scr[u["idx"]] = _mm(w[i], rbe[i], TN)
            for i, u in enumerate(units):
                z_scr[u["idx"]] = _mm(uv[i], jnp.concatenate([rbe[i], u["rk"] * u["ecp"]], axis=0), TN)
            return carry
        each = lambda fn: [fn(u) for u in units]
        m_ab = each(lambda u: jnp.where(strict, _mm(u["la"], u["rb"], NT), 0.0))
        m_ak = each(lambda u: jnp.where(strict, _mm(u["la"], u["rk"], NT), 0.0))
        m_rb = each(lambda u: jnp.where(incl, _mm(u["lr"], u["rb"], NT), 0.0))
        m_rk = each(lambda u: jnp.where(incl, _mm(u["lr"], u["rk"], NT), 0.0))
        akv = [_mm(m, u["vb"]) for m, u in zip(m_ak, units)]
        rkv = [_mm(m, u["vb"]) for m, u in zip(m_rk, units)]
        apow = m_ab
        tinv = [eye + a_ for a_ in apow]
        for _ in range(nsteps):
            apow = [_mm(a_, a_) for a_ in apow]
            tinv = [t_ + _mm(a_, t_) for a_, t_ in zip(apow, tinv)]
        w = [_mm(t_, u["la"]) for t_, u in zip(tinv, units)]
        u0 = [_mm(t_, x_) for t_, x_ in zip(tinv, akv)]
        rbe = [u["rb"] * u["ecp"] for u in units]
        for i, u in enumerate(units):
            q_scr[u["idx"]] = u["lr"] + _mm(m_rb[i], w[i])
        for i, u in enumerate(units):
            o0_scr[u["idx"]] = _mm(m_rb[i], u0[i]) + rkv[i]
        for i, u in enumerate(units):
            gp_scr[u["idx"]] = _mm(w[i], rbe[i], TN)
        for i, u in enumerate(units):
            z_scr[u["idx"]] = _mm(u0[i], rbe[i], TN) + _mm(u["vb"], u["rk"] * u["ecp"], TN)
        return carry

    lax.fori_loop(0, nchunks // unroll, stage_b, 0)

    for ci in range(nchunks):
        for pp in range(lp):
            ls = slice(pp * LANES, (pp + 1) * LANES)
            idx = ci * lp + pp
            sidx = idx if nseq > 1 else pp
            s = s_ref[sidx]
            o_bd = _mm(q_scr[idx], s, NT) + o0_scr[idx]
            o_scr[ci * C:(ci + 1) * C, ls] = o_bd[0:C] + o_bd[C:n]
            s_ref[sidx] = s * ec_scr[ci, 0:1, ls] + _mm(s, gp_scr[idx]) + z_scr[idx]

    _rwkv_finish(o_scr, r_ref, v_ref, g_ref, k2_scr, vec_ref, og_ref)

    @pl.when(t == nt - 1)
    def _():
        for si in range(nseq):
            for pp in range(lp):
                sout_ref[si, 2 * pp] = s_ref[si * lp + pp, 0:HEAD, 0:HEAD]
                sout_ref[si, 2 * pp + 1] = s_ref[si * lp + pp, HEAD:LANES, HEAD:LANES]


def _rwkv_mix(rkvg, lora, wup, vec, s0, bn, tn_, C, tb, lp, unroll, nseq, og_dtype):
    m = rkvg.shape[1]
    nch = rkvg.shape[2]
    wl = lp * LANES
    ngrp = nch // wl
    nt = tn_ * nseq // tb
    nchunks = tb // C
    rank = lora.shape[2]
    nh = s0.shape[1]
    row = lambda b, t: b * nt + t

    def xspec(c):
        return pl.BlockSpec((1, tb, wl), lambda b, p, t: (c, row(b, t), p))

    def lspec(c):
        return pl.BlockSpec((1, tb, rank), lambda b, p, t: (c, row(b, t), 0))

    blk = [pltpu.VMEM((tb, wl), F32)]
    scratch = ([pltpu.VMEM((nseq * lp, LANES, LANES), F32)] + blk * 4
               + [pltpu.VMEM((nchunks * lp, 2 * C, LANES), F32), pltpu.VMEM((nchunks * lp, 2 * C, LANES), F32),
                  pltpu.VMEM((nchunks * lp, LANES, LANES), F32), pltpu.VMEM((nchunks * lp, LANES, LANES), F32),
                  pltpu.VMEM((nchunks, 8, wl), F32)] + blk)
    og, sout = pl.pallas_call(
        functools.partial(_rwkv_kernel, C=C, tb=tb, lp=lp, nt=nt, unroll=unroll, nseq=nseq),
        grid=(bn // nseq, ngrp, nt),
        in_specs=[
            xspec(0), xspec(1), xspec(2), xspec(3), lspec(0), lspec(1),
            pl.BlockSpec((2, rank, wl), lambda b, p, t: (0, 0, p)),
            pl.BlockSpec((8, wl), lambda b, p, t: (0, p)),
            pl.BlockSpec((nseq, 2 * lp, HEAD, HEAD), lambda b, p, t: (b, p, 0, 0)),
        ],
        out_specs=[
            pl.BlockSpec((tb, wl), lambda b, p, t: (row(b, t), p)),
            pl.BlockSpec((nseq, 2 * lp, HEAD, HEAD), lambda b, p, t: (b, p, 0, 0)),
        ],
        out_shape=[
            jax.ShapeDtypeStruct((m, nch), og_dtype),
            jax.ShapeDtypeStruct((bn, nh, HEAD, HEAD), F32),
        ],
        scratch_shapes=scratch,
        compiler_params=_params(("parallel", "parallel", "arbitrary")),
        name="rwkv7_chunked",
    )(rkvg, rkvg, rkvg, rkvg, lora, lora, wup, vec, s0)
    return og, sout


def _rwkv_layer(x2d, bn, tn_, g, j, shift, s0, mu, w_rkvg, w_down, w_up, w0, a0, k_k, k_a, r_k, ln_g, ln_b,
                w_out, tm, tn_cap, C, tb, lp, unroll, nseq, og_dtype):
    m, d = x2d.shape
    nblk = m // tm
    if tm <= tn_:
        starts = jnp.arange(nblk) * tm
        prev = _rms_rows(x2d[jnp.maximum(starts - 1, 0)], g)
        first = jnp.where((starts % tn_ == 0)[:, None], shift[starts // tn_], prev).reshape(nblk, 1, d)
        period = None
    else:
        assert nblk == 1
        first = jnp.zeros((bn, tn_, d), F32).at[:, 0].set(shift).reshape(1, m, d)
        period = tn_
    shift_out = _rms_rows(x2d[tn_ - 1::tn_], g)
    nch = w_rkvg.shape[2]
    rkvg = _proj(x2d, g, w_rkvg, 0, nch, tm, _pick_tn(nch, tn_cap), og_dtype, first=first, mu=mu[:4], period=period,
                 w_base=4 * j, nc=4)
    rank = w_down.shape[2]
    lora = _proj(x2d, g, w_down, 0, rank, tm, rank, F32, first=first, mu=mu[4:6], period=period, w_base=2 * j, nc=2)
    vec = jnp.stack([w0, a0, k_k, k_a, r_k.reshape(nch), ln_g, ln_b, jnp.zeros((nch,), F32)])
    og, sout = _rwkv_mix(rkvg, lora, w_up, vec, s0, bn, tn_, C, tb, lp, unroll, nseq, og_dtype)
    y = _outproj(og, w_out, j, x2d, tm, 512)
    return y, sout, shift_out


GLA_SUB = 16


def _gla_kernel(q_ref, k_ref, v_ref, gate_ref, gl_ref, wup_ref, bgk_ref, og_w_ref, s0_ref,
                og_ref, sout_ref, st_ref, z_scr, qd_scr, o_scr, ec_scr, *, C, tb, nt, nseq):
    t = pl.program_id(2)
    dk = q_ref.shape[-1]
    nchunks = tb // C
    sub = min(GLA_SUB, C)
    nsub = C // sub
    qscale = dk ** -0.5

    @pl.when(t == 0)
    def _():
        for si in range(nseq):
            st_ref[si] = s0_ref[si, 0].T

    tri = _tri(C)
    glog_all = -_softplus(-(_mm(gl_ref[...], wup_ref[...], NN, 3) + bgk_ref[...])) * (1.0 / C_GATE_NORM)
    chunks = range(nchunks)
    rows = [slice(c * C, (c + 1) * C) for c in chunks]
    glog = [glog_all[rs] for rs in rows]
    b = [_mm_exact_lhs(tri, gl_) for gl_ in glog]
    q = [q_ref[rs, :].astype(F32) * qscale for rs in rows]
    k = [k_ref[rs, :].astype(F32) for rs in rows]
    v = [v_ref[rs, :].astype(F32) for rs in rows]
    for c in chunks:
        bl = b[c][C - 1:C, :]
        qd_scr[rows[c], :] = q[c] * jnp.exp(b[c])
        ec_scr[c] = jnp.broadcast_to(jnp.exp(bl), (8, dk))
        z_scr[c] = _mm(v[c], k[c] * jnp.exp(bl - b[c]), TN)
    parts = [[] for _ in chunks]
    for i in range(nsub):
        r0 = i * sub
        nk = r0 + sub
        ar = lax.broadcasted_iota(jnp.int32, (sub, nk), 0) + r0
        ac = lax.broadcasted_iota(jnp.int32, (sub, nk), 1)
        att = []
        for c in chunks:
            ref = b[c][r0:r0 + 1, :] - glog[c][r0:r0 + 1, :]
            qi = q[c][r0:nk] * jnp.exp(b[c][r0:nk] - ref)
            ki = k[c][0:nk] * jnp.exp(ref - b[c][0:nk])
            att.append(jnp.where(ac <= ar, _mm(qi, ki, NT), 0.0))
        for c in chunks:
            parts[c].append(_mm(att[c], v[c][0:nk]))
    for c in chunks:
        o_scr[rows[c], :] = jnp.concatenate(parts[c], axis=0) if nsub > 1 else parts[c][0]

    for c in chunks:
        si = c if nseq > 1 else 0
        st = st_ref[si]
        o_scr[rows[c], :] = o_scr[rows[c], :] + _mm(qd_scr[rows[c], :], st, NT)
        st_ref[si] = st * ec_scr[c, 0:1, :] + z_scr[c]

    o = o_scr[...]
    on = o * lax.rsqrt(jnp.mean(o * o, axis=-1, keepdims=True) + NORM_EPS) * og_w_ref[...]
    og_ref[...] = (on * _silu(gate_ref[...].astype(F32))).astype(og_ref.dtype)

    @pl.when(t == nt - 1)
    def _():
        for si in range(nseq):
            sout_ref[si, 0] = st_ref[si].T


def _gla_mix(qkvg, gl, w_up, b_gk, o_g, s0, bn, tn_, C, tb, nseq, og_dtype):
    m = qkvg.shape[0]
    nh, dk, dv = s0.shape[1:]
    nt = tn_ * nseq // tb
    rank = gl.shape[1]
    row = lambda b, t: b * nt + t
    kcol = nh
    vcol = 2 * nh * dk // dv
    gcol = vcol + nh
    og, sout = pl.pallas_call(
        functools.partial(_gla_kernel, C=C, tb=tb, nt=nt, nseq=nseq),
        grid=(bn // nseq, nh, nt),
        in_specs=[
            pl.BlockSpec((tb, dk), lambda b, h, t: (row(b, t), h)),
            pl.BlockSpec((tb, dk), lambda b, h, t: (row(b, t), kcol + h)),
            pl.BlockSpec((tb, dv), lambda b, h, t: (row(b, t), vcol + h)),
            pl.BlockSpec((tb, dv), lambda b, h, t: (row(b, t), gcol + h)),
            pl.BlockSpec((tb, rank), lambda b, h, t: (row(b, t), 0)),
            pl.BlockSpec((rank, dk), lambda b, h, t: (0, h)),
            pl.BlockSpec((1, dk), lambda b, h, t: (0, h)),
            pl.BlockSpec((1, dv), lambda b, h, t: (0, 0)),
            pl.BlockSpec((nseq, 1, dk, dv), lambda b, h, t: (b, h, 0, 0)),
        ],
        out_specs=[
            pl.BlockSpec((tb, dv), lambda b, h, t: (row(b, t), h)),
            pl.BlockSpec((nseq, 1, dk, dv), lambda b, h, t: (b, h, 0, 0)),
        ],
        out_shape=[
            jax.ShapeDtypeStruct((m, nh * dv), og_dtype),
            jax.ShapeDtypeStruct((bn, nh, dk, dv), F32),
        ],
        scratch_shapes=[
            pltpu.VMEM((nseq, dv, dk), F32),
            pltpu.VMEM((tb // C, dv, dk), F32),
            pltpu.VMEM((tb, dk), F32),
            pltpu.VMEM((tb, dv), F32),
            pltpu.VMEM((tb // C, 8, dk), F32),
        ],
        compiler_params=_params(("parallel", "parallel", "arbitrary")),
        name="gla_chunked",
    )(qkvg, qkvg, qkvg, qkvg, gl, w_up, b_gk.reshape(1, -1), o_g.reshape(1, -1), s0)
    return og, sout


def _gla_layer(x2d, bn, tn_, g, j, s0, w3, w_gl, w_up, b_gk, o_g, w_out, tm, tn_cap, C, tb, nseq, og_dtype):
    nh, dk, dv = s0.shape[1:]
    nfused = 2 * nh * dk + 2 * nh * dv
    rank = w_up.shape[0]
    qkvg = _proj(x2d, g, w3, 0, nfused, tm, _pick_tn(nfused, tn_cap), og_dtype, w_base=j)[0]
    gl = _proj(x2d, g, w_gl, 0, rank, tm, rank, F32, w_base=j)[0]
    og, sout = _gla_mix(qkvg, gl, w_up, b_gk, o_g, s0, bn, tn_, C, tb, nseq, og_dtype)
    y = _outproj(og, w_out, j, x2d, tm, 512)
    return y, sout


PROMPT_TM = 1024
RWKV_CHUNK = 64
RWKV_TB = 512
RWKV_PAIRS = 4
RWKV_UNROLL = 4
STAT_PASSES = 1
GLA_CHUNK = 64
GLA_TB = 512
PROMPT_TN = 1024
LERP_TN = 512
SAMPLE_TN = 2048
SAMPLE_SEQS = 8
GLA_SAMPLE_SEQS = 4


def kernel(x_prompt, x_sample, cache_k_win, cache_v_win, state_wkv, state_shift, state_gla, norm_g, rel_bias, w_in_a, q_norm_g, k_norm_g, sinks, w_out_a, mu_b, w_rkvg_b, w_lora_down_b, w_lora_up_b, w0_b, a0_b, k_k_b, k_a_b, r_k_b, ln_x_g_b, ln_x_b_b, w_out_b, w_in_c, w_gk_up_c, b_gk_c, o_norm_g_c, w_out_c):
    bp, tp, d = x_prompt.shape
    bs, ts, _ = x_sample.shape
    depth = norm_g.shape[0]
    bf = lambda w: w.astype(BF16)
    nfused_c = w_in_c.shape[2] - w_gk_up_c.shape[1]
    w_in_a, w_out_a, w_out_b, w_out_c = bf(w_in_a), bf(w_out_a), bf(w_out_b), bf(w_out_c)
    w_rkvg_b = bf(w_rkvg_b).reshape((-1,) + w_rkvg_b.shape[2:])
    w_lora_down_b = bf(w_lora_down_b).reshape((-1,) + w_lora_down_b.shape[2:])
    w_gl_c = bf(w_in_c[:, :, nfused_c:])
    w_in_c = bf(w_in_c)
    xp = x_prompt.reshape(bp * tp, d)
    xs = x_sample.reshape(bs * ts, d)
    ms = bs * ts
    kwp, vwp, kws, vws, wkvp, shp, wkvs, shs, glap, glas = ([] for _ in range(10))
    for layer in range(depth):
        kind, j = layer % 3, layer // 3
        g = norm_g[layer]
        if kind == 0:
            wa = (w_in_a, q_norm_g[j], k_norm_g[j], sinks[j], w_out_a, rel_bias)
            xp, kp_, vp_ = _attn_layer(xp, bp, tp, g, j, *wa, None, None, PROMPT_TM, PROMPT_TN, BF16)
            xs, ks_, vs_ = _attn_layer(xs, bs, ts, g, j, *wa, cache_k_win[j], cache_v_win[j], ms, SAMPLE_TN, F32)
            kwp.append(kp_); vwp.append(vp_); kws.append(ks_); vws.append(vs_)
        elif kind == 1:
            wb = (mu_b[j], w_rkvg_b, w_lora_down_b, w_lora_up_b[j], w0_b[j], a0_b[j], k_k_b[j], k_a_b[j],
                  r_k_b[j], ln_x_g_b[j], ln_x_b_b[j], w_out_b)
            nh = w_rkvg_b.shape[2] // HEAD
            xp, sp_, lp_ = _rwkv_layer(xp, bp, tp, g, j, jnp.zeros((bp, d), F32), jnp.zeros((bp, nh, HEAD, HEAD), F32),
                                       *wb, PROMPT_TM, LERP_TN, RWKV_CHUNK, RWKV_TB, RWKV_PAIRS, RWKV_UNROLL, 1, BF16)
            xs, ss_, ls_ = _rwkv_layer(xs, bs, ts, g, j, state_shift[j], state_wkv[j], *wb, ms, SAMPLE_TN, ts,
                                       ts * SAMPLE_SEQS, RWKV_PAIRS, SAMPLE_SEQS, SAMPLE_SEQS, F32)
            wkvp.append(sp_); shp.append(lp_); wkvs.append(ss_); shs.append(ls_)
        else:
            wc = (w_in_c, w_gl_c, w_gk_up_c[j], b_gk_c[j], o_norm_g_c[j], w_out_c)
            xp, sp_ = _gla_layer(xp, bp, tp, g, j, jnp.zeros((bp,) + state_gla.shape[2:], F32), *wc,
                                 PROMPT_TM, PROMPT_TN, GLA_CHUNK, GLA_TB, 1, BF16)
            xs, ss_ = _gla_layer(xs, bs, ts, g, j, state_gla[j], *wc, ms, SAMPLE_TN, ts, ts * GLA_SAMPLE_SEQS,
                                 GLA_SAMPLE_SEQS, F32)
            glap.append(sp_); glas.append(ss_)
    return (xp.reshape(bp, tp, d), xs.reshape(bs, ts, d),
            jnp.stack(kwp), jnp.stack(vwp), jnp.stack(kws), jnp.stack(vws),
            jnp.stack(wkvp), jnp.stack(shp), jnp.stack(wkvs), jnp.stack(shs),
            jnp.stack(glap), jnp.stack(glas))
```

```python
import functools
import math

import jax
import jax.numpy as jnp
from jax import lax
from jax.experimental import pallas as pl
from jax.experimental.pallas import tpu as pltpu

F32 = jnp.float32
BF16 = jnp.bfloat16

NORM_EPS = 1e-6
HEAD = 64
LANES = 128
WINDOW = 128
N_BUCKETS = 32
MAX_EXACT = N_BUCKETS // 2
MAX_DISTANCE = 128
B_LN_EPS = 64e-5
C_GATE_NORM = 16.0
VMEM_LIMIT = 56 * 1024 * 1024

NN = (((1,), (0,)), ((), ()))
NT = (((1,), (1,)), ((), ()))
TN = (((0,), (0,)), ((), ()))


def _params(sem):
    return pltpu.CompilerParams(dimension_semantics=sem, vmem_limit_bytes=VMEM_LIMIT)


def _split(x):
    hi = x.astype(BF16)
    lo = (x - hi.astype(F32)).astype(BF16)
    return hi, lo


def _mm(a, b, dims=NN, passes=1):
    if passes == 1:
        return lax.dot_general(a.astype(BF16), b.astype(BF16), dims, preferred_element_type=F32)
    a_hi, a_lo = _split(a)
    b_hi, b_lo = _split(b)
    dg = functools.partial(lax.dot_general, dimension_numbers=dims, preferred_element_type=F32)
    return dg(a_hi, b_hi) + (dg(a_hi, b_lo) + dg(a_lo, b_hi))


def _mm_exact_lhs(a_bf16, b):
    b0 = b.astype(BF16)
    r1 = b - b0.astype(F32)
    b1 = r1.astype(BF16)
    b2 = (r1 - b1.astype(F32)).astype(BF16)
    dg = functools.partial(lax.dot_general, dimension_numbers=NN, preferred_element_type=F32)
    return dg(a_bf16, b0) + (dg(a_bf16, b1) + dg(a_bf16, b2))


def _softplus(z):
    return jnp.maximum(z, 0.0) + jnp.log1p(jnp.exp(-jnp.abs(z)))


def _silu(g):
    return g * jax.nn.sigmoid(g)


def _tri(c):
    r = lax.broadcasted_iota(jnp.int32, (c, c), 0)
    col = lax.broadcasted_iota(jnp.int32, (c, c), 1)
    return (col <= r).astype(BF16)


def _rms_rows_kernel(x_ref, g_ref, o_ref):
    x = x_ref[...]
    o_ref[...] = x * lax.rsqrt(jnp.mean(x * x, axis=-1, keepdims=True) + NORM_EPS) * g_ref[...]


def _rms_rows(rows, g):
    n, d = rows.shape
    npad = -(-n // 8) * 8
    rows_p = jnp.pad(rows, ((0, npad - n), (0, 0)))
    out = pl.pallas_call(
        _rms_rows_kernel,
        out_shape=jax.ShapeDtypeStruct((npad, d), F32),
        name="rms_rows",
    )(rows_p, g.reshape(1, d))
    return out[:n]


PROLOGUE_ROWS = 256


def _proj_kernel(*refs, lerp, period, tm):
    if lerp:
        x_ref, g_ref, w_ref, first_ref, mu_ref, o_ref, xm_ref, h_ref, d_ref = refs
    else:
        x_ref, g_ref, w_ref, o_ref, xm_ref = refs
    c = pl.program_id(1)
    j = pl.program_id(2)
    rc = min(PROLOGUE_ROWS, tm)

    def normed(c0):
        x = x_ref[c0:c0 + rc, :]
        return x * lax.rsqrt(jnp.mean(x * x, axis=-1, keepdims=True) + NORM_EPS) * g_ref[...]

    if lerp:
        @pl.when((c == 0) & (j == 0))
        def _():
            carry = first_ref[0, 0:1, :]
            for c0 in range(0, tm, rc):
                h = normed(c0)
                row = lax.broadcasted_iota(jnp.int32, h.shape, 0)
                hs = jnp.where(row == 0, carry, pltpu.roll(h, 1, 0))
                if period is not None:
                    hs = jnp.where(row % period == 0, first_ref[0, c0:c0 + rc, :], hs)
                carry = h[rc - 1:rc, :]
                h_ref[c0:c0 + rc, :] = h.astype(h_ref.dtype)
                d_ref[c0:c0 + rc, :] = (hs - h).astype(d_ref.dtype)

        @pl.when(j == 0)
        def _():
            for c0 in range(0, tm, rc):
                xm_ref[c0:c0 + rc, :] = (h_ref[c0:c0 + rc, :].astype(F32)
                                         + d_ref[c0:c0 + rc, :].astype(F32) * mu_ref[0]).astype(BF16)
    else:
        @pl.when(j == 0)
        def _():
            for c0 in range(0, tm, rc):
                xm_ref[c0:c0 + rc, :] = normed(c0).astype(BF16)

    o_ref[0] = jnp.dot(xm_ref[...], w_ref[0], preferred_element_type=F32).astype(o_ref.dtype)


def _pick_tn(n, cap):
    if n < LANES:
        return n
    units = n // LANES
    return LANES * max(u for u in range(1, units + 1) if units % u == 0 and u * LANES <= cap)


def _proj(x2d, g, w3, col_off, n_out, tm, tn, out_dtype, first=None, mu=None, period=None, w_base=0, nc=1):
    m, d = x2d.shape
    lerp = first is not None
    assert m % tm == 0 and n_out % tn == 0 and col_off % tn == 0
    joff = col_off // tn
    in_specs = [
        pl.BlockSpec((tm, d), lambda i, c, j: (i, 0)),
        pl.BlockSpec((1, d), lambda i, c, j: (0, 0)),
        pl.BlockSpec((1, d, tn), lambda i, c, j: (w_base + c, 0, joff + j)),
    ]
    args = [x2d, g.reshape(1, d), w3]
    if lerp:
        fr = first.shape[1]
        in_specs += [
            pl.BlockSpec((1, fr, d), lambda i, c, j: (i, 0, 0)),
            pl.BlockSpec((1, 1, d), lambda i, c, j: (c, 0, 0)),
        ]
        args += [first, mu.reshape(nc, 1, d)]
    return pl.pallas_call(
        functools.partial(_proj_kernel, lerp=lerp, period=period, tm=tm),
        grid=(m // tm, nc, n_out // tn),
        in_specs=in_specs,
        out_specs=pl.BlockSpec((1, tm, tn), lambda i, c, j: (c, i, j)),
        out_shape=jax.ShapeDtypeStruct((nc, m, n_out), out_dtype),
        scratch_shapes=[pltpu.VMEM((tm, d), BF16)] + ([pltpu.VMEM((tm, d), out_dtype)] * 2 if lerp else []),
        compiler_params=_params(("parallel", "arbitrary", "arbitrary")),
        name="proj",
    )(*args)


def _outproj_kernel(a_ref, w_ref, x_ref, o_ref):
    o_ref[...] = x_ref[...] + jnp.dot(a_ref[...].astype(BF16), w_ref[0], preferred_element_type=F32)


def _outproj(a2d, w3, w_base, x2d, tm, tn):
    m, kdim = a2d.shape
    n = w3.shape[2]
    return pl.pallas_call(
        _outproj_kernel,
        grid=(m // tm, n // tn),
        in_specs=[
            pl.BlockSpec((tm, kdim), lambda i, j: (i, 0)),
            pl.BlockSpec((1, kdim, tn), lambda i, j: (w_base, 0, j)),
            pl.BlockSpec((tm, tn), lambda i, j: (i, j)),
        ],
        out_specs=pl.BlockSpec((tm, tn), lambda i, j: (i, j)),
        out_shape=jax.ShapeDtypeStruct((m, n), F32),
        compiler_params=_params(("parallel", "arbitrary")),
        name="outproj",
    )(a2d, w3, x2d)


A_KV_HEADS = 8
A_GROUP = 8


def _attn_kernel(*refs, tq, prompt, nblocks):
    q_ref, kc_ref, vc_ref, kp_ref, vp_ref = refs[:5]
    gate_refs = refs[5:5 + A_KV_HEADS]
    (bias_ref, qg_ref, kg_ref, og_ref, kwin_ref, vwin_ref,
     k_scr, v_scr, s_scr, p_scr, qn_scr) = refs[5 + A_KV_HEADS:]
    return _attn_body(q_ref, kc_ref, vc_ref, kp_ref, vp_ref, gate_refs, bias_ref, qg_ref, kg_ref,
                      og_ref, kwin_ref, vwin_ref, k_scr, v_scr, s_scr, p_scr, qn_scr,
                      tq=tq, prompt=prompt, nblocks=nblocks)


def _attn_body(q_ref, kc_ref, vc_ref, kp_ref, vp_ref, gate_refs, bias_ref, qg_ref, kg_ref,
               og_ref, kwin_ref, vwin_ref, k_scr, v_scr, s_scr, p_scr, qn_scr, *, tq, prompt, nblocks):
    i = pl.program_id(1)
    tk = WINDOW + tq
    kg = kg_ref[...]
    qg = qg_ref[...] * (HEAD ** -0.5)
    nslab = kg.shape[1]
    jr = lax.broadcasted_iota(jnp.int32, (nslab, nslab), 0)
    jc = lax.broadcasted_iota(jnp.int32, (nslab, nslab), 1)
    mean_bd = jnp.where((jr // HEAD) == (jc // HEAD), 1.0 / HEAD, 0.0).astype(BF16)
    ones_v = jnp.ones((tk, HEAD), BF16)

    def slab_norm(x, gain):
        ms = jnp.dot((x * x).astype(BF16), mean_bd, preferred_element_type=F32)
        return x * lax.rsqrt(ms + NORM_EPS) * gain

    if prompt:
        k_scr[0:WINDOW, :] = slab_norm(kp_ref[...].astype(F32), kg)
        k_scr[WINDOW:tk, :] = slab_norm(kc_ref[...].astype(F32), kg)
        v_scr[0:WINDOW, :] = vp_ref[...].astype(F32)
    else:
        q_all = q_ref[...].astype(F32)
        slabs = [q_all[:, g * nslab:(g + 1) * nslab] for g in range(A_KV_HEADS)] + [kc_ref[...].astype(F32)]
        inv = lax.rsqrt(jnp.dot(jnp.concatenate([x * x for x in slabs], axis=0).astype(BF16), mean_bd,
                                preferred_element_type=F32) + NORM_EPS)
        for g in range(A_KV_HEADS):
            qn_scr[:, g * nslab:(g + 1) * nslab] = slabs[g] * inv[g * tq:(g + 1) * tq] * qg
        k_scr[WINDOW:tk, :] = slabs[A_KV_HEADS] * inv[A_KV_HEADS * tq:(A_KV_HEADS + 1) * tq] * kg
        for kh in range(A_KV_HEADS):
            k_scr[0:WINDOW, kh * HEAD:(kh + 1) * HEAD] = kp_ref[0, :, kh, :]
            v_scr[0:WINDOW, kh * HEAD:(kh + 1) * HEAD] = vp_ref[0, :, kh, :]
    v_scr[WINDOW:tk, :] = vc_ref[...].astype(F32)

    @pl.when(i == nblocks - 1)
    def _():
        keep = 0 if prompt else WINDOW - tq
        if keep:
            kwin_ref[0, 0:keep] = kp_ref[0, tq:WINDOW]
            vwin_ref[0, 0:keep] = vp_ref[0, tq:WINDOW]
        for kh in range(A_KV_HEADS):
            sl = slice(kh * HEAD, (kh + 1) * HEAD)
            kwin_ref[0, keep:WINDOW, kh, :] = k_scr[tk - WINDOW + keep:tk, sl]
            vwin_ref[0, keep:WINDOW, kh, :] = v_scr[tk - WINDOW + keep:tk, sl]

    k_scr[0:1, :] = jnp.zeros((1, k_scr.shape[1]), F32)
    v_scr[0:1, :] = jnp.zeros((1, v_scr.shape[1]), F32)

    c = lax.broadcasted_iota(jnp.int32, (tq, tk), 1)
    no_prev = (c >= WINDOW) | (c == 0)
    lo_half = lax.broadcasted_iota(jnp.int32, (tq, LANES), 1) < HEAD

    def group(kh, first_block):
        kk = k_scr[:, kh * HEAD:(kh + 1) * HEAD].astype(BF16)
        vv = v_scr[:, kh * HEAD:(kh + 1) * HEAD].astype(BF16)
        h0 = kh * A_GROUP
        lanes = slice(h0 * HEAD, (h0 + A_GROUP) * HEAD)
        qn = slab_norm(q_ref[:, lanes].astype(F32), qg) if prompt else qn_scr[:, lanes]
        qs = jnp.concatenate([qn[:, e * HEAD:(e + 1) * HEAD] for e in range(A_GROUP)], axis=0)
        s_scr[...] = lax.dot_general(qs.astype(BF16), kk, NT, preferred_element_type=F32)
        for e in range(A_GROUP):
            rs = slice(e * tq, (e + 1) * tq)
            s = s_scr[rs, :] + bias_ref[h0 + e]
            if first_block:
                s = jnp.where(no_prev, s, -jnp.inf)
            p_scr[rs, :] = jnp.exp(s - jnp.max(s, axis=-1, keepdims=True)).astype(p_scr.dtype)
        x = jnp.dot(p_scr[...].astype(BF16), jnp.concatenate([vv, ones_v, ones_v, vv], axis=-1),
                    preferred_element_type=F32)
        pairs = []
        for e in range(0, A_GROUP, 2):
            xe, xo = x[e * tq:(e + 1) * tq], x[(e + 1) * tq:(e + 2) * tq]
            num = jnp.where(lo_half, xe[:, 0:LANES], xo[:, LANES:2 * LANES])
            den = jnp.where(lo_half, xe[:, LANES:2 * LANES], xo[:, 0:LANES])
            pairs.append(num / den)
        o = jnp.concatenate(pairs, axis=-1)
        og_ref[:, lanes] = (o * _silu(gate_refs[kh][...].astype(F32))).astype(og_ref.dtype)

    if prompt:
        @pl.when(i == 0)
        def _():
            for kh in range(A_KV_HEADS):
                group(kh, True)

        @pl.when(i > 0)
        def _():
            for kh in range(A_KV_HEADS):
                group(kh, False)
    else:
        for kh in range(A_KV_HEADS):
            group(kh, False)


def _t5_bucket(dist):
    d = jnp.maximum(dist, 0)
    large = MAX_EXACT + (jnp.log(jnp.maximum(d, 1).astype(F32) / MAX_EXACT)
                         / math.log(MAX_DISTANCE / MAX_EXACT) * (N_BUCKETS - MAX_EXACT)).astype(jnp.int32)
    return jnp.where(d < MAX_EXACT, d, jnp.minimum(large, N_BUCKETS - 1))


def _attn_mix(qkvg, nq, rel_bias, q_g, k_g, sinks, bn, tn_, cache_k, cache_v, og_dtype):
    m = qkvg.shape[0]
    prompt = cache_k is None
    tq = WINDOW if prompt else tn_
    nb = tn_ // tq
    tk = WINDOW + tq
    col = jnp.arange(tk)[None, :]
    dist = WINDOW + jnp.arange(tq)[:, None] - col
    onehot = (_t5_bucket(dist)[None] == jnp.arange(N_BUCKETS)[:, None, None]).astype(F32)
    bias = jnp.einsum("bh,bqk->hqk", rel_bias.astype(F32), onehot, precision=lax.Precision.HIGHEST)
    bias = jnp.where(((dist >= 0) & (dist < WINDOW))[None], bias, -jnp.inf)
    bias = jnp.where((col == 0)[None], sinks.astype(F32)[:, None, None], bias)
    nkv = A_KV_HEADS * HEAD
    row = lambda b, i: b * nb + i
    kcol = nq // nkv
    gcol = kcol + 2
    win_shape = (1, WINDOW, A_KV_HEADS, HEAD)
    if prompt:
        kp_arr, vp_arr = qkvg, qkvg
        kp_spec = pl.BlockSpec((WINDOW, nkv), lambda b, i: (jnp.maximum(row(b, i) - 1, 0), kcol))
        vp_spec = pl.BlockSpec((WINDOW, nkv), lambda b, i: (jnp.maximum(row(b, i) - 1, 0), kcol + 1))
    else:
        kp_arr, vp_arr = cache_k, cache_v
        kp_spec = pl.BlockSpec(win_shape, lambda b, i: (b, 0, 0, 0))
        vp_spec = pl.BlockSpec(win_shape, lambda b, i: (b, 0, 0, 0))

    def gate_spec(kh):
        return pl.BlockSpec((tq, nkv), lambda b, i: (row(b, i), gcol + kh))

    og, kwin, vwin = pl.pallas_call(
        functools.partial(_attn_kernel, tq=tq, prompt=prompt, nblocks=nb),
        grid=(bn, nb),
        in_specs=[
            pl.BlockSpec((tq, nq), lambda b, i: (row(b, i), 0)),
            pl.BlockSpec((tq, nkv), lambda b, i: (row(b, i), kcol)),
            pl.BlockSpec((tq, nkv), lambda b, i: (row(b, i), kcol + 1)),
            kp_spec, vp_spec,
            *[gate_spec(kh) for kh in range(A_KV_HEADS)],
            pl.BlockSpec((nq // HEAD, tq, tk), lambda b, i: (0, 0, 0)),
            pl.BlockSpec((1, nkv), lambda b, i: (0, 0)),
            pl.BlockSpec((1, nkv), lambda b, i: (0, 0)),
        ],
        out_specs=[
            pl.BlockSpec((tq, nq), lambda b, i: (row(b, i), 0)),
            pl.BlockSpec(win_shape, lambda b, i: (b, 0, 0, 0)),
            pl.BlockSpec(win_shape, lambda b, i: (b, 0, 0, 0)),
        ],
        out_shape=[
            jax.ShapeDtypeStruct((m, nq), og_dtype),
            jax.ShapeDtypeStruct((bn,) + win_shape[1:], F32),
            jax.ShapeDtypeStruct((bn,) + win_shape[1:], F32),
        ],
        scratch_shapes=[
            pltpu.VMEM((tk, nkv), F32), pltpu.VMEM((tk, nkv), F32),
            pltpu.VMEM((A_GROUP * tq, tk), F32),
            pltpu.VMEM((A_GROUP * tq, tk), BF16 if tq % 16 == 0 else F32),
            pltpu.VMEM((8, LANES) if prompt else (tq, nq), F32),
        ],
        compiler_params=_params(("parallel", "arbitrary")),
        name="swa_attention",
    )(qkvg, qkvg, qkvg, kp_arr, vp_arr, *([qkvg] * A_KV_HEADS), bias,
      jnp.tile(q_g.astype(F32), A_GROUP).reshape(1, nkv), jnp.tile(k_g.astype(F32), A_KV_HEADS).reshape(1, nkv))
    return og, kwin, vwin


def _attn_layer(x2d, bn, tn_, g, j, w3, q_g, k_g, sinks, w_out, rel_bias, cache_k, cache_v, tm, tn_cap, og_dtype):
    nq = w_out.shape[1]
    nall = w3.shape[2]
    qkvg = _proj(x2d, g, w3, 0, nall, tm, _pick_tn(nall, tn_cap), og_dtype, w_base=j)[0]
    og, kwin, vwin = _attn_mix(qkvg, nq, rel_bias, q_g, k_g, sinks, bn, tn_, cache_k, cache_v, og_dtype)
    y = _outproj(og, w_out, j, x2d, tm, 512)
    return y, kwin, vwin


MXU_TILE = 256


def _head_sum_mats():
    jr = lax.broadcasted_iota(jnp.int32, (MXU_TILE, MXU_TILE), 0)
    jc = lax.broadcasted_iota(jnp.int32, (MXU_TILE, MXU_TILE), 1)
    ones_bd = ((jr // HEAD) == (jc // HEAD)).astype(F32)
    return ones_bd, ones_bd * (1.0 / HEAD)


def _per_slab(fn, x):
    return jnp.concatenate([fn(x[:, p * MXU_TILE:(p + 1) * MXU_TILE]) for p in range(x.shape[1] // MXU_TILE)],
                           axis=-1)


def _rwkv_token_terms(k_ref, lw_ref, la_ref, wup_ref, vec_ref, kk_scr, k2_scr, a_scr, ld_scr):
    vec = vec_ref[...]
    w0, a0, kk_w, ka_w = (vec[i:i + 1, :] for i in range(4))
    ones_bd, _ = _head_sum_mats()
    k = k_ref[0].astype(F32)
    xw = w0 + _mm(jnp.tanh(lw_ref[0]), wup_ref[0], NN, 3)
    ld_scr[...] = -math.exp(-0.5) * jax.nn.sigmoid(xw)
    a = jax.nn.sigmoid(a0 + _mm(la_ref[0], wup_ref[1], NN, 3))
    a_scr[...] = a
    kkr = k * kk_w
    ssq = _per_slab(lambda x: _mm(x, ones_bd, NN, STAT_PASSES), kkr * kkr)
    kk_scr[...] = kkr * lax.rsqrt(jnp.maximum(ssq, 1e-24))
    k2_scr[...] = k * (1.0 + (a - 1.0) * ka_w)


def _rwkv_finish(o_scr, r_ref, v_ref, g_ref, k2_scr, vec_ref, og_ref):
    vec = vec_ref[...]
    rk_w, ln_g, ln_b = (vec[i:i + 1, :] for i in range(4, 7))
    ones_bd, mean_bd = _head_sum_mats()
    o = o_scr[...]
    v = v_ref[0].astype(F32)
    mean = _per_slab(lambda x: _mm(x, mean_bd, NN, STAT_PASSES), o)
    dlt = o - mean
    var = _per_slab(lambda x: _mm(x, mean_bd, NN, STAT_PASSES), dlt * dlt)
    on = dlt * lax.rsqrt(var + B_LN_EPS) * ln_g + ln_b
    bonus = _per_slab(lambda x: _mm(x, ones_bd, NN, STAT_PASSES), r_ref[0].astype(F32) * k2_scr[...] * rk_w)
    on = on + bonus * v
    og_ref[...] = (on * _silu(g_ref[0].astype(F32))).astype(og_ref.dtype)


def _rwkv_kernel(r_ref, k_ref, v_ref, g_ref, lw_ref, la_ref, wup_ref, vec_ref, s0_ref,
                 og_ref, sout_ref,
                 s_ref, kk_scr, k2_scr, a_scr, ld_scr, q_scr, o0_scr, gp_scr, z_scr, ec_scr, o_scr,
                 *, C, tb, lp, nt, unroll, nseq):
    t = pl.program_id(2)
    n = 2 * C
    nchunks = tb // C
    lane = lax.broadcasted_iota(jnp.int32, (1, LANES), 1)
    lo = (lane < HEAD).astype(F32)
    hi = 1.0 - lo

    @pl.when(t == 0)
    def _():
        z = jnp.zeros((HEAD, HEAD), F32)
        for si in range(nseq):
            for pp in range(lp):
                s_ref[si * lp + pp, 0:HEAD, :] = jnp.concatenate([s0_ref[si, 2 * pp], z], axis=-1)
                s_ref[si * lp + pp, HEAD:LANES, :] = jnp.concatenate([z, s0_ref[si, 2 * pp + 1]], axis=-1)

    _rwkv_token_terms(k_ref, lw_ref, la_ref, wup_ref, vec_ref, kk_scr, k2_scr, a_scr, ld_scr)

    rr = lax.broadcasted_iota(jnp.int32, (n, n), 0)
    cc = lax.broadcasted_iota(jnp.int32, (n, n), 1)
    strict = cc < rr
    incl = cc <= rr
    eye = (cc == rr).astype(F32)
    tri = _tri(C)
    nsteps = int(math.log2(C)) - 1

    def stage_b(gi, carry):
        units = []
        for cu in range(unroll):
            ci = gi * unroll + cu
            rows = pl.ds(pl.multiple_of(ci * C, C), C)
            ld = ld_scr[rows, :]
            b = _mm_exact_lhs(tri, ld)
            eb = jnp.exp(b)
            enb = jnp.exp(-b)
            kk = kk_scr[rows, :]
            al = -kk * jnp.exp(b - ld)
            be = kk * a_scr[rows, :] * enb
            kt = k2_scr[rows, :] * enb
            rb = r_ref[0, rows, :].astype(F32) * eb
            v = v_ref[0, rows, :].astype(F32)
            e_c = eb[C - 1:C, :]
            ec_scr[ci] = jnp.broadcast_to(e_c, (8, e_c.shape[1]))
            for pp in range(lp):
                ls = slice(pp * LANES, (pp + 1) * LANES)
                cat = lambda x: jnp.concatenate([x[:, ls] * lo, x[:, ls] * hi], axis=0)
                units.append(dict(idx=ci * lp + pp, ecp=e_c[:, ls], la=cat(al), lr=cat(rb), rb=cat(be),
                                  rk=cat(kt), vb=cat(v)))
        nu = range(len(units))
        if n % LANES == 0:
            prod = [_mm(jnp.concatenate([u["la"], u["lr"]], axis=0),
                        jnp.concatenate([u["rb"], u["rk"]], axis=0), NT) for u in units]
            m_ab = [jnp.where(strict, x[0:n, 0:n], 0.0) for x in prod]
            m_ak = [jnp.where(strict, x[0:n, n:2 * n], 0.0) for x in prod]
            m_rb = [jnp.where(incl, x[n:2 * n, 0:n], 0.0) for x in prod]
            m_rk = [jnp.where(incl, x[n:2 * n, n:2 * n], 0.0) for x in prod]
            akv = [_mm(m_ak[i], units[i]["vb"]) for i in nu]
            tinv = [eye + a_ for a_ in m_ab]
            apow = [_mm(a_, a_) for a_ in m_ab]
            for step in range(nsteps):
                if step < nsteps - 1:
                    both = [_mm(apow[i], jnp.concatenate([tinv[i], apow[i]], axis=1)) for i in nu]
                    tinv = [tinv[i] + both[i][:, 0:n] for i in nu]
                    apow = [x[:, n:2 * n] for x in both]
                else:
                    tinv = [tinv[i] + _mm(apow[i], tinv[i]) for i in nu]
            wu = [_mm(tinv[i], jnp.concatenate([units[i]["la"], akv[i]], axis=1)) for i in nu]
            w = [x[:, 0:LANES] for x in wu]
            uv = [jnp.concatenate([wu[i][:, LANES:2 * LANES], units[i]["vb"]], axis=0) for i in nu]
            rbe = [u["rb"] * u["ecp"] for u in units]
            for i, u in enumerate(units):
                q_scr[u["idx"]] = u["lr"] + _mm(m_rb[i], w[i])
            for i, u in enumerate(units):
                o0_scr[u["idx"]] = _mm(jnp.concatenate([m_rb[i], m_rk[i]], axis=1), uv[i])
            for i, u in enumerate(units):
                gp_scr[u["idx"]] = _mm(w[i], rbe[i], TN)
            for i, u in enumerate(units):
                z_scr[u["idx"]] = _mm(uv[i], jnp.concatenate([rbe[i], u["rk"] * u["ecp"]], axis=0), TN)
            return carry
        each = lambda fn: [fn(u) for u in units]
        m_ab = each(lambda u: jnp.where(strict, _mm(u["la"], u["rb"], NT), 0.0))
        m_ak = each(lambda u: jnp.where(strict, _mm(u["la"], u["rk"], NT), 0.0))
        m_rb = each(lambda u: jnp.where(incl, _mm(u["lr"], u["rb"], NT), 0.0))
        m_rk = each(lambda u: jnp.where(incl, _mm(u["lr"], u["rk"], NT), 0.0))
        akv = [_mm(m, u["vb"]) for m, u in zip(m_ak, units)]
        rkv = [_mm(m, u["vb"]) for m, u in zip(m_rk, units)]
        apow = m_ab
        tinv = [eye + a_ for a_ in apow]
        for _ in range(nsteps):
            apow = [_mm(a_, a_) for a_ in apow]
            tinv = [t_ + _mm(a_, t_) for a_, t_ in zip(apow, tinv)]
        w = [_mm(t_, u["la"]) for t_, u in zip(tinv, units)]
        u0 = [_mm(t_, x_) for t_, x_ in zip(tinv, akv)]
        rbe = [u["rb"] * u["ecp"] for u in units]
        for i, u in enumerate(units):
            q_scr[u["idx"]] = u["lr"] + _mm(m_rb[i], w[i])
        for i, u in enumerate(units):
            o0_scr[u["idx"]] = _mm(m_rb[i], u0[i]) + rkv[i]
        for i, u in enumerate(units):
            gp_scr[u["idx"]] = _mm(w[i], rbe[i], TN)
        for i, u in enumerate(units):
            z_scr[u["idx"]] = _mm(u0[i], rbe[i], TN) + _mm(u["vb"], u["rk"] * u["ecp"], TN)
        return carry

    lax.fori_loop(0, nchunks // unroll, stage_b, 0)

    for ci in range(nchunks):
        for pp in range(lp):
            ls = slice(pp * LANES, (pp + 1) * LANES)
            idx = ci * lp + pp
            sidx = idx if nseq > 1 else pp
            s = s_ref[sidx]
            o_bd = _mm(q_scr[idx], s, NT) + o0_scr[idx]
            o_scr[ci * C:(ci + 1) * C, ls] = o_bd[0:C] + o_bd[C:n]
            s_ref[sidx] = s * ec_scr[ci, 0:1, ls] + _mm(s, gp_scr[idx]) + z_scr[idx]

    _rwkv_finish(o_scr, r_ref, v_ref, g_ref, k2_scr, vec_ref, og_ref)

    @pl.when(t == nt - 1)
    def _():
        for si in range(nseq):
            for pp in range(lp):
                sout_ref[si, 2 * pp] = s_ref[si * lp + pp, 0:HEAD, 0:HEAD]
                sout_ref[si, 2 * pp + 1] = s_ref[si * lp + pp, HEAD:LANES, HEAD:LANES]


def _rwkv_mix(rkvg, lora, wup, vec, s0, bn, tn_, C, tb, lp, unroll, nseq, og_dtype):
    m = rkvg.shape[1]
    nch = rkvg.shape[2]
    wl = lp * LANES
    ngrp = nch // wl
    nt = tn_ * nseq // tb
    nchunks = tb // C
    rank = lora.shape[2]
    nh = s0.shape[1]
    row = lambda b, t: b * nt + t

    def xspec(c):
        return pl.BlockSpec((1, tb, wl), lambda b, p, t: (c, row(b, t), p))

    def lspec(c):
        return pl.BlockSpec((1, tb, rank), lambda b, p, t: (c, row(b, t), 0))

    blk = [pltpu.VMEM((tb, wl), F32)]
    scratch = ([pltpu.VMEM((nseq * lp, LANES, LANES), F32)] + blk * 4
               + [pltpu.VMEM((nchunks * lp, 2 * C, LANES), F32), pltpu.VMEM((nchunks * lp, 2 * C, LANES), F32),
                  pltpu.VMEM((nchunks * lp, LANES, LANES), F32), pltpu.VMEM((nchunks * lp, LANES, LANES), F32),
                  pltpu.VMEM((nchunks, 8, wl), F32)] + blk)
    og, sout = pl.pallas_call(
        functools.partial(_rwkv_kernel, C=C, tb=tb, lp=lp, nt=nt, unroll=unroll, nseq=nseq),
        grid=(bn // nseq, ngrp, nt),
        in_specs=[
            xspec(0), xspec(1), xspec(2), xspec(3), lspec(0), lspec(1),
            pl.BlockSpec((2, rank, wl), lambda b, p, t: (0, 0, p)),
            pl.BlockSpec((8, wl), lambda b, p, t: (0, p)),
            pl.BlockSpec((nseq, 2 * lp, HEAD, HEAD), lambda b, p, t: (b, p, 0, 0)),
        ],
        out_specs=[
            pl.BlockSpec((tb, wl), lambda b, p, t: (row(b, t), p)),
            pl.BlockSpec((nseq, 2 * lp, HEAD, HEAD), lambda b, p, t: (b, p, 0, 0)),
        ],
        out_shape=[
            jax.ShapeDtypeStruct((m, nch), og_dtype),
            jax.ShapeDtypeStruct((bn, nh, HEAD, HEAD), F32),
        ],
        scratch_shapes=scratch,
        compiler_params=_params(("parallel", "parallel", "arbitrary")),
        name="rwkv7_chunked",
    )(rkvg, rkvg, rkvg, rkvg, lora, lora, wup, vec, s0)
    return og, sout


def _rwkv_layer(x2d, bn, tn_, g, j, shift, s0, mu, w_rkvg, w_down, w_up, w0, a0, k_k, k_a, r_k, ln_g, ln_b,
                w_out, tm, tn_cap, C, tb, lp, unroll, nseq, og_dtype):
    m, d = x2d.shape
    nblk = m // tm
    if tm <= tn_:
        starts = jnp.arange(nblk) * tm
        prev = _rms_rows(x2d[jnp.maximum(starts - 1, 0)], g)
        first = jnp.where((starts % tn_ == 0)[:, None], shift[starts // tn_], prev).reshape(nblk, 1, d)
        period = None
    else:
        assert nblk == 1
        first = jnp.zeros((bn, tn_, d), F32).at[:, 0].set(shift).reshape(1, m, d)
        period = tn_
    shift_out = _rms_rows(x2d[tn_ - 1::tn_], g)
    nch = w_rkvg.shape[2]
    rkvg = _proj(x2d, g, w_rkvg, 0, nch, tm, _pick_tn(nch, tn_cap), og_dtype, first=first, mu=mu[:4], period=period,
                 w_base=4 * j, nc=4)
    rank = w_down.shape[2]
    lora = _proj(x2d, g, w_down, 0, rank, tm, rank, F32, first=first, mu=mu[4:6], period=period, w_base=2 * j, nc=2)
    vec = jnp.stack([w0, a0, k_k, k_a, r_k.reshape(nch), ln_g, ln_b, jnp.zeros((nch,), F32)])
    og, sout = _rwkv_mix(rkvg, lora, w_up, vec, s0, bn, tn_, C, tb, lp, unroll, nseq, og_dtype)
    y = _outproj(og, w_out, j, x2d, tm, 512)
    return y, sout, shift_out


GLA_SUB = 16


def _gla_kernel(q_ref, k_ref, v_ref, gate_ref, gl_ref, wup_ref, bgk_ref, og_w_ref, s0_ref,
                og_ref, sout_ref, st_ref, z_scr, qd_scr, o_scr, ec_scr, *, C, tb, nt, nseq):
    t = pl.program_id(2)
    dk = q_ref.shape[-1]
    nchunks = tb // C
    sub = min(GLA_SUB, C)
    nsub = C // sub
    qscale = dk ** -0.5

    @pl.when(t == 0)
    def _():
        for si in range(nseq):
            st_ref[si] = s0_ref[si, 0]

    tri = _tri(C)
    glog_all = -_softplus(-(_mm(gl_ref[...], wup_ref[...], NN, 3) + bgk_ref[...])) * (1.0 / C_GATE_NORM)
    chunks = range(nchunks)
    rows = [slice(c * C, (c + 1) * C) for c in chunks]
    glog = [glog_all[rs] for rs in rows]
    b = [_mm_exact_lhs(tri, gl_) for gl_ in glog]
    q = [q_ref[rs, :].astype(F32) * qscale for rs in rows]
    k = [k_ref[rs, :].astype(F32) for rs in rows]
    v = [v_ref[rs, :].astype(F32) for rs in rows]
    for c in chunks:
        bl = b[c][C - 1:C, :]
        qd_scr[rows[c], :] = q[c] * jnp.exp(b[c])
        ec_scr[c] = jnp.broadcast_to(jnp.exp(bl), (8, dk))
        z_scr[c] = _mm(k[c] * jnp.exp(bl - b[c]), v[c], TN)
    parts = [[] for _ in chunks]
    for i in range(nsub):
        r0 = i * sub
        nk = r0 + sub
        ar = lax.broadcasted_iota(jnp.int32, (sub, nk), 0) + r0
        ac = lax.broadcasted_iota(jnp.int32, (sub, nk), 1)
        att = []
        for c in chunks:
            ref = b[c][r0:r0 + 1, :] - glog[c][r0:r0 + 1, :]
            qi = q[c][r0:nk] * jnp.exp(b[c][r0:nk] - ref)
            ki = k[c][0:nk] * jnp.exp(ref - b[c][0:nk])
            att.append(jnp.where(ac <= ar, _mm(qi, ki, NT), 0.0))
        for c in chunks:
            parts[c].append(_mm(att[c], v[c][0:nk]))
    for c in chunks:
        o_scr[rows[c], :] = jnp.concatenate(parts[c], axis=0) if nsub > 1 else parts[c][0]

    for c in chunks:
        si = c if nseq > 1 else 0
        st = st_ref[si]
        o_scr[rows[c], :] = o_scr[rows[c], :] + _mm(qd_scr[rows[c], :], st)
        st_ref[si] = st * jnp.transpose(ec_scr[c])[:, 0:1] + z_scr[c]

    o = o_scr[...]
    on = o * lax.rsqrt(jnp.mean(o * o, axis=-1, keepdims=True) + NORM_EPS) * og_w_ref[...]
    og_ref[...] = (on * _silu(gate_ref[...].astype(F32))).astype(og_ref.dtype)

    @pl.when(t == nt - 1)
    def _():
        for si in range(nseq):
            sout_ref[si, 0] = st_ref[si]


def _gla_mix(qkvg, gl, w_up, b_gk, o_g, s0, bn, tn_, C, tb, nseq, og_dtype):
    m = qkvg.shape[0]
    nh, dk, dv = s0.shape[1:]
    nt = tn_ * nseq // tb
    rank = gl.shape[1]
    row = lambda b, t: b * nt + t
    kcol = nh
    vcol = 2 * nh * dk // dv
    gcol = vcol + nh
    og, sout = pl.pallas_call(
        functools.partial(_gla_kernel, C=C, tb=tb, nt=nt, nseq=nseq),
        grid=(bn // nseq, nh, nt),
        in_specs=[
            pl.BlockSpec((tb, dk), lambda b, h, t: (row(b, t), h)),
            pl.BlockSpec((tb, dk), lambda b, h, t: (row(b, t), kcol + h)),
            pl.BlockSpec((tb, dv), lambda b, h, t: (row(b, t), vcol + h)),
            pl.BlockSpec((tb, dv), lambda b, h, t: (row(b, t), gcol + h)),
            pl.BlockSpec((tb, rank), lambda b, h, t: (row(b, t), 0)),
            pl.BlockSpec((rank, dk), lambda b, h, t: (0, h)),
            pl.BlockSpec((1, dk), lambda b, h, t: (0, h)),
            pl.BlockSpec((1, dv), lambda b, h, t: (0, 0)),
            pl.BlockSpec((nseq, 1, dk, dv), lambda b, h, t: (b, h, 0, 0)),
        ],
        out_specs=[
            pl.BlockSpec((tb, dv), lambda b, h, t: (row(b, t), h)),
            pl.BlockSpec((nseq, 1, dk, dv), lambda b, h, t: (b, h, 0, 0)),
        ],
        out_shape=[
            jax.ShapeDtypeStruct((m, nh * dv), og_dtype),
            jax.ShapeDtypeStruct((bn, nh, dk, dv), F32),
        ],
        scratch_shapes=[
            pltpu.VMEM((nseq, dk, dv), F32),
            pltpu.VMEM((tb // C, dk, dv), F32),
            pltpu.VMEM((tb, dk), F32),
            pltpu.VMEM((tb, dv), F32),
            pltpu.VMEM((tb // C, 8, dk), F32),
        ],
        compiler_params=_params(("parallel", "parallel", "arbitrary")),
        name="gla_chunked",
    )(qkvg, qkvg, qkvg, qkvg, gl, w_up, b_gk.reshape(1, -1), o_g.reshape(1, -1), s0)
    return og, sout


def _gla_layer(x2d, bn, tn_, g, j, s0, w3, w_gl, w_up, b_gk, o_g, w_out, tm, tn_cap, C, tb, nseq, og_dtype):
    nh, dk, dv = s0.shape[1:]
    nfused = 2 * nh * dk + 2 * nh * dv
    rank = w_up.shape[0]
    qkvg = _proj(x2d, g, w3, 0, nfused, tm, _pick_tn(nfused, tn_cap), og_dtype, w_base=j)[0]
    gl = _proj(x2d, g, w_gl, 0, rank, tm, rank, F32, w_base=j)[0]
    og, sout = _gla_mix(qkvg, gl, w_up, b_gk, o_g, s0, bn, tn_, C, tb, nseq, og_dtype)
    y = _outproj(og, w_out, j, x2d, tm, 512)
    return y, sout


PROMPT_TM = 1024
RWKV_CHUNK = 64
RWKV_TB = 512
RWKV_PAIRS = 4
RWKV_UNROLL = 4
STAT_PASSES = 1
GLA_CHUNK = 64
GLA_TB = 512
PROMPT_TN = 1024
LERP_TN = 1024
SAMPLE_TN = 2048
SAMPLE_SEQS = 8
GLA_SAMPLE_SEQS = 8


def kernel(x_prompt, x_sample, cache_k_win, cache_v_win, state_wkv, state_shift, state_gla, norm_g, rel_bias, w_in_a, q_norm_g, k_norm_g, sinks, w_out_a, mu_b, w_rkvg_b, w_lora_down_b, w_lora_up_b, w0_b, a0_b, k_k_b, k_a_b, r_k_b, ln_x_g_b, ln_x_b_b, w_out_b, w_in_c, w_gk_up_c, b_gk_c, o_norm_g_c, w_out_c):
    bp, tp, d = x_prompt.shape
    bs, ts, _ = x_sample.shape
    depth = norm_g.shape[0]
    bf = lambda w: w.astype(BF16)
    nfused_c = w_in_c.shape[2] - w_gk_up_c.shape[1]
    w_in_a, w_out_a, w_out_b, w_out_c = bf(w_in_a), bf(w_out_a), bf(w_out_b), bf(w_out_c)
    w_rkvg_b = bf(w_rkvg_b).reshape((-1,) + w_rkvg_b.shape[2:])
    w_lora_down_b = bf(w_lora_down_b).reshape((-1,) + w_lora_down_b.shape[2:])
    w_gl_c = bf(w_in_c[:, :, nfused_c:])
    w_in_c = bf(w_in_c)
    xp = x_prompt.reshape(bp * tp, d)
    xs = x_sample.reshape(bs * ts, d)
    ms = bs * ts
    kwp, vwp, kws, vws, wkvp, shp, wkvs, shs, glap, glas = ([] for _ in range(10))
    for layer in range(depth):
        kind, j = layer % 3, layer // 3
        g = norm_g[layer]
        if kind == 0:
            wa = (w_in_a, q_norm_g[j], k_norm_g[j], sinks[j], w_out_a, rel_bias)
            xp, kp_, vp_ = _attn_layer(xp, bp, tp, g, j, *wa, None, None, PROMPT_TM, PROMPT_TN, BF16)
            xs, ks_, vs_ = _attn_layer(xs, bs, ts, g, j, *wa, cache_k_win[j], cache_v_win[j], ms, SAMPLE_TN, F32)
            kwp.append(kp_); vwp.append(vp_); kws.append(ks_); vws.append(vs_)
        elif kind == 1:
            wb = (mu_b[j], w_rkvg_b, w_lora_down_b, w_lora_up_b[j], w0_b[j], a0_b[j], k_k_b[j], k_a_b[j],
                  r_k_b[j], ln_x_g_b[j], ln_x_b_b[j], w_out_b)
            nh = w_rkvg_b.shape[2] // HEAD
            xp, sp_, lp_ = _rwkv_layer(xp, bp, tp, g, j, jnp.zeros((bp, d), F32), jnp.zeros((bp, nh, HEAD, HEAD), F32),
                                       *wb, PROMPT_TM, LERP_TN, RWKV_CHUNK, RWKV_TB, RWKV_PAIRS, RWKV_UNROLL, 1, BF16)
            xs, ss_, ls_ = _rwkv_layer(xs, bs, ts, g, j, state_shift[j], state_wkv[j], *wb, ms, SAMPLE_TN, ts,
                                       ts * SAMPLE_SEQS, RWKV_PAIRS, SAMPLE_SEQS, SAMPLE_SEQS, F32)
            wkvp.append(sp_); shp.append(lp_); wkvs.append(ss_); shs.append(ls_)
        else:
            wc = (w_in_c, w_gl_c, w_gk_up_c[j], b_gk_c[j], o_norm_g_c[j], w_out_c)
            xp, sp_ = _gla_layer(xp, bp, tp, g, j, jnp.zeros((bp,) + state_gla.shape[2:], F32), *wc,
                                 PROMPT_TM, PROMPT_TN, GLA_CHUNK, GLA_TB, 1, BF16)
            xs, ss_ = _gla_layer(xs, bs, ts, g, j, state_gla[j], *wc, ms, SAMPLE_TN, ts, ts * GLA_SAMPLE_SEQS,
                                 GLA_SAMPLE_SEQS, F32)
            glap.append(sp_); glas.append(ss_)
    return (xp.reshape(bp, tp, d), xs.reshape(bs, ts, d),
            jnp.stack(kwp), jnp.stack(vwp), jnp.stack(kws), jnp.stack(vws),
            jnp.stack(wkvp), jnp.stack(shp), jnp.stack(wkvs), jnp.stack(shs),
            jnp.stack(glap), jnp.stack(glas))
```

```python
import functools
import math

import jax
import jax.numpy as jnp
from jax import lax
from jax.experimental import pallas as pl
from jax.experimental.pallas import tpu as pltpu

F32 = jnp.float32
BF16 = jnp.bfloat16

NORM_EPS = 1e-6
HEAD = 64
LANES = 128
SUBLANES = 8
OUT_TN = 512
WINDOW = 128
N_BUCKETS = 32
MAX_EXACT = N_BUCKETS // 2
MAX_DISTANCE = 128
B_LN_EPS = 64e-5
C_GATE_NORM = 16.0
VMEM_LIMIT = 56 * 1024 * 1024

NN = (((1,), (0,)), ((), ()))
NT = (((1,), (1,)), ((), ()))
TN = (((0,), (0,)), ((), ()))


def _params(sem):
    return pltpu.CompilerParams(dimension_semantics=sem, vmem_limit_bytes=VMEM_LIMIT)


def _split(x):
    hi = x.astype(BF16)
    lo = (x - hi.astype(F32)).astype(BF16)
    return hi, lo


def _mm(a, b, dims=NN, passes=1):
    if passes == 1:
        return lax.dot_general(a.astype(BF16), b.astype(BF16), dims, preferred_element_type=F32)
    a_hi, a_lo = _split(a)
    b_hi, b_lo = _split(b)
    dg = functools.partial(lax.dot_general, dimension_numbers=dims, preferred_element_type=F32)
    return dg(a_hi, b_hi) + (dg(a_hi, b_lo) + dg(a_lo, b_hi))


def _mm_exact_lhs(a_bf16, b):
    b0 = b.astype(BF16)
    r1 = b - b0.astype(F32)
    b1 = r1.astype(BF16)
    b2 = (r1 - b1.astype(F32)).astype(BF16)
    dg = functools.partial(lax.dot_general, dimension_numbers=NN, preferred_element_type=F32)
    return dg(a_bf16, b0) + (dg(a_bf16, b1) + dg(a_bf16, b2))


def _softplus(z):
    return jnp.maximum(z, 0.0) + jnp.log1p(jnp.exp(-jnp.abs(z)))


def _silu(g):
    return g * jax.nn.sigmoid(g)


def _tri(c):
    r = lax.broadcasted_iota(jnp.int32, (c, c), 0)
    col = lax.broadcasted_iota(jnp.int32, (c, c), 1)
    return (col <= r).astype(BF16)


def _rms_rows_kernel(x_ref, g_ref, o_ref):
    x = x_ref[...]
    o_ref[...] = x * lax.rsqrt(jnp.mean(x * x, axis=-1, keepdims=True) + NORM_EPS) * g_ref[...]


def _rms_rows(rows, g):
    n, d = rows.shape
    npad = -(-n // 8) * 8
    rows_p = jnp.pad(rows, ((0, npad - n), (0, 0)))
    out = pl.pallas_call(
        _rms_rows_kernel,
        out_shape=jax.ShapeDtypeStruct((npad, d), F32),
        name="rms_rows",
    )(rows_p, g.reshape(1, d))
    return out[:n]


PROLOGUE_ROWS = 256


def _proj_kernel(*refs, lerp, period, tm):
    if lerp:
        x_ref, g_ref, w_ref, first_ref, mu_ref, o_ref, xm_ref, h_ref, d_ref = refs
    else:
        x_ref, g_ref, w_ref, o_ref, xm_ref = refs
    c = pl.program_id(1)
    j = pl.program_id(2)
    rc = min(PROLOGUE_ROWS, tm)

    def normed(c0):
        x = x_ref[c0:c0 + rc, :]
        return x * lax.rsqrt(jnp.mean(x * x, axis=-1, keepdims=True) + NORM_EPS) * g_ref[...]

    if lerp:
        @pl.when((c == 0) & (j == 0))
        def _():
            carry = first_ref[0, 0:1, :]
            for c0 in range(0, tm, rc):
                h = normed(c0)
                row = lax.broadcasted_iota(jnp.int32, h.shape, 0)
                hs = jnp.where(row == 0, carry, pltpu.roll(h, 1, 0))
                if period is not None:
                    hs = jnp.where(row % period == 0, first_ref[0, c0:c0 + rc, :], hs)
                carry = h[rc - 1:rc, :]
                h_ref[c0:c0 + rc, :] = h.astype(h_ref.dtype)
                d_ref[c0:c0 + rc, :] = (hs - h).astype(d_ref.dtype)

        @pl.when(j == 0)
        def _():
            for c0 in range(0, tm, rc):
                xm_ref[c0:c0 + rc, :] = (h_ref[c0:c0 + rc, :].astype(F32)
                                         + d_ref[c0:c0 + rc, :].astype(F32) * mu_ref[0]).astype(BF16)
    else:
        @pl.when(j == 0)
        def _():
            for c0 in range(0, tm, rc):
                xm_ref[c0:c0 + rc, :] = normed(c0).astype(BF16)

    o_ref[0] = jnp.dot(xm_ref[...], w_ref[0], preferred_element_type=F32).astype(o_ref.dtype)


def _pick_tn(n, cap):
    if n < LANES:
        return n
    units = n // LANES
    return LANES * max(u for u in range(1, units + 1) if units % u == 0 and u * LANES <= cap)


def _proj(x2d, g, w3, col_off, n_out, tm, tn, out_dtype, first=None, mu=None, period=None, w_base=0, nc=1):
    m, d = x2d.shape
    lerp = first is not None
    assert m % tm == 0 and n_out % tn == 0 and col_off % tn == 0
    joff = col_off // tn
    in_specs = [
        pl.BlockSpec((tm, d), lambda i, c, j: (i, 0)),
        pl.BlockSpec((1, d), lambda i, c, j: (0, 0)),
        pl.BlockSpec((1, d, tn), lambda i, c, j: (w_base + c, 0, joff + j)),
    ]
    args = [x2d, g.reshape(1, d), w3]
    if lerp:
        fr = first.shape[1]
        in_specs += [
            pl.BlockSpec((1, fr, d), lambda i, c, j: (i, 0, 0)),
            pl.BlockSpec((1, 1, d), lambda i, c, j: (c, 0, 0)),
        ]
        args += [first, mu.reshape(nc, 1, d)]
    return pl.pallas_call(
        functools.partial(_proj_kernel, lerp=lerp, period=period, tm=tm),
        grid=(m // tm, nc, n_out // tn),
        in_specs=in_specs,
        out_specs=pl.BlockSpec((1, tm, tn), lambda i, c, j: (c, i, j)),
        out_shape=jax.ShapeDtypeStruct((nc, m, n_out), out_dtype),
        scratch_shapes=[pltpu.VMEM((tm, d), BF16)] + ([pltpu.VMEM((tm, d), out_dtype)] * 2 if lerp else []),
        compiler_params=_params(("parallel", "arbitrary", "arbitrary")),
        name="proj",
    )(*args)


def _outproj_kernel(a_ref, w_ref, x_ref, o_ref):
    o_ref[...] = x_ref[...] + jnp.dot(a_ref[...].astype(BF16), w_ref[0], preferred_element_type=F32)


def _outproj(a2d, w3, w_base, x2d, tm, tn):
    m, kdim = a2d.shape
    n = w3.shape[2]
    return pl.pallas_call(
        _outproj_kernel,
        grid=(m // tm, n // tn),
        in_specs=[
            pl.BlockSpec((tm, kdim), lambda i, j: (i, 0)),
            pl.BlockSpec((1, kdim, tn), lambda i, j: (w_base, 0, j)),
            pl.BlockSpec((tm, tn), lambda i, j: (i, j)),
        ],
        out_specs=pl.BlockSpec((tm, tn), lambda i, j: (i, j)),
        out_shape=jax.ShapeDtypeStruct((m, n), F32),
        compiler_params=_params(("parallel", "arbitrary")),
        name="outproj",
    )(a2d, w3, x2d)


A_KV_HEADS = 8
A_GROUP = 8


def _attn_kernel(*refs, tq, prompt, nblocks):
    q_ref, kc_ref, vc_ref, kp_ref, vp_ref = refs[:5]
    gate_refs = refs[5:5 + A_KV_HEADS]
    (bias_ref, qg_ref, kg_ref, og_ref, kwin_ref, vwin_ref,
     k_scr, v_scr, s_scr, p_scr, qn_scr) = refs[5 + A_KV_HEADS:]
    return _attn_body(q_ref, kc_ref, vc_ref, kp_ref, vp_ref, gate_refs, bias_ref, qg_ref, kg_ref,
                      og_ref, kwin_ref, vwin_ref, k_scr, v_scr, s_scr, p_scr, qn_scr,
                      tq=tq, prompt=prompt, nblocks=nblocks)


def _attn_body(q_ref, kc_ref, vc_ref, kp_ref, vp_ref, gate_refs, bias_ref, qg_ref, kg_ref,
               og_ref, kwin_ref, vwin_ref, k_scr, v_scr, s_scr, p_scr, qn_scr, *, tq, prompt, nblocks):
    i = pl.program_id(1)
    tk = WINDOW + tq
    kg = kg_ref[...]
    qg = qg_ref[...] * (HEAD ** -0.5)
    nslab = kg.shape[1]
    jr = lax.broadcasted_iota(jnp.int32, (nslab, nslab), 0)
    jc = lax.broadcasted_iota(jnp.int32, (nslab, nslab), 1)
    mean_bd = jnp.where((jr // HEAD) == (jc // HEAD), 1.0 / HEAD, 0.0).astype(BF16)
    ones_v = jnp.ones((tk, HEAD), BF16)

    def slab_norm(x, gain):
        ms = jnp.dot((x * x).astype(BF16), mean_bd, preferred_element_type=F32)
        return x * lax.rsqrt(ms + NORM_EPS) * gain

    if prompt:
        k_scr[0:WINDOW, :] = slab_norm(kp_ref[...].astype(F32), kg)
        k_scr[WINDOW:tk, :] = slab_norm(kc_ref[...].astype(F32), kg)
        v_scr[0:WINDOW, :] = vp_ref[...].astype(F32)
    else:
        q_all = q_ref[...].astype(F32)
        slabs = [q_all[:, g * nslab:(g + 1) * nslab] for g in range(A_KV_HEADS)] + [kc_ref[...].astype(F32)]
        inv = lax.rsqrt(jnp.dot(jnp.concatenate([x * x for x in slabs], axis=0).astype(BF16), mean_bd,
                                preferred_element_type=F32) + NORM_EPS)
        for g in range(A_KV_HEADS):
            qn_scr[:, g * nslab:(g + 1) * nslab] = slabs[g] * inv[g * tq:(g + 1) * tq] * qg
        k_scr[WINDOW:tk, :] = slabs[A_KV_HEADS] * inv[A_KV_HEADS * tq:(A_KV_HEADS + 1) * tq] * kg
        for kh in range(A_KV_HEADS):
            k_scr[0:WINDOW, kh * HEAD:(kh + 1) * HEAD] = kp_ref[0, :, kh, :]
            v_scr[0:WINDOW, kh * HEAD:(kh + 1) * HEAD] = vp_ref[0, :, kh, :]
    v_scr[WINDOW:tk, :] = vc_ref[...].astype(F32)

    @pl.when(i == nblocks - 1)
    def _():
        keep = 0 if prompt else WINDOW - tq
        if keep:
            kwin_ref[0, 0:keep] = kp_ref[0, tq:WINDOW]
            vwin_ref[0, 0:keep] = vp_ref[0, tq:WINDOW]
        for kh in range(A_KV_HEADS):
            sl = slice(kh * HEAD, (kh + 1) * HEAD)
            kwin_ref[0, keep:WINDOW, kh, :] = k_scr[tk - WINDOW + keep:tk, sl]
            vwin_ref[0, keep:WINDOW, kh, :] = v_scr[tk - WINDOW + keep:tk, sl]

    k_scr[0:1, :] = jnp.zeros((1, k_scr.shape[1]), F32)
    v_scr[0:1, :] = jnp.zeros((1, v_scr.shape[1]), F32)

    c = lax.broadcasted_iota(jnp.int32, (tq, tk), 1)
    no_prev = (c >= WINDOW) | (c == 0)
    lo_half = lax.broadcasted_iota(jnp.int32, (tq, LANES), 1) < HEAD

    def group(kh, first_block):
        kk = k_scr[:, kh * HEAD:(kh + 1) * HEAD].astype(BF16)
        vv = v_scr[:, kh * HEAD:(kh + 1) * HEAD].astype(BF16)
        h0 = kh * A_GROUP
        lanes = slice(h0 * HEAD, (h0 + A_GROUP) * HEAD)
        qn = slab_norm(q_ref[:, lanes].astype(F32), qg) if prompt else qn_scr[:, lanes]
        qs = jnp.concatenate([qn[:, e * HEAD:(e + 1) * HEAD] for e in range(A_GROUP)], axis=0)
        s_scr[...] = lax.dot_general(qs.astype(BF16), kk, NT, preferred_element_type=F32)
        for e in range(A_GROUP):
            rs = slice(e * tq, (e + 1) * tq)
            s = s_scr[rs, :] + bias_ref[h0 + e]
            if first_block:
                s = jnp.where(no_prev, s, -jnp.inf)
            p_scr[rs, :] = jnp.exp(s - jnp.max(s, axis=-1, keepdims=True)).astype(p_scr.dtype)
        x = jnp.dot(p_scr[...].astype(BF16), jnp.concatenate([vv, ones_v, ones_v, vv], axis=-1),
                    preferred_element_type=F32)
        pairs = []
        for e in range(0, A_GROUP, 2):
            xe, xo = x[e * tq:(e + 1) * tq], x[(e + 1) * tq:(e + 2) * tq]
            num = jnp.where(lo_half, xe[:, 0:LANES], xo[:, LANES:2 * LANES])
            den = jnp.where(lo_half, xe[:, LANES:2 * LANES], xo[:, 0:LANES])
            pairs.append(num / den)
        o = jnp.concatenate(pairs, axis=-1)
        og_ref[:, lanes] = (o * _silu(gate_refs[kh][...].astype(F32))).astype(og_ref.dtype)

    if prompt:
        @pl.when(i == 0)
        def _():
            for kh in range(A_KV_HEADS):
                group(kh, True)

        @pl.when(i > 0)
        def _():
            for kh in range(A_KV_HEADS):
                group(kh, False)
    else:
        for kh in range(A_KV_HEADS):
            group(kh, False)


def _t5_bucket(dist):
    d = jnp.maximum(dist, 0)
    large = MAX_EXACT + (jnp.log(jnp.maximum(d, 1).astype(F32) / MAX_EXACT)
                         / math.log(MAX_DISTANCE / MAX_EXACT) * (N_BUCKETS - MAX_EXACT)).astype(jnp.int32)
    return jnp.where(d < MAX_EXACT, d, jnp.minimum(large, N_BUCKETS - 1))


def _attn_mix(qkvg, nq, rel_bias, q_g, k_g, sinks, bn, tn_, cache_k, cache_v, og_dtype):
    m = qkvg.shape[0]
    prompt = cache_k is None
    tq = WINDOW if prompt else tn_
    nb = tn_ // tq
    tk = WINDOW + tq
    col = jnp.arange(tk)[None, :]
    dist = WINDOW + jnp.arange(tq)[:, None] - col
    onehot = (_t5_bucket(dist)[None] == jnp.arange(N_BUCKETS)[:, None, None]).astype(F32)
    bias = jnp.einsum("bh,bqk->hqk", rel_bias.astype(F32), onehot, precision=lax.Precision.HIGHEST)
    bias = jnp.where(((dist >= 0) & (dist < WINDOW))[None], bias, -jnp.inf)
    bias = jnp.where((col == 0)[None], sinks.astype(F32)[:, None, None], bias)
    nkv = A_KV_HEADS * HEAD
    row = lambda b, i: b * nb + i
    kcol = nq // nkv
    gcol = kcol + 2
    win_shape = (1, WINDOW, A_KV_HEADS, HEAD)
    if prompt:
        kp_arr, vp_arr = qkvg, qkvg
        kp_spec = pl.BlockSpec((WINDOW, nkv), lambda b, i: (jnp.maximum(row(b, i) - 1, 0), kcol))
        vp_spec = pl.BlockSpec((WINDOW, nkv), lambda b, i: (jnp.maximum(row(b, i) - 1, 0), kcol + 1))
    else:
        kp_arr, vp_arr = cache_k, cache_v
        kp_spec = pl.BlockSpec(win_shape, lambda b, i: (b, 0, 0, 0))
        vp_spec = pl.BlockSpec(win_shape, lambda b, i: (b, 0, 0, 0))

    def gate_spec(kh):
        return pl.BlockSpec((tq, nkv), lambda b, i: (row(b, i), gcol + kh))

    og, kwin, vwin = pl.pallas_call(
        functools.partial(_attn_kernel, tq=tq, prompt=prompt, nblocks=nb),
        grid=(bn, nb),
        in_specs=[
            pl.BlockSpec((tq, nq), lambda b, i: (row(b, i), 0)),
            pl.BlockSpec((tq, nkv), lambda b, i: (row(b, i), kcol)),
            pl.BlockSpec((tq, nkv), lambda b, i: (row(b, i), kcol + 1)),
            kp_spec, vp_spec,
            *[gate_spec(kh) for kh in range(A_KV_HEADS)],
            pl.BlockSpec((nq // HEAD, tq, tk), lambda b, i: (0, 0, 0)),
            pl.BlockSpec((1, nkv), lambda b, i: (0, 0)),
            pl.BlockSpec((1, nkv), lambda b, i: (0, 0)),
        ],
        out_specs=[
            pl.BlockSpec((tq, nq), lambda b, i: (row(b, i), 0)),
            pl.BlockSpec(win_shape, lambda b, i: (b, 0, 0, 0)),
            pl.BlockSpec(win_shape, lambda b, i: (b, 0, 0, 0)),
        ],
        out_shape=[
            jax.ShapeDtypeStruct((m, nq), og_dtype),
            jax.ShapeDtypeStruct((bn,) + win_shape[1:], F32),
            jax.ShapeDtypeStruct((bn,) + win_shape[1:], F32),
        ],
        scratch_shapes=[
            pltpu.VMEM((tk, nkv), F32), pltpu.VMEM((tk, nkv), F32),
            pltpu.VMEM((A_GROUP * tq, tk), F32),
            pltpu.VMEM((A_GROUP * tq, tk), BF16 if tq % 16 == 0 else F32),
            pltpu.VMEM((8, LANES) if prompt else (tq, nq), F32),
        ],
        compiler_params=_params(("parallel", "arbitrary")),
        name="swa_attention",
    )(qkvg, qkvg, qkvg, kp_arr, vp_arr, *([qkvg] * A_KV_HEADS), bias,
      jnp.tile(q_g.astype(F32), A_GROUP).reshape(1, nkv), jnp.tile(k_g.astype(F32), A_KV_HEADS).reshape(1, nkv))
    return og, kwin, vwin


def _attn_layer(x2d, bn, tn_, g, j, w3, q_g, k_g, sinks, w_out, rel_bias, cache_k, cache_v, tm, tn_cap, og_dtype):
    nq = w_out.shape[1]
    nall = w3.shape[2]
    qkvg = _proj(x2d, g, w3, 0, nall, tm, _pick_tn(nall, tn_cap), og_dtype, w_base=j)[0]
    og, kwin, vwin = _attn_mix(qkvg, nq, rel_bias, q_g, k_g, sinks, bn, tn_, cache_k, cache_v, og_dtype)
    y = _outproj(og, w_out, j, x2d, tm, OUT_TN)
    return y, kwin, vwin


MXU_TILE = 256


def _head_sum_mats():
    jr = lax.broadcasted_iota(jnp.int32, (MXU_TILE, MXU_TILE), 0)
    jc = lax.broadcasted_iota(jnp.int32, (MXU_TILE, MXU_TILE), 1)
    ones_bd = ((jr // HEAD) == (jc // HEAD)).astype(F32)
    return ones_bd, ones_bd * (1.0 / HEAD)


def _per_slab(fn, x):
    return jnp.concatenate([fn(x[:, p * MXU_TILE:(p + 1) * MXU_TILE]) for p in range(x.shape[1] // MXU_TILE)],
                           axis=-1)


def _rwkv_token_terms(k_ref, lw_ref, la_ref, wup_ref, vec_ref, kk_scr, k2_scr, a_scr, ld_scr):
    vec = vec_ref[...]
    w0, a0, kk_w, ka_w = (vec[i:i + 1, :] for i in range(4))
    ones_bd, _ = _head_sum_mats()
    k = k_ref[0].astype(F32)
    xw = w0 + _mm(jnp.tanh(lw_ref[0]), wup_ref[0], NN, 3)
    ld_scr[...] = -math.exp(-0.5) * jax.nn.sigmoid(xw)
    a = jax.nn.sigmoid(a0 + _mm(la_ref[0], wup_ref[1], NN, 3))
    a_scr[...] = a
    kkr = k * kk_w
    ssq = _per_slab(lambda x: _mm(x, ones_bd, NN, STAT_PASSES), kkr * kkr)
    kk_scr[...] = kkr * lax.rsqrt(jnp.maximum(ssq, 1e-24))
    k2_scr[...] = k * (1.0 + (a - 1.0) * ka_w)


def _rwkv_finish(o_scr, r_ref, v_ref, g_ref, k2_scr, vec_ref, og_ref):
    vec = vec_ref[...]
    rk_w, ln_g, ln_b = (vec[i:i + 1, :] for i in range(4, 7))
    ones_bd, mean_bd = _head_sum_mats()
    o = o_scr[...]
    v = v_ref[0].astype(F32)
    mean = _per_slab(lambda x: _mm(x, mean_bd, NN, STAT_PASSES), o)
    dlt = o - mean
    var = _per_slab(lambda x: _mm(x, mean_bd, NN, STAT_PASSES), dlt * dlt)
    on = dlt * lax.rsqrt(var + B_LN_EPS) * ln_g + ln_b
    bonus = _per_slab(lambda x: _mm(x, ones_bd, NN, STAT_PASSES), r_ref[0].astype(F32) * k2_scr[...] * rk_w)
    on = on + bonus * v
    og_ref[...] = (on * _silu(g_ref[0].astype(F32))).astype(og_ref.dtype)


def _rwkv_kernel(r_ref, k_ref, v_ref, g_ref, lw_ref, la_ref, wup_ref, vec_ref, s0_ref,
                 og_ref, sout_ref,
                 s_ref, kk_scr, k2_scr, a_scr, ld_scr, q_scr, o0_scr, gp_scr, z_scr, ec_scr, o_scr,
                 *, C, tb, lp, nt, unroll, nseq):
    t = pl.program_id(2)
    n = 2 * C
    nchunks = tb // C
    lane = lax.broadcasted_iota(jnp.int32, (1, LANES), 1)
    lo = (lane < HEAD).astype(F32)
    hi = 1.0 - lo

    @pl.when(t == 0)
    def _():
        z = jnp.zeros((HEAD, HEAD), F32)
        for si in range(nseq):
            for pp in range(lp):
                s_ref[si * lp + pp, 0:HEAD, :] = jnp.concatenate([s0_ref[si, 2 * pp], z], axis=-1)
                s_ref[si * lp + pp, HEAD:LANES, :] = jnp.concatenate([z, s0_ref[si, 2 * pp + 1]], axis=-1)

    _rwkv_token_terms(k_ref, lw_ref, la_ref, wup_ref, vec_ref, kk_scr, k2_scr, a_scr, ld_scr)

    rr = lax.broadcasted_iota(jnp.int32, (n, n), 0)
    cc = lax.broadcasted_iota(jnp.int32, (n, n), 1)
    strict = cc < rr
    incl = cc <= rr
    eye = (cc == rr).astype(F32)
    tri = _tri(C)
    nsteps = int(math.log2(C)) - 1

    def stage_b(gi, carry):
        units = []
        for cu in range(unroll):
            ci = gi * unroll + cu
            rows = pl.ds(pl.multiple_of(ci * C, C), C)
            ld = ld_scr[rows, :]
            b = _mm_exact_lhs(tri, ld)
            eb = jnp.exp(b)
            enb = jnp.exp(-b)
            kk = kk_scr[rows, :]
            al = -kk * jnp.exp(b - ld)
            be = kk * a_scr[rows, :] * enb
            kt = k2_scr[rows, :] * enb
            rb = r_ref[0, rows, :].astype(F32) * eb
            v = v_ref[0, rows, :].astype(F32)
            e_c = eb[C - 1:C, :]
            ec_scr[ci] = jnp.broadcast_to(e_c, (SUBLANES, e_c.shape[1]))
            for pp in range(lp):
                ls = slice(pp * LANES, (pp + 1) * LANES)
                cat = lambda x: jnp.concatenate([x[:, ls] * lo, x[:, ls] * hi], axis=0)
                units.append(dict(idx=ci * lp + pp, ecp=e_c[:, ls], la=cat(al), lr=cat(rb), rb=cat(be),
                                  rk=cat(kt), vb=cat(v)))
        nu = range(len(units))
        if n % LANES == 0:
            prod = [_mm(jnp.concatenate([u["la"], u["lr"]], axis=0),
                        jnp.concatenate([u["rb"], u["rk"]], axis=0), NT) for u in units]
            m_ab = [jnp.where(strict, x[0:n, 0:n], 0.0) for x in prod]
            m_ak = [jnp.where(strict, x[0:n, n:2 * n], 0.0) for x in prod]
            m_rb = [jnp.where(incl, x[n:2 * n, 0:n], 0.0) for x in prod]
            m_rk = [jnp.where(incl, x[n:2 * n, n:2 * n], 0.0) for x in prod]
            akv = [_mm(m_ak[i], units[i]["vb"]) for i in nu]
            tinv = [eye + a_ for a_ in m_ab]
            apow = [_mm(a_, a_) for a_ in m_ab]
            for step in range(nsteps):
                if step < nsteps - 1:
                    both = [_mm(apow[i], jnp.concatenate([tinv[i], apow[i]], axis=1)) for i in nu]
                    tinv = [tinv[i] + both[i][:, 0:n] for i in nu]
                    apow = [x[:, n:2 * n] for x in both]
                else:
                    tinv = [tinv[i] + _mm(apow[i], tinv[i]) for i in nu]
            wu = [_mm(tinv[i], jnp.concatenate([units[i]["la"], akv[i]], axis=1)) for i in nu]
            w = [x[:, 0:LANES] for x in wu]
            uv = [jnp.concatenate([wu[i][:, LANES:2 * LANES], units[i]["vb"]], axis=0) for i in nu]
            rbe = [u["rb"] * u["ecp"] for u in units]
            for i, u in enumerate(units):
                q_scr[u["idx"]] = u["lr"] + _mm(m_rb[i], w[i])
            for i, u in enumerate(units):
                o0_scr[u["idx"]] = _mm(jnp.concatenate([m_rb[i], m_rk[i]], axis=1), uv[i])
            for i, u in enumerate(units):
                gp_scr[u["idx"]] = _mm(w[i], rbe[i], TN)
            for i, u in enumerate(units):
                z_scr[u["idx"]] = _mm(uv[i], jnp.concatenate([rbe[i], u["rk"] * u["ecp"]], axis=0), TN)
            return carry
        each = lambda fn: [fn(u) for u in units]
        m_ab = each(lambda u: jnp.where(strict, _mm(u["la"], u["rb"], NT), 0.0))
        m_ak = each(lambda u: jnp.where(strict, _mm(u["la"], u["rk"], NT), 0.0))
        m_rb = each(lambda u: jnp.where(incl, _mm(u["lr"], u["rb"], NT), 0.0))
        m_rk = each(lambda u: jnp.where(incl, _mm(u["lr"], u["rk"], NT), 0.0))
        akv = [_mm(m, u["vb"]) for m, u in zip(m_ak, units)]
        rkv = [_mm(m, u["vb"]) for m, u in zip(m_rk, units)]
        apow = m_ab
        tinv = [eye + a_ for a_ in apow]
        for _ in range(nsteps):
            apow = [_mm(a_, a_) for a_ in apow]
            tinv = [t_ + _mm(a_, t_) for a_, t_ in zip(apow, tinv)]
        w = [_mm(t_, u["la"]) for t_, u in zip(tinv, units)]
        u0 = [_mm(t_, x_) for t_, x_ in zip(tinv, akv)]
        rbe = [u["rb"] * u["ecp"] for u in units]
        for i, u in enumerate(units):
            q_scr[u["idx"]] = u["lr"] + _mm(m_rb[i], w[i])
        for i, u in enumerate(units):
            o0_scr[u["idx"]] = _mm(m_rb[i], u0[i]) + rkv[i]
        for i, u in enumerate(units):
            gp_scr[u["idx"]] = _mm(w[i], rbe[i], TN)
        for i, u in enumerate(units):
            z_scr[u["idx"]] = _mm(u0[i], rbe[i], TN) + _mm(u["vb"], u["rk"] * u["ecp"], TN)
        return carry

    lax.fori_loop(0, nchunks // unroll, stage_b, 0)

    for ci in range(nchunks):
        for pp in range(lp):
            ls = slice(pp * LANES, (pp + 1) * LANES)
            idx = ci * lp + pp
            sidx = idx if nseq > 1 else pp
            s = s_ref[sidx]
            o_bd = _mm(q_scr[idx], s, NT) + o0_scr[idx]
            o_scr[ci * C:(ci + 1) * C, ls] = o_bd[0:C] + o_bd[C:n]
            s_ref[sidx] = s * ec_scr[ci, 0:1, ls] + _mm(s, gp_scr[idx]) + z_scr[idx]

    _rwkv_finish(o_scr, r_ref, v_ref, g_ref, k2_scr, vec_ref, og_ref)

    @pl.when(t == nt - 1)
    def _():
        for si in range(nseq):
            for pp in range(lp):
                sout_ref[si, 2 * pp] = s_ref[si * lp + pp, 0:HEAD, 0:HEAD]
                sout_ref[si, 2 * pp + 1] = s_ref[si * lp + pp, HEAD:LANES, HEAD:LANES]


def _rwkv_mix(rkvg, lora, wup, vec, s0, bn, tn_, C, tb, lp, unroll, nseq, og_dtype):
    m = rkvg.shape[1]
    nch = rkvg.shape[2]
    wl = lp * LANES
    ngrp = nch // wl
    nt = tn_ * nseq // tb
    nchunks = tb // C
    rank = lora.shape[2]
    nh = s0.shape[1]
    row = lambda b, t: b * nt + t

    def xspec(c):
        return pl.BlockSpec((1, tb, wl), lambda b, p, t: (c, row(b, t), p))

    def lspec(c):
        return pl.BlockSpec((1, tb, rank), lambda b, p, t: (c, row(b, t), 0))

    blk = [pltpu.VMEM((tb, wl), F32)]
    scratch = ([pltpu.VMEM((nseq * lp, LANES, LANES), F32)] + blk * 4
               + [pltpu.VMEM((nchunks * lp, 2 * C, LANES), F32), pltpu.VMEM((nchunks * lp, 2 * C, LANES), F32),
                  pltpu.VMEM((nchunks * lp, LANES, LANES), F32), pltpu.VMEM((nchunks * lp, LANES, LANES), F32),
                  pltpu.VMEM((nchunks, SUBLANES, wl), F32)] + blk)
    og, sout = pl.pallas_call(
        functools.partial(_rwkv_kernel, C=C, tb=tb, lp=lp, nt=nt, unroll=unroll, nseq=nseq),
        grid=(bn // nseq, ngrp, nt),
        in_specs=[
            xspec(0), xspec(1), xspec(2), xspec(3), lspec(0), lspec(1),
            pl.BlockSpec((2, rank, wl), lambda b, p, t: (0, 0, p)),
            pl.BlockSpec((8, wl), lambda b, p, t: (0, p)),
            pl.BlockSpec((nseq, 2 * lp, HEAD, HEAD), lambda b, p, t: (b, p, 0, 0)),
        ],
        out_specs=[
            pl.BlockSpec((tb, wl), lambda b, p, t: (row(b, t), p)),
            pl.BlockSpec((nseq, 2 * lp, HEAD, HEAD), lambda b, p, t: (b, p, 0, 0)),
        ],
        out_shape=[
            jax.ShapeDtypeStruct((m, nch), og_dtype),
            jax.ShapeDtypeStruct((bn, nh, HEAD, HEAD), F32),
        ],
        scratch_shapes=scratch,
        compiler_params=_params(("parallel", "parallel", "arbitrary")),
        name="rwkv7_chunked",
    )(rkvg, rkvg, rkvg, rkvg, lora, lora, wup, vec, s0)
    return og, sout


def _rwkv_layer(x2d, bn, tn_, g, j, shift, s0, mu, w_rkvg, w_down, w_up, w0, a0, k_k, k_a, r_k, ln_g, ln_b,
                w_out, tm, tn_cap, C, tb, lp, unroll, nseq, og_dtype):
    m, d = x2d.shape
    nblk = m // tm
    if tm <= tn_:
        starts = jnp.arange(nblk) * tm
        prev = _rms_rows(x2d[jnp.maximum(starts - 1, 0)], g)
        first = jnp.where((starts % tn_ == 0)[:, None], shift[starts // tn_], prev).reshape(nblk, 1, d)
        period = None
    else:
        assert nblk == 1
        first = jnp.zeros((bn, tn_, d), F32).at[:, 0].set(shift).reshape(1, m, d)
        period = tn_
    shift_out = _rms_rows(x2d[tn_ - 1::tn_], g)
    nch = w_rkvg.shape[2]
    rkvg = _proj(x2d, g, w_rkvg, 0, nch, tm, _pick_tn(nch, tn_cap), og_dtype, first=first, mu=mu[:4], period=period,
                 w_base=4 * j, nc=4)
    rank = w_down.shape[2]
    lora = _proj(x2d, g, w_down, 0, rank, tm, rank, F32, first=first, mu=mu[4:6], period=period, w_base=2 * j, nc=2)
    vec = jnp.stack([w0, a0, k_k, k_a, r_k.reshape(nch), ln_g, ln_b, jnp.zeros((nch,), F32)])
    og, sout = _rwkv_mix(rkvg, lora, w_up, vec, s0, bn, tn_, C, tb, lp, unroll, nseq, og_dtype)
    y = _outproj(og, w_out, j, x2d, tm, OUT_TN)
    return y, sout, shift_out


GLA_SUB = 16


def _gla_kernel(q_ref, k_ref, v_ref, gate_ref, gl_ref, wup_ref, bgk_ref, og_w_ref, s0_ref,
                og_ref, sout_ref, st_ref, z_scr, qd_scr, o_scr, ec_scr, *, C, tb, nt, nseq):
    t = pl.program_id(2)
    dk = q_ref.shape[-1]
    nchunks = tb // C
    sub = min(GLA_SUB, C)
    nsub = C // sub
    qscale = dk ** -0.5

    @pl.when(t == 0)
    def _():
        for si in range(nseq):
            st_ref[si] = s0_ref[si, 0]

    tri = _tri(C)
    glog_all = -_softplus(-(_mm(gl_ref[...], wup_ref[...], NN, 3) + bgk_ref[...])) * (1.0 / C_GATE_NORM)
    chunks = range(nchunks)
    rows = [slice(c * C, (c + 1) * C) for c in chunks]
    glog = [glog_all[rs] for rs in rows]
    b = [_mm_exact_lhs(tri, gl_) for gl_ in glog]
    q = [q_ref[rs, :].astype(F32) * qscale for rs in rows]
    k = [k_ref[rs, :].astype(F32) for rs in rows]
    v = [v_ref[rs, :].astype(F32) for rs in rows]
    for c in chunks:
        bl = b[c][C - 1:C, :]
        qd_scr[rows[c], :] = q[c] * jnp.exp(b[c])
        ec_scr[c] = jnp.broadcast_to(jnp.exp(bl), (SUBLANES, dk))
        z_scr[c] = _mm(k[c] * jnp.exp(bl - b[c]), v[c], TN)
    parts = [[] for _ in chunks]
    for i in range(nsub):
        r0 = i * sub
        nk = r0 + sub
        ar = lax.broadcasted_iota(jnp.int32, (sub, nk), 0) + r0
        ac = lax.broadcasted_iota(jnp.int32, (sub, nk), 1)
        att = []
        for c in chunks:
            ref = b[c][r0:r0 + 1, :] - glog[c][r0:r0 + 1, :]
            qi = q[c][r0:nk] * jnp.exp(b[c][r0:nk] - ref)
            ki = k[c][0:nk] * jnp.exp(ref - b[c][0:nk])
            att.append(jnp.where(ac <= ar, _mm(qi, ki, NT), 0.0))
        for c in chunks:
            parts[c].append(_mm(att[c], v[c][0:nk]))
    for c in chunks:
        o_scr[rows[c], :] = jnp.concatenate(parts[c], axis=0) if nsub > 1 else parts[c][0]

    for c in chunks:
        si = c if nseq > 1 else 0
        st = st_ref[si]
        o_scr[rows[c], :] = o_scr[rows[c], :] + _mm(qd_scr[rows[c], :], st)
        st_ref[si] = st * jnp.transpose(ec_scr[c])[:, 0:1] + z_scr[c]

    o = o_scr[...]
    on = o * lax.rsqrt(jnp.mean(o * o, axis=-1, keepdims=True) + NORM_EPS) * og_w_ref[...]
    og_ref[...] = (on * _silu(gate_ref[...].astype(F32))).astype(og_ref.dtype)

    @pl.when(t == nt - 1)
    def _():
        for si in range(nseq):
            sout_ref[si, 0] = st_ref[si]


def _gla_mix(qkvg, gl, w_up, b_gk, o_g, s0, bn, tn_, C, tb, nseq, og_dtype):
    m = qkvg.shape[0]
    nh, dk, dv = s0.shape[1:]
    nt = tn_ * nseq // tb
    rank = gl.shape[1]
    row = lambda b, t: b * nt + t
    kcol = nh
    vcol = 2 * nh * dk // dv
    gcol = vcol + nh
    og, sout = pl.pallas_call(
        functools.partial(_gla_kernel, C=C, tb=tb, nt=nt, nseq=nseq),
        grid=(bn // nseq, nh, nt),
        in_specs=[
            pl.BlockSpec((tb, dk), lambda b, h, t: (row(b, t), h)),
            pl.BlockSpec((tb, dk), lambda b, h, t: (row(b, t), kcol + h)),
            pl.BlockSpec((tb, dv), lambda b, h, t: (row(b, t), vcol + h)),
            pl.BlockSpec((tb, dv), lambda b, h, t: (row(b, t), gcol + h)),
            pl.BlockSpec((tb, rank), lambda b, h, t: (row(b, t), 0)),
            pl.BlockSpec((rank, dk), lambda b, h, t: (0, h)),
            pl.BlockSpec((1, dk), lambda b, h, t: (0, h)),
            pl.BlockSpec((1, dv), lambda b, h, t: (0, 0)),
            pl.BlockSpec((nseq, 1, dk, dv), lambda b, h, t: (b, h, 0, 0)),
        ],
        out_specs=[
            pl.BlockSpec((tb, dv), lambda b, h, t: (row(b, t), h)),
            pl.BlockSpec((nseq, 1, dk, dv), lambda b, h, t: (b, h, 0, 0)),
        ],
        out_shape=[
            jax.ShapeDtypeStruct((m, nh * dv), og_dtype),
            jax.ShapeDtypeStruct((bn, nh, dk, dv), F32),
        ],
        scratch_shapes=[
            pltpu.VMEM((nseq, dk, dv), F32),
            pltpu.VMEM((tb // C, dk, dv), F32),
            pltpu.VMEM((tb, dk), F32),
            pltpu.VMEM((tb, dv), F32),
            pltpu.VMEM((tb // C, SUBLANES, dk), F32),
        ],
        compiler_params=_params(("parallel", "parallel", "arbitrary")),
        name="gla_chunked",
    )(qkvg, qkvg, qkvg, qkvg, gl, w_up, b_gk.reshape(1, -1), o_g.reshape(1, -1), s0)
    return og, sout


def _gla_layer(x2d, bn, tn_, g, j, s0, w3, w_gl, w_up, b_gk, o_g, w_out, tm, tn_cap, C, tb, nseq, og_dtype):
    nh, dk, dv = s0.shape[1:]
    nfused = 2 * nh * dk + 2 * nh * dv
    rank = w_up.shape[0]
    qkvg = _proj(x2d, g, w3, 0, nfused, tm, _pick_tn(nfused, tn_cap), og_dtype, w_base=j)[0]
    gl = _proj(x2d, g, w_gl, 0, rank, tm, rank, F32, w_base=j)[0]
    og, sout = _gla_mix(qkvg, gl, w_up, b_gk, o_g, s0, bn, tn_, C, tb, nseq, og_dtype)
    y = _outproj(og, w_out, j, x2d, tm, OUT_TN)
    return y, sout


PROMPT_TM = 1024
RWKV_CHUNK = 64
RWKV_TB = 512
RWKV_PAIRS = 4
RWKV_UNROLL = 4
STAT_PASSES = 1
GLA_CHUNK = 64
GLA_TB = 1024
PROMPT_TN = 1024
LERP_TN = 1024
SAMPLE_TN = 2048
SAMPLE_SEQS = 16
GLA_SAMPLE_SEQS = 8


def kernel(x_prompt, x_sample, cache_k_win, cache_v_win, state_wkv, state_shift, state_gla, norm_g, rel_bias, w_in_a, q_norm_g, k_norm_g, sinks, w_out_a, mu_b, w_rkvg_b, w_lora_down_b, w_lora_up_b, w0_b, a0_b, k_k_b, k_a_b, r_k_b, ln_x_g_b, ln_x_b_b, w_out_b, w_in_c, w_gk_up_c, b_gk_c, o_norm_g_c, w_out_c):
    bp, tp, d = x_prompt.shape
    bs, ts, _ = x_sample.shape
    depth = norm_g.shape[0]
    bf = lambda w: w.astype(BF16)
    nfused_c = w_in_c.shape[2] - w_gk_up_c.shape[1]
    w_in_a, w_out_a, w_out_b, w_out_c = bf(w_in_a), bf(w_out_a), bf(w_out_b), bf(w_out_c)
    w_rkvg_b = bf(w_rkvg_b).reshape((-1,) + w_rkvg_b.shape[2:])
    w_lora_down_b = bf(w_lora_down_b).reshape((-1,) + w_lora_down_b.shape[2:])
    w_gl_c = bf(w_in_c[:, :, nfused_c:])
    w_in_c = bf(w_in_c)
    xp = x_prompt.reshape(bp * tp, d)
    xs = x_sample.reshape(bs * ts, d)
    ms = bs * ts
    kwp, vwp, kws, vws, wkvp, shp, wkvs, shs, glap, glas = ([] for _ in range(10))
    for layer in range(depth):
        kind, j = layer % 3, layer // 3
        g = norm_g[layer]
        if kind == 0:
            wa = (w_in_a, q_norm_g[j], k_norm_g[j], sinks[j], w_out_a, rel_bias)
            xp, kp_, vp_ = _attn_layer(xp, bp, tp, g, j, *wa, None, None, PROMPT_TM, PROMPT_TN, BF16)
            xs, ks_, vs_ = _attn_layer(xs, bs, ts, g, j, *wa, cache_k_win[j], cache_v_win[j], ms, SAMPLE_TN, F32)
            kwp.append(kp_); vwp.append(vp_); kws.append(ks_); vws.append(vs_)
        elif kind == 1:
            wb = (mu_b[j], w_rkvg_b, w_lora_down_b, w_lora_up_b[j], w0_b[j], a0_b[j], k_k_b[j], k_a_b[j],
                  r_k_b[j], ln_x_g_b[j], ln_x_b_b[j], w_out_b)
            nh = w_rkvg_b.shape[2] // HEAD
            xp, sp_, lp_ = _rwkv_layer(xp, bp, tp, g, j, jnp.zeros((bp, d), F32), jnp.zeros((bp, nh, HEAD, HEAD), F32),
                                       *wb, PROMPT_TM, LERP_TN, RWKV_CHUNK, RWKV_TB, RWKV_PAIRS, RWKV_UNROLL, 1, BF16)
            xs, ss_, ls_ = _rwkv_layer(xs, bs, ts, g, j, state_shift[j], state_wkv[j], *wb, ms, SAMPLE_TN, ts,
                                       ts * SAMPLE_SEQS, RWKV_PAIRS, SAMPLE_SEQS, SAMPLE_SEQS, F32)
            wkvp.append(sp_); shp.append(lp_); wkvs.append(ss_); shs.append(ls_)
        else:
            wc = (w_in_c, w_gl_c, w_gk_up_c[j], b_gk_c[j], o_norm_g_c[j], w_out_c)
            xp, sp_ = _gla_layer(xp, bp, tp, g, j, jnp.zeros((bp,) + state_gla.shape[2:], F32), *wc,
                                 PROMPT_TM, PROMPT_TN, GLA_CHUNK, GLA_TB, 1, BF16)
            xs, ss_ = _gla_layer(xs, bs, ts, g, j, state_gla[j], *wc, ms, SAMPLE_TN, ts, ts * GLA_SAMPLE_SEQS,
                                 GLA_SAMPLE_SEQS, F32)
            glap.append(sp_); glas.append(ss_)
    return (xp.reshape(bp, tp, d), xs.reshape(bs, ts, d),
            jnp.stack(kwp), jnp.stack(vwp), jnp.stack(kws), jnp.stack(vws),
            jnp.stack(wkvp), jnp.stack(shp), jnp.stack(wkvs), jnp.stack(shs),
            jnp.stack(glap), jnp.stack(glas))
```

```python
import functools
import math

import jax
import jax.numpy as jnp
from jax import lax
from jax.experimental import pallas as pl
from jax.experimental.pallas import tpu as pltpu

F32 = jnp.float32
BF16 = jnp.bfloat16

NORM_EPS = 1e-6
HEAD = 64
LANES = 128
SUBLANES = 8
OUT_TN = 1024
WINDOW = 128
N_BUCKETS = 32
MAX_EXACT = N_BUCKETS // 2
MAX_DISTANCE = 128
B_LN_EPS = 64e-5
C_GATE_NORM = 16.0
VMEM_LIMIT = 56 * 1024 * 1024

NN = (((1,), (0,)), ((), ()))
NT = (((1,), (1,)), ((), ()))
TN = (((0,), (0,)), ((), ()))


def _params(sem):
    return pltpu.CompilerParams(dimension_semantics=sem, vmem_limit_bytes=VMEM_LIMIT)


def _split(x):
    hi = x.astype(BF16)
    lo = (x - hi.astype(F32)).astype(BF16)
    return hi, lo


def _mm(a, b, dims=NN, passes=1):
    if passes == 1:
        return lax.dot_general(a.astype(BF16), b.astype(BF16), dims, preferred_element_type=F32)
    a_hi, a_lo = _split(a)
    b_hi, b_lo = _split(b)
    dg = functools.partial(lax.dot_general, dimension_numbers=dims, preferred_element_type=F32)
    return dg(a_hi, b_hi) + (dg(a_hi, b_lo) + dg(a_lo, b_hi))


def _mm_exact_lhs(a_bf16, b):
    b0 = b.astype(BF16)
    r1 = b - b0.astype(F32)
    b1 = r1.astype(BF16)
    b2 = (r1 - b1.astype(F32)).astype(BF16)
    dg = functools.partial(lax.dot_general, dimension_numbers=NN, preferred_element_type=F32)
    return dg(a_bf16, b0) + (dg(a_bf16, b1) + dg(a_bf16, b2))


def _softplus(z):
    return jnp.maximum(z, 0.0) + jnp.log1p(jnp.exp(-jnp.abs(z)))


def _silu(g):
    return g * jax.nn.sigmoid(g)


def _tri(c):
    r = lax.broadcasted_iota(jnp.int32, (c, c), 0)
    col = lax.broadcasted_iota(jnp.int32, (c, c), 1)
    return (col <= r).astype(BF16)


def _rms_rows_kernel(x_ref, g_ref, o_ref):
    x = x_ref[...]
    o_ref[...] = x * lax.rsqrt(jnp.mean(x * x, axis=-1, keepdims=True) + NORM_EPS) * g_ref[...]


def _rms_rows(rows, g):
    n, d = rows.shape
    npad = -(-n // 8) * 8
    rows_p = jnp.pad(rows, ((0, npad - n), (0, 0)))
    out = pl.pallas_call(
        _rms_rows_kernel,
        out_shape=jax.ShapeDtypeStruct((npad, d), F32),
        name="rms_rows",
    )(rows_p, g.reshape(1, d))
    return out[:n]


PROLOGUE_ROWS = 256


def _proj_kernel(*refs, lerp, period, tm):
    if lerp:
        x_ref, g_ref, w_ref, first_ref, mu_ref, o_ref, xm_ref, h_ref, d_ref = refs
    else:
        x_ref, g_ref, w_ref, o_ref, xm_ref = refs
    c = pl.program_id(1)
    j = pl.program_id(2)
    rc = min(PROLOGUE_ROWS, tm)

    def normed(c0):
        x = x_ref[c0:c0 + rc, :]
        return x * lax.rsqrt(jnp.mean(x * x, axis=-1, keepdims=True) + NORM_EPS) * g_ref[...]

    if lerp:
        @pl.when((c == 0) & (j == 0))
        def _():
            carry = first_ref[0, 0:1, :]
            for c0 in range(0, tm, rc):
                h = normed(c0)
                row = lax.broadcasted_iota(jnp.int32, h.shape, 0)
                hs = jnp.where(row == 0, carry, pltpu.roll(h, 1, 0))
                if period is not None:
                    hs = jnp.where(row % period == 0, first_ref[0, c0:c0 + rc, :], hs)
                carry = h[rc - 1:rc, :]
                h_ref[c0:c0 + rc, :] = h.astype(h_ref.dtype)
                d_ref[c0:c0 + rc, :] = (hs - h).astype(d_ref.dtype)

        @pl.when(j == 0)
        def _():
            for c0 in range(0, tm, rc):
                xm_ref[c0:c0 + rc, :] = (h_ref[c0:c0 + rc, :].astype(F32)
                                         + d_ref[c0:c0 + rc, :].astype(F32) * mu_ref[0]).astype(BF16)
    else:
        @pl.when(j == 0)
        def _():
            for c0 in range(0, tm, rc):
                xm_ref[c0:c0 + rc, :] = normed(c0).astype(BF16)

    o_ref[0] = jnp.dot(xm_ref[...], w_ref[0], preferred_element_type=F32).astype(o_ref.dtype)


def _pick_tn(n, cap):
    if n < LANES:
        return n
    units = n // LANES
    return LANES * max(u for u in range(1, units + 1) if units % u == 0 and u * LANES <= cap)


def _proj(x2d, g, w3, col_off, n_out, tm, tn, out_dtype, first=None, mu=None, period=None, w_base=0, nc=1):
    m, d = x2d.shape
    lerp = first is not None
    assert m % tm == 0 and n_out % tn == 0 and col_off % tn == 0
    joff = col_off // tn
    in_specs = [
        pl.BlockSpec((tm, d), lambda i, c, j: (i, 0)),
        pl.BlockSpec((1, d), lambda i, c, j: (0, 0)),
        pl.BlockSpec((1, d, tn), lambda i, c, j: (w_base + c, 0, joff + j)),
    ]
    args = [x2d, g.reshape(1, d), w3]
    if lerp:
        fr = first.shape[1]
        in_specs += [
            pl.BlockSpec((1, fr, d), lambda i, c, j: (i, 0, 0)),
            pl.BlockSpec((1, 1, d), lambda i, c, j: (c, 0, 0)),
        ]
        args += [first, mu.reshape(nc, 1, d)]
    return pl.pallas_call(
        functools.partial(_proj_kernel, lerp=lerp, period=period, tm=tm),
        grid=(m // tm, nc, n_out // tn),
        in_specs=in_specs,
        out_specs=pl.BlockSpec((1, tm, tn), lambda i, c, j: (c, i, j)),
        out_shape=jax.ShapeDtypeStruct((nc, m, n_out), out_dtype),
        scratch_shapes=[pltpu.VMEM((tm, d), BF16)] + ([pltpu.VMEM((tm, d), out_dtype)] * 2 if lerp else []),
        compiler_params=_params(("parallel", "arbitrary", "arbitrary")),
        name="proj",
    )(*args)


def _outproj_kernel(a_ref, w_ref, x_ref, o_ref):
    o_ref[...] = x_ref[...] + jnp.dot(a_ref[...].astype(BF16), w_ref[0], preferred_element_type=F32)


def _outproj(a2d, w3, w_base, x2d, tm, tn):
    m, kdim = a2d.shape
    n = w3.shape[2]
    return pl.pallas_call(
        _outproj_kernel,
        grid=(m // tm, n // tn),
        in_specs=[
            pl.BlockSpec((tm, kdim), lambda i, j: (i, 0)),
            pl.BlockSpec((1, kdim, tn), lambda i, j: (w_base, 0, j)),
            pl.BlockSpec((tm, tn), lambda i, j: (i, j)),
        ],
        out_specs=pl.BlockSpec((tm, tn), lambda i, j: (i, j)),
        out_shape=jax.ShapeDtypeStruct((m, n), F32),
        compiler_params=_params(("parallel", "arbitrary")),
        name="outproj",
    )(a2d, w3, x2d)


A_KV_HEADS = 8
A_GROUP = 8


def _attn_kernel(*refs, tq, prompt, nblocks):
    q_ref, kc_ref, vc_ref, kp_ref, vp_ref = refs[:5]
    gate_refs = refs[5:5 + A_KV_HEADS]
    (bias_ref, qg_ref, kg_ref, og_ref, kwin_ref, vwin_ref,
     k_scr, v_scr, s_scr, p_scr, qn_scr) = refs[5 + A_KV_HEADS:]
    return _attn_body(q_ref, kc_ref, vc_ref, kp_ref, vp_ref, gate_refs, bias_ref, qg_ref, kg_ref,
                      og_ref, kwin_ref, vwin_ref, k_scr, v_scr, s_scr, p_scr, qn_scr,
                      tq=tq, prompt=prompt, nblocks=nblocks)


def _attn_body(q_ref, kc_ref, vc_ref, kp_ref, vp_ref, gate_refs, bias_ref, qg_ref, kg_ref,
               og_ref, kwin_ref, vwin_ref, k_scr, v_scr, s_scr, p_scr, qn_scr, *, tq, prompt, nblocks):
    i = pl.program_id(1)
    tk = WINDOW + tq
    kg = kg_ref[...]
    qg = qg_ref[...] * (HEAD ** -0.5)
    nslab = kg.shape[1]
    jr = lax.broadcasted_iota(jnp.int32, (nslab, nslab), 0)
    jc = lax.broadcasted_iota(jnp.int32, (nslab, nslab), 1)
    mean_bd = jnp.where((jr // HEAD) == (jc // HEAD), 1.0 / HEAD, 0.0).astype(BF16)
    ones_v = jnp.ones((tk, HEAD), BF16)

    def slab_norm(x, gain):
        ms = jnp.dot((x * x).astype(BF16), mean_bd, preferred_element_type=F32)
        return x * lax.rsqrt(ms + NORM_EPS) * gain

    if prompt:
        k_scr[0:WINDOW, :] = slab_norm(kp_ref[...].astype(F32), kg)
        k_scr[WINDOW:tk, :] = slab_norm(kc_ref[...].astype(F32), kg)
        v_scr[0:WINDOW, :] = vp_ref[...].astype(F32)
    else:
        q_all = q_ref[...].astype(F32)
        slabs = [q_all[:, g * nslab:(g + 1) * nslab] for g in range(A_KV_HEADS)] + [kc_ref[...].astype(F32)]
        inv = lax.rsqrt(jnp.dot(jnp.concatenate([x * x for x in slabs], axis=0).astype(BF16), mean_bd,
                                preferred_element_type=F32) + NORM_EPS)
        for g in range(A_KV_HEADS):
            qn_scr[:, g * nslab:(g + 1) * nslab] = slabs[g] * inv[g * tq:(g + 1) * tq] * qg
        k_scr[WINDOW:tk, :] = slabs[A_KV_HEADS] * inv[A_KV_HEADS * tq:(A_KV_HEADS + 1) * tq] * kg
        for kh in range(A_KV_HEADS):
            k_scr[0:WINDOW, kh * HEAD:(kh + 1) * HEAD] = kp_ref[0, :, kh, :]
            v_scr[0:WINDOW, kh * HEAD:(kh + 1) * HEAD] = vp_ref[0, :, kh, :]
    v_scr[WINDOW:tk, :] = vc_ref[...].astype(F32)

    @pl.when(i == nblocks - 1)
    def _():
        keep = 0 if prompt else WINDOW - tq
        if keep:
            kwin_ref[0, 0:keep] = kp_ref[0, tq:WINDOW]
            vwin_ref[0, 0:keep] = vp_ref[0, tq:WINDOW]
        for kh in range(A_KV_HEADS):
            sl = slice(kh * HEAD, (kh + 1) * HEAD)
            kwin_ref[0, keep:WINDOW, kh, :] = k_scr[tk - WINDOW + keep:tk, sl]
            vwin_ref[0, keep:WINDOW, kh, :] = v_scr[tk - WINDOW + keep:tk, sl]

    k_scr[0:1, :] = jnp.zeros((1, k_scr.shape[1]), F32)
    v_scr[0:1, :] = jnp.zeros((1, v_scr.shape[1]), F32)

    c = lax.broadcasted_iota(jnp.int32, (tq, tk), 1)
    no_prev = (c >= WINDOW) | (c == 0)
    lo_half = lax.broadcasted_iota(jnp.int32, (tq, LANES), 1) < HEAD

    def group(kh, first_block):
        kk = k_scr[:, kh * HEAD:(kh + 1) * HEAD].astype(BF16)
        vv = v_scr[:, kh * HEAD:(kh + 1) * HEAD].astype(BF16)
        h0 = kh * A_GROUP
        lanes = slice(h0 * HEAD, (h0 + A_GROUP) * HEAD)
        qn = slab_norm(q_ref[:, lanes].astype(F32), qg) if prompt else qn_scr[:, lanes]
        qs = jnp.concatenate([qn[:, e * HEAD:(e + 1) * HEAD] for e in range(A_GROUP)], axis=0)
        s_scr[...] = lax.dot_general(qs.astype(BF16), kk, NT, preferred_element_type=F32)
        for e in range(A_GROUP):
            rs = slice(e * tq, (e + 1) * tq)
            s = s_scr[rs, :] + bias_ref[h0 + e]
            if first_block:
                s = jnp.where(no_prev, s, -jnp.inf)
            p_scr[rs, :] = jnp.exp(s - jnp.max(s, axis=-1, keepdims=True)).astype(p_scr.dtype)
        x = jnp.dot(p_scr[...].astype(BF16), jnp.concatenate([vv, ones_v, ones_v, vv], axis=-1),
                    preferred_element_type=F32)
        pairs = []
        for e in range(0, A_GROUP, 2):
            xe, xo = x[e * tq:(e + 1) * tq], x[(e + 1) * tq:(e + 2) * tq]
            num = jnp.where(lo_half, xe[:, 0:LANES], xo[:, LANES:2 * LANES])
            den = jnp.where(lo_half, xe[:, LANES:2 * LANES], xo[:, 0:LANES])
            pairs.append(num / den)
        o = jnp.concatenate(pairs, axis=-1)
        og_ref[:, lanes] = (o * _silu(gate_refs[kh][...].astype(F32))).astype(og_ref.dtype)

    if prompt:
        @pl.when(i == 0)
        def _():
            for kh in range(A_KV_HEADS):
                group(kh, True)

        @pl.when(i > 0)
        def _():
            for kh in range(A_KV_HEADS):
                group(kh, False)
    else:
        for kh in range(A_KV_HEADS):
            group(kh, False)


def _t5_bucket(dist):
    d = jnp.maximum(dist, 0)
    large = MAX_EXACT + (jnp.log(jnp.maximum(d, 1).astype(F32) / MAX_EXACT)
                         / math.log(MAX_DISTANCE / MAX_EXACT) * (N_BUCKETS - MAX_EXACT)).astype(jnp.int32)
    return jnp.where(d < MAX_EXACT, d, jnp.minimum(large, N_BUCKETS - 1))


def _attn_mix(qkvg, nq, rel_bias, q_g, k_g, sinks, bn, tn_, cache_k, cache_v, og_dtype):
    m = qkvg.shape[0]
    prompt = cache_k is None
    tq = WINDOW if prompt else tn_
    nb = tn_ // tq
    tk = WINDOW + tq
    col = jnp.arange(tk)[None, :]
    dist = WINDOW + jnp.arange(tq)[:, None] - col
    onehot = (_t5_bucket(dist)[None] == jnp.arange(N_BUCKETS)[:, None, None]).astype(F32)
    bias = jnp.einsum("bh,bqk->hqk", rel_bias.astype(F32), onehot, precision=lax.Precision.HIGHEST)
    bias = jnp.where(((dist >= 0) & (dist < WINDOW))[None], bias, -jnp.inf)
    bias = jnp.where((col == 0)[None], sinks.astype(F32)[:, None, None], bias)
    nkv = A_KV_HEADS * HEAD
    row = lambda b, i: b * nb + i
    kcol = nq // nkv
    gcol = kcol + 2
    win_shape = (1, WINDOW, A_KV_HEADS, HEAD)
    if prompt:
        kp_arr, vp_arr = qkvg, qkvg
        kp_spec = pl.BlockSpec((WINDOW, nkv), lambda b, i: (jnp.maximum(row(b, i) - 1, 0), kcol))
        vp_spec = pl.BlockSpec((WINDOW, nkv), lambda b, i: (jnp.maximum(row(b, i) - 1, 0), kcol + 1))
    else:
        kp_arr, vp_arr = cache_k, cache_v
        kp_spec = pl.BlockSpec(win_shape, lambda b, i: (b, 0, 0, 0))
        vp_spec = pl.BlockSpec(win_shape, lambda b, i: (b, 0, 0, 0))

    def gate_spec(kh):
        return pl.BlockSpec((tq, nkv), lambda b, i: (row(b, i), gcol + kh))

    og, kwin, vwin = pl.pallas_call(
        functools.partial(_attn_kernel, tq=tq, prompt=prompt, nblocks=nb),
        grid=(bn, nb),
        in_specs=[
            pl.BlockSpec((tq, nq), lambda b, i: (row(b, i), 0)),
            pl.BlockSpec((tq, nkv), lambda b, i: (row(b, i), kcol)),
            pl.BlockSpec((tq, nkv), lambda b, i: (row(b, i), kcol + 1)),
            kp_spec, vp_spec,
            *[gate_spec(kh) for kh in range(A_KV_HEADS)],
            pl.BlockSpec((nq // HEAD, tq, tk), lambda b, i: (0, 0, 0)),
            pl.BlockSpec((1, nkv), lambda b, i: (0, 0)),
            pl.BlockSpec((1, nkv), lambda b, i: (0, 0)),
        ],
        out_specs=[
            pl.BlockSpec((tq, nq), lambda b, i: (row(b, i), 0)),
            pl.BlockSpec(win_shape, lambda b, i: (b, 0, 0, 0)),
            pl.BlockSpec(win_shape, lambda b, i: (b, 0, 0, 0)),
        ],
        out_shape=[
            jax.ShapeDtypeStruct((m, nq), og_dtype),
            jax.ShapeDtypeStruct((bn,) + win_shape[1:], F32),
            jax.ShapeDtypeStruct((bn,) + win_shape[1:], F32),
        ],
        scratch_shapes=[
            pltpu.VMEM((tk, nkv), F32), pltpu.VMEM((tk, nkv), F32),
            pltpu.VMEM((A_GROUP * tq, tk), F32),
            pltpu.VMEM((A_GROUP * tq, tk), BF16 if tq % 16 == 0 else F32),
            pltpu.VMEM((8, LANES) if prompt else (tq, nq), F32),
        ],
        compiler_params=_params(("parallel", "arbitrary")),
        name="swa_attention",
    )(qkvg, qkvg, qkvg, kp_arr, vp_arr, *([qkvg] * A_KV_HEADS), bias,
      jnp.tile(q_g.astype(F32), A_GROUP).reshape(1, nkv), jnp.tile(k_g.astype(F32), A_KV_HEADS).reshape(1, nkv))
    return og, kwin, vwin


def _attn_layer(x2d, bn, tn_, g, j, w3, q_g, k_g, sinks, w_out, rel_bias, cache_k, cache_v, tm, tn_cap, og_dtype):
    nq = w_out.shape[1]
    nall = w3.shape[2]
    qkvg = _proj(x2d, g, w3, 0, nall, tm, _pick_tn(nall, tn_cap), og_dtype, w_base=j)[0]
    og, kwin, vwin = _attn_mix(qkvg, nq, rel_bias, q_g, k_g, sinks, bn, tn_, cache_k, cache_v, og_dtype)
    y = _outproj(og, w_out, j, x2d, tm, OUT_TN)
    return y, kwin, vwin


MXU_TILE = 256


def _head_sum_mats():
    jr = lax.broadcasted_iota(jnp.int32, (MXU_TILE, MXU_TILE), 0)
    jc = lax.broadcasted_iota(jnp.int32, (MXU_TILE, MXU_TILE), 1)
    ones_bd = ((jr // HEAD) == (jc // HEAD)).astype(F32)
    return ones_bd, ones_bd * (1.0 / HEAD)


def _per_slab(fn, x):
    return jnp.concatenate([fn(x[:, p * MXU_TILE:(p + 1) * MXU_TILE]) for p in range(x.shape[1] // MXU_TILE)],
                           axis=-1)


def _rwkv_token_terms(k_ref, lw_ref, la_ref, wup_ref, vec_ref, kk_scr, k2_scr, a_scr, ld_scr):
    vec = vec_ref[...]
    w0, a0, kk_w, ka_w = (vec[i:i + 1, :] for i in range(4))
    ones_bd, _ = _head_sum_mats()
    k = k_ref[0].astype(F32)
    xw = w0 + _mm(jnp.tanh(lw_ref[0]), wup_ref[0], NN, 3)
    ld_scr[...] = -math.exp(-0.5) * jax.nn.sigmoid(xw)
    a = jax.nn.sigmoid(a0 + _mm(la_ref[0], wup_ref[1], NN, 3))
    a_scr[...] = a
    kkr = k * kk_w
    ssq = _per_slab(lambda x: _mm(x, ones_bd, NN, STAT_PASSES), kkr * kkr)
    kk_scr[...] = kkr * lax.rsqrt(jnp.maximum(ssq, 1e-24))
    k2_scr[...] = k * (1.0 + (a - 1.0) * ka_w)


def _rwkv_finish(o_scr, r_ref, v_ref, g_ref, k2_scr, vec_ref, og_ref):
    vec = vec_ref[...]
    rk_w, ln_g, ln_b = (vec[i:i + 1, :] for i in range(4, 7))
    ones_bd, mean_bd = _head_sum_mats()
    o = o_scr[...]
    v = v_ref[0].astype(F32)
    mean = _per_slab(lambda x: _mm(x, mean_bd, NN, STAT_PASSES), o)
    dlt = o - mean
    var = _per_slab(lambda x: _mm(x, mean_bd, NN, STAT_PASSES), dlt * dlt)
    on = dlt * lax.rsqrt(var + B_LN_EPS) * ln_g + ln_b
    bonus = _per_slab(lambda x: _mm(x, ones_bd, NN, STAT_PASSES), r_ref[0].astype(F32) * k2_scr[...] * rk_w)
    on = on + bonus * v
    og_ref[...] = (on * _silu(g_ref[0].astype(F32))).astype(og_ref.dtype)


def _rwkv_kernel(r_ref, k_ref, v_ref, g_ref, lw_ref, la_ref, wup_ref, vec_ref, s0_ref,
                 og_ref, sout_ref,
                 s_ref, kk_scr, k2_scr, a_scr, ld_scr, q_scr, o0_scr, gp_scr, z_scr, ec_scr, o_scr,
                 *, C, tb, lp, nt, unroll, nseq):
    t = pl.program_id(2)
    n = 2 * C
    nchunks = tb // C
    lane = lax.broadcasted_iota(jnp.int32, (1, LANES), 1)
    lo = (lane < HEAD).astype(F32)
    hi = 1.0 - lo

    @pl.when(t == 0)
    def _():
        z = jnp.zeros((HEAD, HEAD), F32)
        for si in range(nseq):
            for pp in range(lp):
                s_ref[si * lp + pp, 0:HEAD, :] = jnp.concatenate([s0_ref[si, 2 * pp], z], axis=-1)
                s_ref[si * lp + pp, HEAD:LANES, :] = jnp.concatenate([z, s0_ref[si, 2 * pp + 1]], axis=-1)

    _rwkv_token_terms(k_ref, lw_ref, la_ref, wup_ref, vec_ref, kk_scr, k2_scr, a_scr, ld_scr)

    rr = lax.broadcasted_iota(jnp.int32, (n, n), 0)
    cc = lax.broadcasted_iota(jnp.int32, (n, n), 1)
    strict = cc < rr
    incl = cc <= rr
    eye = (cc == rr).astype(F32)
    tri = _tri(C)
    nsteps = int(math.log2(C)) - 1

    def stage_b(gi, carry):
        units = []
        for cu in range(unroll):
            ci = gi * unroll + cu
            rows = pl.ds(pl.multiple_of(ci * C, C), C)
            ld = ld_scr[rows, :]
            b = _mm_exact_lhs(tri, ld)
            eb = jnp.exp(b)
            enb = jnp.exp(-b)
            kk = kk_scr[rows, :]
            al = -kk * jnp.exp(b - ld)
            be = kk * a_scr[rows, :] * enb
            kt = k2_scr[rows, :] * enb
            rb = r_ref[0, rows, :].astype(F32) * eb
            v = v_ref[0, rows, :].astype(F32)
            e_c = eb[C - 1:C, :]
            ec_scr[ci] = jnp.broadcast_to(e_c, (SUBLANES, e_c.shape[1]))
            for pp in range(lp):
                ls = slice(pp * LANES, (pp + 1) * LANES)
                cat = lambda x: jnp.concatenate([x[:, ls] * lo, x[:, ls] * hi], axis=0)
                units.append(dict(idx=ci * lp + pp, ecp=e_c[:, ls], la=cat(al), lr=cat(rb), rb=cat(be),
                                  rk=cat(kt), vb=cat(v)))
        nu = range(len(units))
        if n % LANES == 0:
            prod = [_mm(jnp.concatenate([u["la"], u["lr"]], axis=0),
                        jnp.concatenate([u["rb"], u["rk"]], axis=0), NT) for u in units]
            m_ab = [jnp.where(strict, x[0:n, 0:n], 0.0) for x in prod]
            m_ak = [jnp.where(strict, x[0:n, n:2 * n], 0.0) for x in prod]
            m_rb = [jnp.where(incl, x[n:2 * n, 0:n], 0.0) for x in prod]
            m_rk = [jnp.where(incl, x[n:2 * n, n:2 * n], 0.0) for x in prod]
            akv = [_mm(m_ak[i], units[i]["vb"]) for i in nu]
            tinv = [eye + a_ for a_ in m_ab]
            apow = [_mm(a_, a_) for a_ in m_ab]
            for step in range(nsteps):
                if step < nsteps - 1:
                    both = [_mm(apow[i], jnp.concatenate([tinv[i], apow[i]], axis=1)) for i in nu]
                    tinv = [tinv[i] + both[i][:, 0:n] for i in nu]
                    apow = [x[:, n:2 * n] for x in both]
                else:
                    tinv = [tinv[i] + _mm(apow[i], tinv[i]) for i in nu]
            wu = [_mm(tinv[i], jnp.concatenate([units[i]["la"], akv[i]], axis=1)) for i in nu]
            w = [x[:, 0:LANES] for x in wu]
            uv = [jnp.concatenate([wu[i][:, LANES:2 * LANES], units[i]["vb"]], axis=0) for i in nu]
            rbe = [u["rb"] * u["ecp"] for u in units]
            for i, u in enumerate(units):
                q_scr[u["idx"]] = u["lr"] + _mm(m_rb[i], w[i])
            for i, u in enumerate(units):
                o0_scr[u["idx"]] = _mm(jnp.concatenate([m_rb[i], m_rk[i]], axis=1), uv[i])
            for i, u in enumerate(units):
                gp_scr[u["idx"]] = _mm(w[i], rbe[i], TN)
            for i, u in enumerate(units):
                z_scr[u["idx"]] = _mm(uv[i], jnp.concatenate([rbe[i], u["rk"] * u["ecp"]], axis=0), TN)
            return carry
        each = lambda fn: [fn(u) for u in units]
        m_ab = each(lambda u: jnp.where(strict, _mm(u["la"], u["rb"], NT), 0.0))
        m_ak = each(lambda u: jnp.where(strict, _mm(u["la"], u["rk"], NT), 0.0))
        m_rb = each(lambda u: jnp.where(incl, _mm(u["lr"], u["rb"], NT), 0.0))
        m_rk = each(lambda u: jnp.where(incl, _mm(u["lr"], u["rk"], NT), 0.0))
        akv = [_mm(m, u["vb"]) for m, u in zip(m_ak, units)]
        rkv = [_mm(m, u["vb"]) for m, u in zip(m_rk, units)]
        apow = m_ab
        tinv = [eye + a_ for a_ in apow]
        for _ in range(nsteps):
            apow = [_mm(a_, a_) for a_ in apow]
            tinv = [t_ + _mm(a_, t_) for a_, t_ in zip(apow, tinv)]
        w = [_mm(t_, u["la"]) for t_, u in zip(tinv, units)]
        u0 = [_mm(t_, x_) for t_, x_ in zip(tinv, akv)]
        rbe = [u["rb"] * u["ecp"] for u in units]
        for i, u in enumerate(units):
            q_scr[u["idx"]] = u["lr"] + _mm(m_rb[i], w[i])
        for i, u in enumerate(units):
            o0_scr[u["idx"]] = _mm(m_rb[i], u0[i]) + rkv[i]
        for i, u in enumerate(units):
            gp_scr[u["idx"]] = _mm(w[i], rbe[i], TN)
        for i, u in enumerate(units):
            z_scr[u["idx"]] = _mm(u0[i], rbe[i], TN) + _mm(u["vb"], u["rk"] * u["ecp"], TN)
        return carry

    lax.fori_loop(0, nchunks // unroll, stage_b, 0)

    for ci in range(nchunks):
        for pp in range(lp):
            ls = slice(pp * LANES, (pp + 1) * LANES)
            idx = ci * lp + pp
            sidx = idx if nseq > 1 else pp
            s = s_ref[sidx]
            o_bd = _mm(q_scr[idx], s, NT) + o0_scr[idx]
            o_scr[ci * C:(ci + 1) * C, ls] = o_bd[0:C] + o_bd[C:n]
            s_ref[sidx] = s * ec_scr[ci, 0:1, ls] + _mm(s, gp_scr[idx]) + z_scr[idx]

    _rwkv_finish(o_scr, r_ref, v_ref, g_ref, k2_scr, vec_ref, og_ref)

    @pl.when(t == nt - 1)
    def _():
        for si in range(nseq):
            for pp in range(lp):
                sout_ref[si, 2 * pp] = s_ref[si * lp + pp, 0:HEAD, 0:HEAD]
                sout_ref[si, 2 * pp + 1] = s_ref[si * lp + pp, HEAD:LANES, HEAD:LANES]


def _rwkv_mix(rkvg, lora, wup, vec, s0, bn, tn_, C, tb, lp, unroll, nseq, og_dtype):
    m = rkvg.shape[1]
    nch = rkvg.shape[2]
    wl = lp * LANES
    ngrp = nch // wl
    nt = tn_ * nseq // tb
    nchunks = tb // C
    rank = lora.shape[2]
    nh = s0.shape[1]
    row = lambda b, t: b * nt + t

    def xspec(c):
        return pl.BlockSpec((1, tb, wl), lambda b, p, t: (c, row(b, t), p))

    def lspec(c):
        return pl.BlockSpec((1, tb, rank), lambda b, p, t: (c, row(b, t), 0))

    blk = [pltpu.VMEM((tb, wl), F32)]
    scratch = ([pltpu.VMEM((nseq * lp, LANES, LANES), F32)] + blk * 4
               + [pltpu.VMEM((nchunks * lp, 2 * C, LANES), F32), pltpu.VMEM((nchunks * lp, 2 * C, LANES), F32),
                  pltpu.VMEM((nchunks * lp, LANES, LANES), F32), pltpu.VMEM((nchunks * lp, LANES, LANES), F32),
                  pltpu.VMEM((nchunks, SUBLANES, wl), F32)] + blk)
    og, sout = pl.pallas_call(
        functools.partial(_rwkv_kernel, C=C, tb=tb, lp=lp, nt=nt, unroll=unroll, nseq=nseq),
        grid=(bn // nseq, ngrp, nt),
        in_specs=[
            xspec(0), xspec(1), xspec(2), xspec(3), lspec(0), lspec(1),
            pl.BlockSpec((2, rank, wl), lambda b, p, t: (0, 0, p)),
            pl.BlockSpec((8, wl), lambda b, p, t: (0, p)),
            pl.BlockSpec((nseq, 2 * lp, HEAD, HEAD), lambda b, p, t: (b, p, 0, 0)),
        ],
        out_specs=[
            pl.BlockSpec((tb, wl), lambda b, p, t: (row(b, t), p)),
            pl.BlockSpec((nseq, 2 * lp, HEAD, HEAD), lambda b, p, t: (b, p, 0, 0)),
        ],
        out_shape=[
            jax.ShapeDtypeStruct((m, nch), og_dtype),
            jax.ShapeDtypeStruct((bn, nh, HEAD, HEAD), F32),
        ],
        scratch_shapes=scratch,
        compiler_params=_params(("parallel", "parallel", "arbitrary")),
        name="rwkv7_chunked",
    )(rkvg, rkvg, rkvg, rkvg, lora, lora, wup, vec, s0)
    return og, sout


def _rwkv_layer(x2d, bn, tn_, g, j, shift, s0, mu, w_rkvg, w_down, w_up, w0, a0, k_k, k_a, r_k, ln_g, ln_b,
                w_out, tm, tn_cap, C, tb, lp, unroll, nseq, og_dtype):
    m, d = x2d.shape
    nblk = m // tm
    if tm <= tn_:
        starts = jnp.arange(nblk) * tm
        prev = _rms_rows(x2d[jnp.maximum(starts - 1, 0)], g)
        first = jnp.where((starts % tn_ == 0)[:, None], shift[starts // tn_], prev).reshape(nblk, 1, d)
        period = None
    else:
        assert nblk == 1
        first = jnp.zeros((bn, tn_, d), F32).at[:, 0].set(shift).reshape(1, m, d)
        period = tn_
    shift_out = _rms_rows(x2d[tn_ - 1::tn_], g)
    nch = w_rkvg.shape[2]
    rkvg = _proj(x2d, g, w_rkvg, 0, nch, tm, _pick_tn(nch, tn_cap), og_dtype, first=first, mu=mu[:4], period=period,
                 w_base=4 * j, nc=4)
    rank = w_down.shape[2]
    lora = _proj(x2d, g, w_down, 0, rank, tm, rank, F32, first=first, mu=mu[4:6], period=period, w_base=2 * j, nc=2)
    vec = jnp.stack([w0, a0, k_k, k_a, r_k.reshape(nch), ln_g, ln_b, jnp.zeros((nch,), F32)])
    og, sout = _rwkv_mix(rkvg, lora, w_up, vec, s0, bn, tn_, C, tb, lp, unroll, nseq, og_dtype)
    y = _outproj(og, w_out, j, x2d, tm, OUT_TN)
    return y, sout, shift_out


GLA_SUB = 16


def _gla_kernel(q_ref, k_ref, v_ref, gate_ref, gl_ref, wup_ref, bgk_ref, og_w_ref, s0_ref,
                og_ref, sout_ref, st_ref, z_scr, qd_scr, o_scr, ec_scr, *, C, tb, nt, nseq):
    t = pl.program_id(2)
    dk = q_ref.shape[-1]
    nchunks = tb // C
    sub = min(GLA_SUB, C)
    nsub = C // sub
    qscale = dk ** -0.5

    @pl.when(t == 0)
    def _():
        for si in range(nseq):
            st_ref[si] = s0_ref[si, 0]

    tri = _tri(C)
    glog_all = -_softplus(-(_mm(gl_ref[...], wup_ref[...], NN, 3) + bgk_ref[...])) * (1.0 / C_GATE_NORM)
    chunks = range(nchunks)
    rows = [slice(c * C, (c + 1) * C) for c in chunks]
    glog = [glog_all[rs] for rs in rows]
    b = [_mm_exact_lhs(tri, gl_) for gl_ in glog]
    q = [q_ref[rs, :].astype(F32) * qscale for rs in rows]
    k = [k_ref[rs, :].astype(F32) for rs in rows]
    v = [v_ref[rs, :].astype(F32) for rs in rows]
    for c in chunks:
        bl = b[c][C - 1:C, :]
        qd_scr[rows[c], :] = q[c] * jnp.exp(b[c])
        ec_scr[c] = jnp.broadcast_to(jnp.exp(bl), (SUBLANES, dk))
        z_scr[c] = _mm(k[c] * jnp.exp(bl - b[c]), v[c], TN)
    parts = [[] for _ in chunks]
    for i in range(nsub):
        r0 = i * sub
        nk = r0 + sub
        ar = lax.broadcasted_iota(jnp.int32, (sub, nk), 0) + r0
        ac = lax.broadcasted_iota(jnp.int32, (sub, nk), 1)
        att = []
        for c in chunks:
            ref = b[c][r0:r0 + 1, :] - glog[c][r0:r0 + 1, :]
            qi = q[c][r0:nk] * jnp.exp(b[c][r0:nk] - ref)
            ki = k[c][0:nk] * jnp.exp(ref - b[c][0:nk])
            att.append(jnp.where(ac <= ar, _mm(qi, ki, NT), 0.0))
        for c in chunks:
            parts[c].append(_mm(att[c], v[c][0:nk]))
    for c in chunks:
        o_scr[rows[c], :] = jnp.concatenate(parts[c], axis=0) if nsub > 1 else parts[c][0]

    for c in chunks:
        si = c if nseq > 1 else 0
        st = st_ref[si]
        o_scr[rows[c], :] = o_scr[rows[c], :] + _mm(qd_scr[rows[c], :], st)
        st_ref[si] = st * jnp.transpose(ec_scr[c])[:, 0:1] + z_scr[c]

    o = o_scr[...]
    on = o * lax.rsqrt(jnp.mean(o * o, axis=-1, keepdims=True) + NORM_EPS) * og_w_ref[...]
    og_ref[...] = (on * _silu(gate_ref[...].astype(F32))).astype(og_ref.dtype)

    @pl.when(t == nt - 1)
    def _():
        for si in range(nseq):
            sout_ref[si, 0] = st_ref[si]


def _gla_mix(qkvg, gl, w_up, b_gk, o_g, s0, bn, tn_, C, tb, nseq, og_dtype):
    m = qkvg.shape[0]
    nh, dk, dv = s0.shape[1:]
    nt = tn_ * nseq // tb
    rank = gl.shape[1]
    row = lambda b, t: b * nt + t
    kcol = nh
    vcol = 2 * nh * dk // dv
    gcol = vcol + nh
    og, sout = pl.pallas_call(
        functools.partial(_gla_kernel, C=C, tb=tb, nt=nt, nseq=nseq),
        grid=(bn // nseq, nh, nt),
        in_specs=[
            pl.BlockSpec((tb, dk), lambda b, h, t: (row(b, t), h)),
            pl.BlockSpec((tb, dk), lambda b, h, t: (row(b, t), kcol + h)),
            pl.BlockSpec((tb, dv), lambda b, h, t: (row(b, t), vcol + h)),
            pl.BlockSpec((tb, dv), lambda b, h, t: (row(b, t), gcol + h)),
            pl.BlockSpec((tb, rank), lambda b, h, t: (row(b, t), 0)),
            pl.BlockSpec((rank, dk), lambda b, h, t: (0, h)),
            pl.BlockSpec((1, dk), lambda b, h, t: (0, h)),
            pl.BlockSpec((1, dv), lambda b, h, t: (0, 0)),
            pl.BlockSpec((nseq, 1, dk, dv), lambda b, h, t: (b, h, 0, 0)),
        ],
        out_specs=[
            pl.BlockSpec((tb, dv), lambda b, h, t: (row(b, t), h)),
            pl.BlockSpec((nseq, 1, dk, dv), lambda b, h, t: (b, h, 0, 0)),
        ],
        out_shape=[
            jax.ShapeDtypeStruct((m, nh * dv), og_dtype),
            jax.ShapeDtypeStruct((bn, nh, dk, dv), F32),
        ],
        scratch_shapes=[
            pltpu.VMEM((nseq, dk, dv), F32),
            pltpu.VMEM((tb // C, dk, dv), F32),
            pltpu.VMEM((tb, dk), F32),
            pltpu.VMEM((tb, dv), F32),
            pltpu.VMEM((tb // C, SUBLANES, dk), F32),
        ],
        compiler_params=_params(("parallel", "parallel", "arbitrary")),
        name="gla_chunked",
    )(qkvg, qkvg, qkvg, qkvg, gl, w_up, b_gk.reshape(1, -1), o_g.reshape(1, -1), s0)
    return og, sout


def _gla_layer(x2d, bn, tn_, g, j, s0, w3, w_gl, w_up, b_gk, o_g, w_out, tm, tn_cap, C, tb, nseq, og_dtype):
    nh, dk, dv = s0.shape[1:]
    nfused = 2 * nh * dk + 2 * nh * dv
    rank = w_up.shape[0]
    qkvg = _proj(x2d, g, w3, 0, nfused, tm, _pick_tn(nfused, tn_cap), og_dtype, w_base=j)[0]
    gl = _proj(x2d, g, w_gl, 0, rank, tm, rank, F32, w_base=j)[0]
    og, sout = _gla_mix(qkvg, gl, w_up, b_gk, o_g, s0, bn, tn_, C, tb, nseq, og_dtype)
    y = _outproj(og, w_out, j, x2d, tm, OUT_TN)
    return y, sout


PROMPT_TM = 1024
RWKV_CHUNK = 64
RWKV_TB = 512
RWKV_PAIRS = 4
RWKV_UNROLL = 4
STAT_PASSES = 1
GLA_CHUNK = 64
GLA_TB = 1024
PROMPT_TN = 2048
LERP_TN = 1024
SAMPLE_TN = 2048
SAMPLE_SEQS = 16
GLA_SAMPLE_SEQS = 8


def kernel(x_prompt, x_sample, cache_k_win, cache_v_win, state_wkv, state_shift, state_gla, norm_g, rel_bias, w_in_a, q_norm_g, k_norm_g, sinks, w_out_a, mu_b, w_rkvg_b, w_lora_down_b, w_lora_up_b, w0_b, a0_b, k_k_b, k_a_b, r_k_b, ln_x_g_b, ln_x_b_b, w_out_b, w_in_c, w_gk_up_c, b_gk_c, o_norm_g_c, w_out_c):
    bp, tp, d = x_prompt.shape
    bs, ts, _ = x_sample.shape
    depth = norm_g.shape[0]
    bf = lambda w: w.astype(BF16)
    nfused_c = w_in_c.shape[2] - w_gk_up_c.shape[1]
    w_in_a, w_out_a, w_out_b, w_out_c = bf(w_in_a), bf(w_out_a), bf(w_out_b), bf(w_out_c)
    w_rkvg_b = bf(w_rkvg_b).reshape((-1,) + w_rkvg_b.shape[2:])
    w_lora_down_b = bf(w_lora_down_b).reshape((-1,) + w_lora_down_b.shape[2:])
    w_gl_c = bf(w_in_c[:, :, nfused_c:])
    w_in_c = bf(w_in_c)
    xp = x_prompt.reshape(bp * tp, d)
    xs = x_sample.reshape(bs * ts, d)
    ms = bs * ts
    kwp, vwp, kws, vws, wkvp, shp, wkvs, shs, glap, glas = ([] for _ in range(10))
    for layer in range(depth):
        kind, j = layer % 3, layer // 3
        g = norm_g[layer]
        if kind == 0:
            wa = (w_in_a, q_norm_g[j], k_norm_g[j], sinks[j], w_out_a, rel_bias)
            xp, kp_, vp_ = _attn_layer(xp, bp, tp, g, j, *wa, None, None, PROMPT_TM, PROMPT_TN, BF16)
            xs, ks_, vs_ = _attn_layer(xs, bs, ts, g, j, *wa, cache_k_win[j], cache_v_win[j], ms, SAMPLE_TN, F32)
            kwp.append(kp_); vwp.append(vp_); kws.append(ks_); vws.append(vs_)
        elif kind == 1:
            wb = (mu_b[j], w_rkvg_b, w_lora_down_b, w_lora_up_b[j], w0_b[j], a0_b[j], k_k_b[j], k_a_b[j],
                  r_k_b[j], ln_x_g_b[j], ln_x_b_b[j], w_out_b)
            nh = w_rkvg_b.shape[2] // HEAD
            xp, sp_, lp_ = _rwkv_layer(xp, bp, tp, g, j, jnp.zeros((bp, d), F32), jnp.zeros((bp, nh, HEAD, HEAD), F32),
                                       *wb, PROMPT_TM, LERP_TN, RWKV_CHUNK, RWKV_TB, RWKV_PAIRS, RWKV_UNROLL, 1, BF16)
            xs, ss_, ls_ = _rwkv_layer(xs, bs, ts, g, j, state_shift[j], state_wkv[j], *wb, ms, SAMPLE_TN, ts,
                                       ts * SAMPLE_SEQS, RWKV_PAIRS, SAMPLE_SEQS, SAMPLE_SEQS, F32)
            wkvp.append(sp_); shp.append(lp_); wkvs.append(ss_); shs.append(ls_)
        else:
            wc = (w_in_c, w_gl_c, w_gk_up_c[j], b_gk_c[j], o_norm_g_c[j], w_out_c)
            xp, sp_ = _gla_layer(xp, bp, tp, g, j, jnp.zeros((bp,) + state_gla.shape[2:], F32), *wc,
                                 PROMPT_TM, PROMPT_TN, GLA_CHUNK, GLA_TB, 1, BF16)
            xs, ss_ = _gla_layer(xs, bs, ts, g, j, state_gla[j], *wc, ms, SAMPLE_TN, ts, ts * GLA_SAMPLE_SEQS,
                                 GLA_SAMPLE_SEQS, F32)
            glap.append(sp_); glas.append(ss_)
    return (xp.reshape(bp, tp, d), xs.reshape(bs, ts, d),
            jnp.stack(kwp), jnp.stack(vwp), jnp.stack(kws), jnp.stack(vws),
            jnp.stack(wkvp), jnp.stack(shp), jnp.stack(wkvs), jnp.stack(shs),
            jnp.stack(glap), jnp.stack(glas))
```

```python
import functools
import math

import jax
import jax.numpy as jnp
from jax import lax
from jax.experimental import pallas as pl
from jax.experimental.pallas import tpu as pltpu

F32 = jnp.float32
BF16 = jnp.bfloat16

NORM_EPS = 1e-6
HEAD = 64
LANES = 128
SUBLANES = 8
OUT_TN = 1024
WINDOW = 128
N_BUCKETS = 32
MAX_EXACT = N_BUCKETS // 2
MAX_DISTANCE = 128
B_LN_EPS = 64e-5
C_GATE_NORM = 16.0
VMEM_LIMIT = 56 * 1024 * 1024

NN = (((1,), (0,)), ((), ()))
NT = (((1,), (1,)), ((), ()))
TN = (((0,), (0,)), ((), ()))


def _params(sem):
    return pltpu.CompilerParams(dimension_semantics=sem, vmem_limit_bytes=VMEM_LIMIT)


def _split(x):
    hi = x.astype(BF16)
    lo = (x - hi.astype(F32)).astype(BF16)
    return hi, lo


def _mm(a, b, dims=NN, passes=1):
    if passes == 1:
        return lax.dot_general(a.astype(BF16), b.astype(BF16), dims, preferred_element_type=F32)
    a_hi, a_lo = _split(a)
    b_hi, b_lo = _split(b)
    dg = functools.partial(lax.dot_general, dimension_numbers=dims, preferred_element_type=F32)
    return dg(a_hi, b_hi) + (dg(a_hi, b_lo) + dg(a_lo, b_hi))


def _mm_exact_lhs(a_bf16, b):
    b0 = b.astype(BF16)
    r1 = b - b0.astype(F32)
    b1 = r1.astype(BF16)
    b2 = (r1 - b1.astype(F32)).astype(BF16)
    dg = functools.partial(lax.dot_general, dimension_numbers=NN, preferred_element_type=F32)
    return dg(a_bf16, b0) + (dg(a_bf16, b1) + dg(a_bf16, b2))


def _softplus(z):
    return jnp.maximum(z, 0.0) + jnp.log1p(jnp.exp(-jnp.abs(z)))


def _silu(g):
    return g * jax.nn.sigmoid(g)


def _tri(c):
    r = lax.broadcasted_iota(jnp.int32, (c, c), 0)
    col = lax.broadcasted_iota(jnp.int32, (c, c), 1)
    return (col <= r).astype(BF16)


def _rms_rows_kernel(x_ref, g_ref, o_ref):
    x = x_ref[...]
    o_ref[...] = x * lax.rsqrt(jnp.mean(x * x, axis=-1, keepdims=True) + NORM_EPS) * g_ref[...]


def _rms_rows(rows, g):
    n, d = rows.shape
    npad = -(-n // 8) * 8
    rows_p = jnp.pad(rows, ((0, npad - n), (0, 0)))
    out = pl.pallas_call(
        _rms_rows_kernel,
        out_shape=jax.ShapeDtypeStruct((npad, d), F32),
        name="rms_rows",
    )(rows_p, g.reshape(1, d))
    return out[:n]


PROLOGUE_ROWS = 256


def _proj_kernel(*refs, lerp, period, tm):
    if lerp:
        x_ref, g_ref, w_ref, first_ref, mu_ref, o_ref, xm_ref, h_ref, d_ref = refs
    else:
        x_ref, g_ref, w_ref, o_ref, xm_ref = refs
    c = pl.program_id(1)
    j = pl.program_id(2)
    rc = min(PROLOGUE_ROWS, tm)

    def normed(c0):
        x = x_ref[c0:c0 + rc, :]
        return x * lax.rsqrt(jnp.mean(x * x, axis=-1, keepdims=True) + NORM_EPS) * g_ref[...]

    if lerp:
        @pl.when((c == 0) & (j == 0))
        def _():
            carry = first_ref[0, 0:1, :]
            for c0 in range(0, tm, rc):
                h = normed(c0)
                row = lax.broadcasted_iota(jnp.int32, h.shape, 0)
                hs = jnp.where(row == 0, carry, pltpu.roll(h, 1, 0))
                if period is not None:
                    hs = jnp.where(row % period == 0, first_ref[0, c0:c0 + rc, :], hs)
                carry = h[rc - 1:rc, :]
                h_ref[c0:c0 + rc, :] = h.astype(h_ref.dtype)
                d_ref[c0:c0 + rc, :] = (hs - h).astype(d_ref.dtype)

        @pl.when(j == 0)
        def _():
            for c0 in range(0, tm, rc):
                xm_ref[c0:c0 + rc, :] = (h_ref[c0:c0 + rc, :].astype(F32)
                                         + d_ref[c0:c0 + rc, :].astype(F32) * mu_ref[0]).astype(BF16)
    else:
        @pl.when(j == 0)
        def _():
            for c0 in range(0, tm, rc):
                xm_ref[c0:c0 + rc, :] = normed(c0).astype(BF16)

    o_ref[0] = jnp.dot(xm_ref[...], w_ref[0], preferred_element_type=F32).astype(o_ref.dtype)


def _pick_tn(n, cap):
    if n < LANES:
        return n
    units = n // LANES
    return LANES * max(u for u in range(1, units + 1) if units % u == 0 and u * LANES <= cap)


def _proj(x2d, g, w3, col_off, n_out, tm, tn, out_dtype, first=None, mu=None, period=None, w_base=0, nc=1):
    m, d = x2d.shape
    lerp = first is not None
    assert m % tm == 0 and n_out % tn == 0 and col_off % tn == 0
    joff = col_off // tn
    in_specs = [
        pl.BlockSpec((tm, d), lambda i, c, j: (i, 0)),
        pl.BlockSpec((1, d), lambda i, c, j: (0, 0)),
        pl.BlockSpec((1, d, tn), lambda i, c, j: (w_base + c, 0, joff + j)),
    ]
    args = [x2d, g.reshape(1, d), w3]
    if lerp:
        fr = first.shape[1]
        in_specs += [
            pl.BlockSpec((1, fr, d), lambda i, c, j: (i, 0, 0)),
            pl.BlockSpec((1, 1, d), lambda i, c, j: (c, 0, 0)),
        ]
        args += [first, mu.reshape(nc, 1, d)]
    return pl.pallas_call(
        functools.partial(_proj_kernel, lerp=lerp, period=period, tm=tm),
        grid=(m // tm, nc, n_out // tn),
        in_specs=in_specs,
        out_specs=pl.BlockSpec((1, tm, tn), lambda i, c, j: (c, i, j)),
        out_shape=jax.ShapeDtypeStruct((nc, m, n_out), out_dtype),
        scratch_shapes=[pltpu.VMEM((tm, d), BF16)] + ([pltpu.VMEM((tm, d), out_dtype)] * 2 if lerp else []),
        compiler_params=_params(("parallel", "arbitrary", "arbitrary")),
        name="proj",
    )(*args)


def _outproj_kernel(a_ref, w_ref, x_ref, o_ref):
    o_ref[...] = x_ref[...] + jnp.dot(a_ref[...].astype(BF16), w_ref[0], preferred_element_type=F32)


def _outproj(a2d, w3, w_base, x2d, tm, tn):
    m, kdim = a2d.shape
    n = w3.shape[2]
    return pl.pallas_call(
        _outproj_kernel,
        grid=(m // tm, n // tn),
        in_specs=[
            pl.BlockSpec((tm, kdim), lambda i, j: (i, 0)),
            pl.BlockSpec((1, kdim, tn), lambda i, j: (w_base, 0, j)),
            pl.BlockSpec((tm, tn), lambda i, j: (i, j)),
        ],
        out_specs=pl.BlockSpec((tm, tn), lambda i, j: (i, j)),
        out_shape=jax.ShapeDtypeStruct((m, n), F32),
        compiler_params=_params(("parallel", "arbitrary")),
        name="outproj",
    )(a2d, w3, x2d)


A_KV_HEADS = 8
A_GROUP = 8


def _attn_kernel(*refs, tq, prompt, nblocks):
    q_ref, kc_ref, vc_ref, kp_ref, vp_ref = refs[:5]
    gate_refs = refs[5:5 + A_KV_HEADS]
    (bias_ref, qg_ref, kg_ref, og_ref, kwin_ref, vwin_ref,
     k_scr, v_scr, s_scr, p_scr, qn_scr) = refs[5 + A_KV_HEADS:]
    return _attn_body(q_ref, kc_ref, vc_ref, kp_ref, vp_ref, gate_refs, bias_ref, qg_ref, kg_ref,
                      og_ref, kwin_ref, vwin_ref, k_scr, v_scr, s_scr, p_scr, qn_scr,
                      tq=tq, prompt=prompt, nblocks=nblocks)


def _attn_body(q_ref, kc_ref, vc_ref, kp_ref, vp_ref, gate_refs, bias_ref, qg_ref, kg_ref,
               og_ref, kwin_ref, vwin_ref, k_scr, v_scr, s_scr, p_scr, qn_scr, *, tq, prompt, nblocks):
    i = pl.program_id(1)
    tk = WINDOW + tq
    kg = kg_ref[...]
    qg = qg_ref[...] * (HEAD ** -0.5)
    nslab = kg.shape[1]
    jr = lax.broadcasted_iota(jnp.int32, (nslab, nslab), 0)
    jc = lax.broadcasted_iota(jnp.int32, (nslab, nslab), 1)
    mean_bd = jnp.where((jr // HEAD) == (jc // HEAD), 1.0 / HEAD, 0.0).astype(BF16)
    ones_v = jnp.ones((tk, HEAD), BF16)

    def slab_norm(x, gain):
        ms = jnp.dot((x * x).astype(BF16), mean_bd, preferred_element_type=F32)
        return x * lax.rsqrt(ms + NORM_EPS) * gain

    if prompt:
        k_scr[0:WINDOW, :] = slab_norm(kp_ref[...].astype(F32), kg)
        k_scr[WINDOW:tk, :] = slab_norm(kc_ref[...].astype(F32), kg)
        v_scr[0:WINDOW, :] = vp_ref[...].astype(F32)
    else:
        q_all = q_ref[...].astype(F32)
        slabs = [q_all[:, g * nslab:(g + 1) * nslab] for g in range(A_KV_HEADS)] + [kc_ref[...].astype(F32)]
        inv = lax.rsqrt(jnp.dot(jnp.concatenate([x * x for x in slabs], axis=0).astype(BF16), mean_bd,
                                preferred_element_type=F32) + NORM_EPS)
        for g in range(A_KV_HEADS):
            qn_scr[:, g * nslab:(g + 1) * nslab] = slabs[g] * inv[g * tq:(g + 1) * tq] * qg
        k_scr[WINDOW:tk, :] = slabs[A_KV_HEADS] * inv[A_KV_HEADS * tq:(A_KV_HEADS + 1) * tq] * kg
        for kh in range(A_KV_HEADS):
            k_scr[0:WINDOW, kh * HEAD:(kh + 1) * HEAD] = kp_ref[0, :, kh, :]
            v_scr[0:WINDOW, kh * HEAD:(kh + 1) * HEAD] = vp_ref[0, :, kh, :]
    v_scr[WINDOW:tk, :] = vc_ref[...].astype(F32)

    @pl.when(i == nblocks - 1)
    def _():
        keep = 0 if prompt else WINDOW - tq
        if keep:
            kwin_ref[0, 0:keep] = kp_ref[0, tq:WINDOW]
            vwin_ref[0, 0:keep] = vp_ref[0, tq:WINDOW]
        for kh in range(A_KV_HEADS):
            sl = slice(kh * HEAD, (kh + 1) * HEAD)
            kwin_ref[0, keep:WINDOW, kh, :] = k_scr[tk - WINDOW + keep:tk, sl]
            vwin_ref[0, keep:WINDOW, kh, :] = v_scr[tk - WINDOW + keep:tk, sl]

    k_scr[0:1, :] = jnp.zeros((1, k_scr.shape[1]), F32)
    v_scr[0:1, :] = jnp.zeros((1, v_scr.shape[1]), F32)

    c = lax.broadcasted_iota(jnp.int32, (tq, tk), 1)
    no_prev = (c >= WINDOW) | (c == 0)
    lo_half = lax.broadcasted_iota(jnp.int32, (tq, LANES), 1) < HEAD

    def group(kh, first_block):
        kk = k_scr[:, kh * HEAD:(kh + 1) * HEAD].astype(BF16)
        vv = v_scr[:, kh * HEAD:(kh + 1) * HEAD].astype(BF16)
        h0 = kh * A_GROUP
        lanes = slice(h0 * HEAD, (h0 + A_GROUP) * HEAD)
        qn = slab_norm(q_ref[:, lanes].astype(F32), qg) if prompt else qn_scr[:, lanes]
        qs = jnp.concatenate([qn[:, e * HEAD:(e + 1) * HEAD] for e in range(A_GROUP)], axis=0)
        s_scr[...] = lax.dot_general(qs.astype(BF16), kk, NT, preferred_element_type=F32)
        for e in range(A_GROUP):
            rs = slice(e * tq, (e + 1) * tq)
            s = s_scr[rs, :] + bias_ref[h0 + e]
            if first_block:
                s = jnp.where(no_prev, s, -jnp.inf)
            p_scr[rs, :] = jnp.exp(s - jnp.max(s, axis=-1, keepdims=True)).astype(p_scr.dtype)
        x = jnp.dot(p_scr[...].astype(BF16), jnp.concatenate([vv, ones_v, ones_v, vv], axis=-1),
                    preferred_element_type=F32)
        pairs = []
        for e in range(0, A_GROUP, 2):
            xe, xo = x[e * tq:(e + 1) * tq], x[(e + 1) * tq:(e + 2) * tq]
            num = jnp.where(lo_half, xe[:, 0:LANES], xo[:, LANES:2 * LANES])
            den = jnp.where(lo_half, xe[:, LANES:2 * LANES], xo[:, 0:LANES])
            pairs.append(num / den)
        o = jnp.concatenate(pairs, axis=-1)
        og_ref[:, lanes] = (o * _silu(gate_refs[kh][...].astype(F32))).astype(og_ref.dtype)

    if prompt:
        @pl.when(i == 0)
        def _():
            for kh in range(A_KV_HEADS):
                group(kh, True)

        @pl.when(i > 0)
        def _():
            for kh in range(A_KV_HEADS):
                group(kh, False)
    else:
        for kh in range(A_KV_HEADS):
            group(kh, False)


def _t5_bucket(dist):
    d = jnp.maximum(dist, 0)
    large = MAX_EXACT + (jnp.log(jnp.maximum(d, 1).astype(F32) / MAX_EXACT)
                         / math.log(MAX_DISTANCE / MAX_EXACT) * (N_BUCKETS - MAX_EXACT)).astype(jnp.int32)
    return jnp.where(d < MAX_EXACT, d, jnp.minimum(large, N_BUCKETS - 1))


def _attn_mix(qkvg, nq, rel_bias, q_g, k_g, sinks, bn, tn_, cache_k, cache_v, og_dtype):
    m = qkvg.shape[0]
    prompt = cache_k is None
    tq = WINDOW if prompt else tn_
    nb = tn_ // tq
    tk = WINDOW + tq
    col = jnp.arange(tk)[None, :]
    dist = WINDOW + jnp.arange(tq)[:, None] - col
    onehot = (_t5_bucket(dist)[None] == jnp.arange(N_BUCKETS)[:, None, None]).astype(F32)
    bias = jnp.einsum("bh,bqk->hqk", rel_bias.astype(F32), onehot, precision=lax.Precision.HIGHEST)
    bias = jnp.where(((dist >= 0) & (dist < WINDOW))[None], bias, -jnp.inf)
    bias = jnp.where((col == 0)[None], sinks.astype(F32)[:, None, None], bias)
    nkv = A_KV_HEADS * HEAD
    row = lambda b, i: b * nb + i
    kcol = nq // nkv
    gcol = kcol + 2
    win_shape = (1, WINDOW, A_KV_HEADS, HEAD)
    if prompt:
        kp_arr, vp_arr = qkvg, qkvg
        kp_spec = pl.BlockSpec((WINDOW, nkv), lambda b, i: (jnp.maximum(row(b, i) - 1, 0), kcol))
        vp_spec = pl.BlockSpec((WINDOW, nkv), lambda b, i: (jnp.maximum(row(b, i) - 1, 0), kcol + 1))
    else:
        kp_arr, vp_arr = cache_k, cache_v
        kp_spec = pl.BlockSpec(win_shape, lambda b, i: (b, 0, 0, 0))
        vp_spec = pl.BlockSpec(win_shape, lambda b, i: (b, 0, 0, 0))

    def gate_spec(kh):
        return pl.BlockSpec((tq, nkv), lambda b, i: (row(b, i), gcol + kh))

    og, kwin, vwin = pl.pallas_call(
        functools.partial(_attn_kernel, tq=tq, prompt=prompt, nblocks=nb),
        grid=(bn, nb),
        in_specs=[
            pl.BlockSpec((tq, nq), lambda b, i: (row(b, i), 0)),
            pl.BlockSpec((tq, nkv), lambda b, i: (row(b, i), kcol)),
            pl.BlockSpec((tq, nkv), lambda b, i: (row(b, i), kcol + 1)),
            kp_spec, vp_spec,
            *[gate_spec(kh) for kh in range(A_KV_HEADS)],
            pl.BlockSpec((nq // HEAD, tq, tk), lambda b, i: (0, 0, 0)),
            pl.BlockSpec((1, nkv), lambda b, i: (0, 0)),
            pl.BlockSpec((1, nkv), lambda b, i: (0, 0)),
        ],
        out_specs=[
            pl.BlockSpec((tq, nq), lambda b, i: (row(b, i), 0)),
            pl.BlockSpec(win_shape, lambda b, i: (b, 0, 0, 0)),
            pl.BlockSpec(win_shape, lambda b, i: (b, 0, 0, 0)),
        ],
        out_shape=[
            jax.ShapeDtypeStruct((m, nq), og_dtype),
            jax.ShapeDtypeStruct((bn,) + win_shape[1:], F32),
            jax.ShapeDtypeStruct((bn,) + win_shape[1:], F32),
        ],
        scratch_shapes=[
            pltpu.VMEM((tk, nkv), F32), pltpu.VMEM((tk, nkv), F32),
            pltpu.VMEM((A_GROUP * tq, tk), F32),
            pltpu.VMEM((A_GROUP * tq, tk), BF16 if tq % 16 == 0 else F32),
            pltpu.VMEM((8, LANES) if prompt else (tq, nq), F32),
        ],
        compiler_params=_params(("parallel", "arbitrary")),
        name="swa_attention",
    )(qkvg, qkvg, qkvg, kp_arr, vp_arr, *([qkvg] * A_KV_HEADS), bias,
      jnp.tile(q_g.astype(F32), A_GROUP).reshape(1, nkv), jnp.tile(k_g.astype(F32), A_KV_HEADS).reshape(1, nkv))
    return og, kwin, vwin


def _attn_layer(x2d, bn, tn_, g, j, w3, q_g, k_g, sinks, w_out, rel_bias, cache_k, cache_v, tm, tn_cap, og_dtype):
    nq = w_out.shape[1]
    nall = w3.shape[2]
    qkvg = _proj(x2d, g, w3, 0, nall, tm, _pick_tn(nall, tn_cap), og_dtype, w_base=j)[0]
    og, kwin, vwin = _attn_mix(qkvg, nq, rel_bias, q_g, k_g, sinks, bn, tn_, cache_k, cache_v, og_dtype)
    y = _outproj(og, w_out, j, x2d, tm, OUT_TN)
    return y, kwin, vwin


MXU_TILE = 256


def _head_sum_mats():
    jr = lax.broadcasted_iota(jnp.int32, (MXU_TILE, MXU_TILE), 0)
    jc = lax.broadcasted_iota(jnp.int32, (MXU_TILE, MXU_TILE), 1)
    ones_bd = ((jr // HEAD) == (jc // HEAD)).astype(F32)
    return ones_bd, ones_bd * (1.0 / HEAD)


def _per_slab(fn, x):
    return jnp.concatenate([fn(x[:, p * MXU_TILE:(p + 1) * MXU_TILE]) for p in range(x.shape[1] // MXU_TILE)],
                           axis=-1)


def _rwkv_token_terms(k_ref, lw_ref, la_ref, wup_ref, vec_ref, kk_scr, k2_scr, a_scr, ld_scr):
    vec = vec_ref[...]
    w0, a0, kk_w, ka_w = (vec[i:i + 1, :] for i in range(4))
    ones_bd, _ = _head_sum_mats()
    k = k_ref[0].astype(F32)
    xw = w0 + _mm(jnp.tanh(lw_ref[0]), wup_ref[0], NN, 3)
    ld_scr[...] = -math.exp(-0.5) * jax.nn.sigmoid(xw)
    a = jax.nn.sigmoid(a0 + _mm(la_ref[0], wup_ref[1], NN, 3))
    a_scr[...] = a
    kkr = k * kk_w
    ssq = _per_slab(lambda x: _mm(x, ones_bd, NN, STAT_PASSES), kkr * kkr)
    kk_scr[...] = kkr * lax.rsqrt(jnp.maximum(ssq, 1e-24))
    k2_scr[...] = k * (1.0 + (a - 1.0) * ka_w)


def _rwkv_finish(o_scr, r_ref, v_ref, g_ref, k2_scr, vec_ref, og_ref):
    vec = vec_ref[...]
    rk_w, ln_g, ln_b = (vec[i:i + 1, :] for i in range(4, 7))
    ones_bd, mean_bd = _head_sum_mats()
    o = o_scr[...]
    v = v_ref[0].astype(F32)
    mean = _per_slab(lambda x: _mm(x, mean_bd, NN, STAT_PASSES), o)
    dlt = o - mean
    var = _per_slab(lambda x: _mm(x, mean_bd, NN, STAT_PASSES), dlt * dlt)
    on = dlt * lax.rsqrt(var + B_LN_EPS) * ln_g + ln_b
    bonus = _per_slab(lambda x: _mm(x, ones_bd, NN, STAT_PASSES), r_ref[0].astype(F32) * k2_scr[...] * rk_w)
    on = on + bonus * v
    og_ref[...] = (on * _silu(g_ref[0].astype(F32))).astype(og_ref.dtype)


def _rwkv_kernel(r_ref, k_ref, v_ref, g_ref, lw_ref, la_ref, wup_ref, vec_ref, s0_ref,
                 og_ref, sout_ref,
                 s_ref, kk_scr, k2_scr, a_scr, ld_scr, q_scr, o0_scr, gp_scr, z_scr, ec_scr, o_scr,
                 *, C, tb, lp, nt, unroll, nseq):
    t = pl.program_id(2)
    n = 2 * C
    nchunks = tb // C
    lane = lax.broadcasted_iota(jnp.int32, (1, LANES), 1)
    lo = (lane < HEAD).astype(F32)
    hi = 1.0 - lo

    @pl.when(t == 0)
    def _():
        z = jnp.zeros((HEAD, HEAD), F32)
        for si in range(nseq):
            for pp in range(lp):
                s_ref[si * lp + pp, 0:HEAD, :] = jnp.concatenate([s0_ref[si, 2 * pp], z], axis=-1)
                s_ref[si * lp + pp, HEAD:LANES, :] = jnp.concatenate([z, s0_ref[si, 2 * pp + 1]], axis=-1)

    _rwkv_token_terms(k_ref, lw_ref, la_ref, wup_ref, vec_ref, kk_scr, k2_scr, a_scr, ld_scr)

    rr = lax.broadcasted_iota(jnp.int32, (n, n), 0)
    cc = lax.broadcasted_iota(jnp.int32, (n, n), 1)
    strict = cc < rr
    incl = cc <= rr
    eye = (cc == rr).astype(F32)
    tri = _tri(C)
    nsteps = int(math.log2(C)) - 1

    def stage_b(gi, carry):
        units = []
        for cu in range(unroll):
            ci = gi * unroll + cu
            rows = pl.ds(pl.multiple_of(ci * C, C), C)
            ld = ld_scr[rows, :]
            b = _mm_exact_lhs(tri, ld)
            eb = jnp.exp(b)
            enb = jnp.exp(-b)
            kk = kk_scr[rows, :]
            al = -kk * jnp.exp(b - ld)
            be = kk * a_scr[rows, :] * enb
            kt = k2_scr[rows, :] * enb
            rb = r_ref[0, rows, :].astype(F32) * eb
            v = v_ref[0, rows, :].astype(F32)
            e_c = eb[C - 1:C, :]
            ec_scr[ci] = jnp.broadcast_to(e_c, (SUBLANES, e_c.shape[1]))
            for pp in range(lp):
                ls = slice(pp * LANES, (pp + 1) * LANES)
                cat = lambda x: jnp.concatenate([x[:, ls] * lo, x[:, ls] * hi], axis=0)
                units.append(dict(idx=ci * lp + pp, ecp=e_c[:, ls], la=cat(al), lr=cat(rb), rb=cat(be),
                                  rk=cat(kt), vb=cat(v)))
        nu = range(len(units))
        if n % LANES == 0:
            prod = [_mm(jnp.concatenate([u["la"], u["lr"]], axis=0),
                        jnp.concatenate([u["rb"], u["rk"]], axis=0), NT) for u in units]
            m_ab = [jnp.where(strict, x[0:n, 0:n], 0.0) for x in prod]
            m_ak = [jnp.where(strict, x[0:n, n:2 * n], 0.0) for x in prod]
            m_rb = [jnp.where(incl, x[n:2 * n, 0:n], 0.0) for x in prod]
            m_rk = [jnp.where(incl, x[n:2 * n, n:2 * n], 0.0) for x in prod]
            akv = [_mm(m_ak[i], units[i]["vb"]) for i in nu]
            tinv = [eye + a_ for a_ in m_ab]
            apow = [_mm(a_, a_) for a_ in m_ab]
            for step in range(nsteps):
                if step < nsteps - 1:
                    both = [_mm(apow[i], jnp.concatenate([tinv[i], apow[i]], axis=1)) for i in nu]
                    tinv = [tinv[i] + both[i][:, 0:n] for i in nu]
                    apow = [x[:, n:2 * n] for x in both]
                else:
                    tinv = [tinv[i] + _mm(apow[i], tinv[i]) for i in nu]
            wu = [_mm(tinv[i], jnp.concatenate([units[i]["la"], akv[i]], axis=1)) for i in nu]
            w = [x[:, 0:LANES] for x in wu]
            uv = [jnp.concatenate([wu[i][:, LANES:2 * LANES], units[i]["vb"]], axis=0) for i in nu]
            rbe = [u["rb"] * u["ecp"] for u in units]
            for i, u in enumerate(units):
                q_scr[u["idx"]] = u["lr"] + _mm(m_rb[i], w[i])
            for i, u in enumerate(units):
                o0_scr[u["idx"]] = _mm(jnp.concatenate([m_rb[i], m_rk[i]], axis=1), uv[i])
            for i, u in enumerate(units):
                gp_scr[u["idx"]] = _mm(w[i], rbe[i], TN)
            for i, u in enumerate(units):
                z_scr[u["idx"]] = _mm(uv[i], jnp.concatenate([rbe[i], u["rk"] * u["ecp"]], axis=0), TN)
            return carry
        each = lambda fn: [fn(u) for u in units]
        m_ab = each(lambda u: jnp.where(strict, _mm(u["la"], u["rb"], NT), 0.0))
        m_ak = each(lambda u: jnp.where(strict, _mm(u["la"], u["rk"], NT), 0.0))
        m_rb = each(lambda u: jnp.where(incl, _mm(u["lr"], u["rb"], NT), 0.0))
        m_rk = each(lambda u: jnp.where(incl, _mm(u["lr"], u["rk"], NT), 0.0))
        akv = [_mm(m, u["vb"]) for m, u in zip(m_ak, units)]
        rkv = [_mm(m, u["vb"]) for m, u in zip(m_rk, units)]
        apow = m_ab
        tinv = [eye + a_ for a_ in apow]
        for _ in range(nsteps):
            apow = [_mm(a_, a_) for a_ in apow]
            tinv = [t_ + _mm(a_, t_) for a_, t_ in zip(apow, tinv)]
        w = [_mm(t_, u["la"]) for t_, u in zip(tinv, units)]
        u0 = [_mm(t_, x_) for t_, x_ in zip(tinv, akv)]
        rbe = [u["rb"] * u["ecp"] for u in units]
        for i, u in enumerate(units):
            q_scr[u["idx"]] = u["lr"] + _mm(m_rb[i], w[i])
        for i, u in enumerate(units):
            o0_scr[u["idx"]] = _mm(m_rb[i], u0[i]) + rkv[i]
        for i, u in enumerate(units):
            gp_scr[u["idx"]] = _mm(w[i], rbe[i], TN)
        for i, u in enumerate(units):
            z_scr[u["idx"]] = _mm(u0[i], rbe[i], TN) + _mm(u["vb"], u["rk"] * u["ecp"], TN)
        return carry

    lax.fori_loop(0, nchunks // unroll, stage_b, 0)

    for ci in range(nchunks):
        for pp in range(lp):
            ls = slice(pp * LANES, (pp + 1) * LANES)
            idx = ci * lp + pp
            sidx = idx if nseq > 1 else pp
            s = s_ref[sidx]
            o_bd = _mm(q_scr[idx], s, NT) + o0_scr[idx]
            o_scr[ci * C:(ci + 1) * C, ls] = o_bd[0:C] + o_bd[C:n]
            s_ref[sidx] = s * ec_scr[ci, 0:1, ls] + _mm(s, gp_scr[idx]) + z_scr[idx]

    _rwkv_finish(o_scr, r_ref, v_ref, g_ref, k2_scr, vec_ref, og_ref)

    @pl.when(t == nt - 1)
    def _():
        for si in range(nseq):
            for pp in range(lp):
                sout_ref[si, 2 * pp] = s_ref[si * lp + pp, 0:HEAD, 0:HEAD]
                sout_ref[si, 2 * pp + 1] = s_ref[si * lp + pp, HEAD:LANES, HEAD:LANES]


def _rwkv_mix(rkvg, lora, wup, vec, s0, bn, tn_, C, tb, lp, unroll, nseq, og_dtype):
    m = rkvg.shape[1]
    nch = rkvg.shape[2]
    wl = lp * LANES
    ngrp = nch // wl
    nt = tn_ * nseq // tb
    nchunks = tb // C
    rank = lora.shape[2]
    nh = s0.shape[1]
    row = lambda b, t: b * nt + t

    def xspec(c):
        return pl.BlockSpec((1, tb, wl), lambda b, p, t: (c, row(b, t), p))

    def lspec(c):
        return pl.BlockSpec((1, tb, rank), lambda b, p, t: (c, row(b, t), 0))

    blk = [pltpu.VMEM((tb, wl), F32)]
    scratch = ([pltpu.VMEM((nseq * lp, LANES, LANES), F32)] + blk * 4
               + [pltpu.VMEM((nchunks * lp, 2 * C, LANES), F32), pltpu.VMEM((nchunks * lp, 2 * C, LANES), F32),
                  pltpu.VMEM((nchunks * lp, LANES, LANES), F32), pltpu.VMEM((nchunks * lp, LANES, LANES), F32),
                  pltpu.VMEM((nchunks, SUBLANES, wl), F32)] + blk)
    og, sout = pl.pallas_call(
        functools.partial(_rwkv_kernel, C=C, tb=tb, lp=lp, nt=nt, unroll=unroll, nseq=nseq),
        grid=(bn // nseq, ngrp, nt),
        in_specs=[
            xspec(0), xspec(1), xspec(2), xspec(3), lspec(0), lspec(1),
            pl.BlockSpec((2, rank, wl), lambda b, p, t: (0, 0, p)),
            pl.BlockSpec((8, wl), lambda b, p, t: (0, p)),
            pl.BlockSpec((nseq, 2 * lp, HEAD, HEAD), lambda b, p, t: (b, p, 0, 0)),
        ],
        out_specs=[
            pl.BlockSpec((tb, wl), lambda b, p, t: (row(b, t), p)),
            pl.BlockSpec((nseq, 2 * lp, HEAD, HEAD), lambda b, p, t: (b, p, 0, 0)),
        ],
        out_shape=[
            jax.ShapeDtypeStruct((m, nch), og_dtype),
            jax.ShapeDtypeStruct((bn, nh, HEAD, HEAD), F32),
        ],
        scratch_shapes=scratch,
        compiler_params=_params(("parallel", "parallel", "arbitrary")),
        name="rwkv7_chunked",
    )(rkvg, rkvg, rkvg, rkvg, lora, lora, wup, vec, s0)
    return og, sout


def _rwkv_layer(x2d, bn, tn_, g, j, shift, s0, mu, w_rkvg, w_down, w_up, w0, a0, k_k, k_a, r_k, ln_g, ln_b,
                w_out, tm, tn_cap, C, tb, lp, unroll, nseq, og_dtype):
    m, d = x2d.shape
    nblk = m // tm
    if tm <= tn_:
        starts = jnp.arange(nblk) * tm
        prev = _rms_rows(x2d[jnp.maximum(starts - 1, 0)], g)
        first = jnp.where((starts % tn_ == 0)[:, None], shift[starts // tn_], prev).reshape(nblk, 1, d)
        period = None
    else:
        assert nblk == 1
        first = jnp.zeros((bn, tn_, d), F32).at[:, 0].set(shift).reshape(1, m, d)
        period = tn_
    shift_out = _rms_rows(x2d[tn_ - 1::tn_], g)
    nch = w_rkvg.shape[2]
    rkvg = _proj(x2d, g, w_rkvg, 0, nch, tm, _pick_tn(nch, tn_cap), og_dtype, first=first, mu=mu[:4], period=period,
                 w_base=4 * j, nc=4)
    rank = w_down.shape[2]
    lora = _proj(x2d, g, w_down, 0, rank, tm, rank, F32, first=first, mu=mu[4:6], period=period, w_base=2 * j, nc=2)
    vec = jnp.stack([w0, a0, k_k, k_a, r_k.reshape(nch), ln_g, ln_b, jnp.zeros((nch,), F32)])
    og, sout = _rwkv_mix(rkvg, lora, w_up, vec, s0, bn, tn_, C, tb, lp, unroll, nseq, og_dtype)
    y = _outproj(og, w_out, j, x2d, tm, OUT_TN)
    return y, sout, shift_out


GLA_SUB = 16


def _gla_kernel(q_ref, k_ref, v_ref, gate_ref, gl_ref, wup_ref, bgk_ref, og_w_ref, s0_ref,
                og_ref, sout_ref, st_ref, z_scr, qd_scr, o_scr, ec_scr, *, C, tb, nt, nseq):
    t = pl.program_id(2)
    dk = q_ref.shape[-1]
    nchunks = tb // C
    sub = min(GLA_SUB, C)
    nsub = C // sub
    qscale = dk ** -0.5

    @pl.when(t == 0)
    def _():
        for si in range(nseq):
            st_ref[si] = s0_ref[si, 0]

    tri = _tri(C)
    glog_all = -_softplus(-(_mm(gl_ref[...], wup_ref[...], NN, 3) + bgk_ref[...])) * (1.0 / C_GATE_NORM)
    chunks = range(nchunks)
    rows = [slice(c * C, (c + 1) * C) for c in chunks]
    glog = [glog_all[rs] for rs in rows]
    b = [_mm_exact_lhs(tri, gl_) for gl_ in glog]
    q = [q_ref[rs, :].astype(F32) * qscale for rs in rows]
    k = [k_ref[rs, :].astype(F32) for rs in rows]
    v = [v_ref[rs, :].astype(F32) for rs in rows]
    for c in chunks:
        bl = b[c][C - 1:C, :]
        qd_scr[rows[c], :] = q[c] * jnp.exp(b[c])
        ec_scr[c] = jnp.broadcast_to(jnp.exp(bl), (SUBLANES, dk))
        z_scr[c] = _mm(k[c] * jnp.exp(bl - b[c]), v[c], TN)
    parts = [[] for _ in chunks]
    for i in range(nsub):
        r0 = i * sub
        nk = r0 + sub
        ar = lax.broadcasted_iota(jnp.int32, (sub, nk), 0) + r0
        ac = lax.broadcasted_iota(jnp.int32, (sub, nk), 1)
        att = []
        for c in chunks:
            ref = b[c][r0:r0 + 1, :] - glog[c][r0:r0 + 1, :]
            qi = q[c][r0:nk] * jnp.exp(b[c][r0:nk] - ref)
            ki = k[c][0:nk] * jnp.exp(ref - b[c][0:nk])
            att.append(jnp.where(ac <= ar, _mm(qi, ki, NT), 0.0))
        for c in chunks:
            parts[c].append(_mm(att[c], v[c][0:nk]))
    for c in chunks:
        o_scr[rows[c], :] = jnp.concatenate(parts[c], axis=0) if nsub > 1 else parts[c][0]

    for c in chunks:
        si = c if nseq > 1 else 0
        st = st_ref[si]
        o_scr[rows[c], :] = o_scr[rows[c], :] + _mm(qd_scr[rows[c], :], st)
        st_ref[si] = st * jnp.transpose(ec_scr[c])[:, 0:1] + z_scr[c]

    o = o_scr[...]
    on = o * lax.rsqrt(jnp.mean(o * o, axis=-1, keepdims=True) + NORM_EPS) * og_w_ref[...]
    og_ref[...] = (on * _silu(gate_ref[...].astype(F32))).astype(og_ref.dtype)

    @pl.when(t == nt - 1)
    def _():
        for si in range(nseq):
            sout_ref[si, 0] = st_ref[si]


def _gla_mix(qkvg, gl, w_up, b_gk, o_g, s0, bn, tn_, C, tb, nseq, og_dtype):
    m = qkvg.shape[0]
    nh, dk, dv = s0.shape[1:]
    nt = tn_ * nseq // tb
    rank = gl.shape[1]
    row = lambda b, t: b * nt + t
    kcol = nh
    vcol = 2 * nh * dk // dv
    gcol = vcol + nh
    og, sout = pl.pallas_call(
        functools.partial(_gla_kernel, C=C, tb=tb, nt=nt, nseq=nseq),
        grid=(bn // nseq, nh, nt),
        in_specs=[
            pl.BlockSpec((tb, dk), lambda b, h, t: (row(b, t), h)),
            pl.BlockSpec((tb, dk), lambda b, h, t: (row(b, t), kcol + h)),
            pl.BlockSpec((tb, dv), lambda b, h, t: (row(b, t), vcol + h)),
            pl.BlockSpec((tb, dv), lambda b, h, t: (row(b, t), gcol + h)),
            pl.BlockSpec((tb, rank), lambda b, h, t: (row(b, t), 0)),
            pl.BlockSpec((rank, dk), lambda b, h, t: (0, h)),
            pl.BlockSpec((1, dk), lambda b, h, t: (0, h)),
            pl.BlockSpec((1, dv), lambda b, h, t: (0, 0)),
            pl.BlockSpec((nseq, 1, dk, dv), lambda b, h, t: (b, h, 0, 0)),
        ],
        out_specs=[
            pl.BlockSpec((tb, dv), lambda b, h, t: (row(b, t), h)),
            pl.BlockSpec((nseq, 1, dk, dv), lambda b, h, t: (b, h, 0, 0)),
        ],
        out_shape=[
            jax.ShapeDtypeStruct((m, nh * dv), og_dtype),
            jax.ShapeDtypeStruct((bn, nh, dk, dv), F32),
        ],
        scratch_shapes=[
            pltpu.VMEM((nseq, dk, dv), F32),
            pltpu.VMEM((tb // C, dk, dv), F32),
            pltpu.VMEM((tb, dk), F32),
            pltpu.VMEM((tb, dv), F32),
            pltpu.VMEM((tb // C, SUBLANES, dk), F32),
        ],
        compiler_params=_params(("parallel", "parallel", "arbitrary")),
        name="gla_chunked",
    )(qkvg, qkvg, qkvg, qkvg, gl, w_up, b_gk.reshape(1, -1), o_g.reshape(1, -1), s0)
    return og, sout


def _gla_layer(x2d, bn, tn_, g, j, s0, w3, w_gl, w_up, b_gk, o_g, w_out, tm, tn_cap, C, tb, nseq, og_dtype):
    nh, dk, dv = s0.shape[1:]
    nfused = 2 * nh * dk + 2 * nh * dv
    rank = w_up.shape[0]
    qkvg = _proj(x2d, g, w3, 0, nfused, tm, _pick_tn(nfused, tn_cap), og_dtype, w_base=j)[0]
    gl = _proj(x2d, g, w_gl, 0, rank, tm, rank, F32, w_base=j)[0]
    og, sout = _gla_mix(qkvg, gl, w_up, b_gk, o_g, s0, bn, tn_, C, tb, nseq, og_dtype)
    y = _outproj(og, w_out, j, x2d, tm, OUT_TN)
    return y, sout


PROMPT_TM = 1024
RWKV_CHUNK = 64
RWKV_TB = 512
RWKV_PAIRS = 4
RWKV_UNROLL = 4
STAT_PASSES = 1
GLA_CHUNK = 64
GLA_TB = 1024
PROMPT_TN = 2048
LERP_TN = 1024
SAMPLE_TN = 2048
SAMPLE_SEQS = 16
GLA_SAMPLE_SEQS = 8


def kernel(x_prompt, x_sample, cache_k_win, cache_v_win, state_wkv, state_shift, state_gla, norm_g, rel_bias, w_in_a, q_norm_g, k_norm_g, sinks, w_out_a, mu_b, w_rkvg_b, w_lora_down_b, w_lora_up_b, w0_b, a0_b, k_k_b, k_a_b, r_k_b, ln_x_g_b, ln_x_b_b, w_out_b, w_in_c, w_gk_up_c, b_gk_c, o_norm_g_c, w_out_c):
    bp, tp, d = x_prompt.shape
    bs, ts, _ = x_sample.shape
    depth = norm_g.shape[0]
    bf = lambda w: w.astype(BF16)
    nfused_c = w_in_c.shape[2] - w_gk_up_c.shape[1]
    w_in_a, w_out_a, w_out_b, w_out_c = bf(w_in_a), bf(w_out_a), bf(w_out_b), bf(w_out_c)
    w_rkvg_b = bf(w_rkvg_b).reshape((-1,) + w_rkvg_b.shape[2:])
    w_lora_down_b = bf(w_lora_down_b).reshape((-1,) + w_lora_down_b.shape[2:])
    w_gl_c = bf(w_in_c[:, :, nfused_c:])
    w_in_c = bf(w_in_c)
    xp = x_prompt.reshape(bp * tp, d)
    xs = x_sample.reshape(bs * ts, d)
    ms = bs * ts
    kwp, vwp, kws, vws, wkvp, shp, wkvs, shs, glap, glas = ([] for _ in range(10))
    for layer in range(depth):
        kind, j = layer % 3, layer // 3
        g = norm_g[layer]
        if kind == 0:
            wa = (w_in_a, q_norm_g[j], k_norm_g[j], sinks[j], w_out_a, rel_bias)
            xp, kp_, vp_ = _attn_layer(xp, bp, tp, g, j, *wa, None, None, PROMPT_TM, PROMPT_TN, BF16)
            xs, ks_, vs_ = _attn_layer(xs, bs, ts, g, j, *wa, cache_k_win[j], cache_v_win[j], ms, SAMPLE_TN, F32)
            kwp.append(kp_); vwp.append(vp_); kws.append(ks_); vws.append(vs_)
        elif kind == 1:
            wb = (mu_b[j], w_rkvg_b, w_lora_down_b, w_lora_up_b[j], w0_b[j], a0_b[j], k_k_b[j], k_a_b[j],
                  r_k_b[j], ln_x_g_b[j], ln_x_b_b[j], w_out_b)
            nh = w_rkvg_b.shape[2] // HEAD
            xp, sp_, lp_ = _rwkv_layer(xp, bp, tp, g, j, jnp.zeros((bp, d), F32), jnp.zeros((bp, nh, HEAD, HEAD), F32),
                                       *wb, PROMPT_TM, LERP_TN, RWKV_CHUNK, RWKV_TB, RWKV_PAIRS, RWKV_UNROLL, 1, BF16)
            xs, ss_, ls_ = _rwkv_layer(xs, bs, ts, g, j, state_shift[j], state_wkv[j], *wb, ms, LERP_TN, ts,
                                       ts * SAMPLE_SEQS, RWKV_PAIRS, SAMPLE_SEQS, SAMPLE_SEQS, F32)
            wkvp.append(sp_); shp.append(lp_); wkvs.append(ss_); shs.append(ls_)
        else:
            wc = (w_in_c, w_gl_c, w_gk_up_c[j], b_gk_c[j], o_norm_g_c[j], w_out_c)
            xp, sp_ = _gla_layer(xp, bp, tp, g, j, jnp.zeros((bp,) + state_gla.shape[2:], F32), *wc,
                                 PROMPT_TM, PROMPT_TN, GLA_CHUNK, GLA_TB, 1, BF16)
            xs, ss_ = _gla_layer(xs, bs, ts, g, j, state_gla[j], *wc, ms, SAMPLE_TN, ts, ts * GLA_SAMPLE_SEQS,
                                 GLA_SAMPLE_SEQS, F32)
            glap.append(sp_); glas.append(ss_)
    return (xp.reshape(bp, tp, d), xs.reshape(bs, ts, d),
            jnp.stack(kwp), jnp.stack(vwp), jnp.stack(kws), jnp.stack(vws),
            jnp.stack(wkvp), jnp.stack(shp), jnp.stack(wkvs), jnp.stack(shs),
            jnp.stack(glap), jnp.stack(glas))
```
